```python
import jax, jax.numpy as jnp
from jax import lax
import numpy as np

D_MODEL = 2048
BATCH = 1
SEQ = 16384
DEPTH = 4
DEC_BATCH = 32
DEC_SEQ = 32
PAST_LEN = 2048

CHUNK = 64
WINDOW = 128
WIN_CHUNKS = WINDOW // CHUNK
N_A_LAYERS = DEPTH // 2
N_B_LAYERS = DEPTH - N_A_LAYERS
CONV_W = 3
D_FF = 4 * D_MODEL
HEAD_DIM = 64
N_HEADS = D_MODEL // HEAD_DIM
N_KV_HEADS = N_HEADS // 8
GROUP = N_HEADS // N_KV_HEADS
ROT_DIM = HEAD_DIM // 4
ROPE_THETA = 500000.0
EPS = 1e-6
SCALE = HEAD_DIM ** -0.5

kernel_name = 'yoco_shortconv_swa_sink_stream_step'


def rms_norm(x, g):
    xf = x.astype(jnp.float32)
    y = xf * lax.rsqrt(jnp.mean(xf * xf, axis=-1, keepdims=True) + EPS)
    return (y * g.astype(jnp.float32)).astype(x.dtype)


def rope_partial(x, pos):
    half = ROT_DIM // 2
    inv = ROPE_THETA ** (-jnp.arange(half, dtype=jnp.float32) / half)
    ang = pos.astype(jnp.float32)[:, None] * inv[None, :]
    cos = jnp.cos(ang)[None, :, None, :]
    sin = jnp.sin(ang)[None, :, None, :]
    xr = x[..., :ROT_DIM].astype(jnp.float32)
    x1, x2 = xr[..., :half], xr[..., half:]
    rot = jnp.concatenate([x1 * cos - x2 * sin, x2 * cos + x1 * sin], axis=-1)
    return jnp.concatenate([rot.astype(x.dtype), x[..., ROT_DIM:]], axis=-1)


def squared_relu_mlp(h, w_up, w_down):
    return jnp.square(jax.nn.relu(h @ w_up)) @ w_down


def short_conv_mixer(h, w_in, conv_w, w_out, prev):
    s = h.shape[1]
    gate_b, gate_c, u = jnp.split(h @ w_in, 3, axis=-1)
    z = gate_c * u
    zp = jnp.concatenate([prev.astype(z.dtype), z], axis=1)
    conv = zp[:, 0:s] * conv_w[0]
    for j in range(1, CONV_W):
        conv = conv + zp[:, j:j + s] * conv_w[j]
    return (gate_b * conv) @ w_out, zp[:, -(CONV_W - 1):]


def shared_kv(x, kv_norm_g, w_kv, k_norm_g, pos):
    b, s, _ = x.shape
    k, v = jnp.split(rms_norm(x, kv_norm_g) @ w_kv, 2, axis=-1)
    k = rope_partial(rms_norm(k.reshape(b, s, N_KV_HEADS, HEAD_DIM), k_norm_g), pos)
    return k, v.reshape(b, s, N_KV_HEADS, HEAD_DIM)


def queries(h, w_q, q_norm_g, pos):
    b, s, _ = h.shape
    q = (h @ w_q).reshape(b, s, N_HEADS, HEAD_DIM)
    return rope_partial(rms_norm(q, q_norm_g), pos)


def sink_softmax(sc, sink, valid):
    sc = jnp.where(valid, sc, -jnp.inf)
    m = jnp.maximum(jnp.max(sc, axis=-1, keepdims=True), sink)
    e = jnp.exp(sc - m)
    return e / (jnp.sum(e, axis=-1, keepdims=True) + jnp.exp(sink - m))


def window_attn_prompt(q, k, v, sinks):
    b, s = q.shape[:2]
    nc = s // CHUNK
    qc = q.reshape(b, nc, CHUNK, N_KV_HEADS, GROUP, HEAD_DIM)

    def band(t):
        tc = t.reshape(b, nc, CHUNK, N_KV_HEADS, HEAD_DIM)
        pad = jnp.zeros((b, WIN_CHUNKS, CHUNK, N_KV_HEADS, HEAD_DIM), t.dtype)
        tp = jnp.concatenate([pad, tc], axis=1)
        return jnp.concatenate([tp[:, j:j + nc] for j in range(WIN_CHUNKS + 1)], axis=2)

    kb, vb = band(k), band(v)
    sc = jnp.einsum('bnqhgd,bnkhd->bnhgqk', qc, kb).astype(jnp.float32) * SCALE
    key_chunk = (jnp.arange(nc)[:, None] - WIN_CHUNKS
                 + jnp.repeat(jnp.arange(WIN_CHUNKS + 1), CHUNK)[None, :])
    valid = (key_chunk >= 0)[None, :, None, None, None, :]
    sink = sinks.astype(jnp.float32).reshape(N_KV_HEADS, GROUP)[None, None, :, :, None, None]
    p = sink_softmax(sc, sink, valid).astype(vb.dtype)
    o = jnp.einsum('bnhgqk,bnkhd->bnqhgd', p, vb)
    return o.reshape(b, s, N_HEADS * HEAD_DIM)


def window_attn_sample(q, k_all, v_all, sinks):
    b, s = q.shape[:2]
    qg = q.reshape(b, s, N_KV_HEADS, GROUP, HEAD_DIM)
    sc = jnp.einsum('bqhgd,bkhd->bhgqk', qg, k_all).astype(jnp.float32) * SCALE
    sink = sinks.astype(jnp.float32).reshape(N_KV_HEADS, GROUP)[None, :, :, None, None]
    p = sink_softmax(sc, sink, True).astype(v_all.dtype)
    o = jnp.einsum('bhgqk,bkhd->bqhgd', p, v_all)
    return o.reshape(b, s, N_HEADS * HEAD_DIM)


def setup_inputs(seed: int = 0) -> dict:
    key = jax.random.key(seed)
    ks = jax.random.split(key, 19)
    f32 = jnp.float32

    def nrm(k, shape, scale):
        return jax.random.normal(k, shape, f32) * scale

    def gain(k, shape):
        return 1.0 + 0.05 * jax.random.normal(k, shape, f32)

    return {
        'x_prompt': nrm(ks[0], (BATCH, SEQ, D_MODEL), 1.0),
        'x_sample': nrm(ks[1], (DEC_BATCH, DEC_SEQ, D_MODEL), 1.0),
        'state_conv': nrm(ks[2], (N_A_LAYERS, DEC_BATCH, CONV_W - 1, D_MODEL), 1.0),
        'cache_k': nrm(ks[3], (DEC_BATCH, WINDOW, N_KV_HEADS, HEAD_DIM), 1.0),
        'cache_v': nrm(ks[4], (DEC_BATCH, WINDOW, N_KV_HEADS, HEAD_DIM), 1.0),
        'mix_norm_g': gain(ks[5], (DEPTH, D_MODEL)),
        'mlp_norm_g': gain(ks[6], (DEPTH, D_MODEL)),
        'w_up': nrm(ks[7], (DEPTH, D_MODEL, D_FF), D_MODEL ** -0.5),
        'w_down': nrm(ks[8], (DEPTH, D_FF, D_MODEL), D_FF ** -0.5),
        'conv_w_in': nrm(ks[9], (N_A_LAYERS, D_MODEL, 3 * D_MODEL), D_MODEL ** -0.5),
        'conv_w': nrm(ks[10], (N_A_LAYERS, CONV_W, D_MODEL), CONV_W ** -0.5),
        'conv_w_out': nrm(ks[11], (N_A_LAYERS, D_MODEL, D_MODEL), D_MODEL ** -0.5),
        'kv_norm_g': gain(ks[12], (D_MODEL,)),
        'w_kv': nrm(ks[13], (D_MODEL, 2 * N_KV_HEADS * HEAD_DIM), D_MODEL ** -0.5),
        'k_norm_g': gain(ks[14], (HEAD_DIM,)),
        'w_q': nrm(ks[15], (N_B_LAYERS, D_MODEL, N_HEADS * HEAD_DIM), D_MODEL ** -0.5),
        'q_norm_g': gain(ks[16], (N_B_LAYERS, HEAD_DIM)),
        'sinks': nrm(ks[17], (N_B_LAYERS, N_HEADS), 0.5),
        'w_o': nrm(ks[18], (N_B_LAYERS, N_HEADS * HEAD_DIM, D_MODEL), (N_HEADS * HEAD_DIM) ** -0.5),
    }


def reference(x_prompt, x_sample, state_conv, cache_k, cache_v,
              mix_norm_g, mlp_norm_g, w_up, w_down,
              conv_w_in, conv_w, conv_w_out,
              kv_norm_g, w_kv, k_norm_g,
              w_q, q_norm_g, sinks, w_o):
    pos_p = jnp.arange(x_prompt.shape[1])
    pos_s = PAST_LEN + jnp.arange(x_sample.shape[1])
    xp, xs = x_prompt, x_sample
    prev_p = jnp.zeros((x_prompt.shape[0], CONV_W - 1, D_MODEL), x_prompt.dtype)
    conv_p, conv_s = [], []
    for i in range(DEPTH):
        if i < N_A_LAYERS:
            yp, cp = short_conv_mixer(rms_norm(xp, mix_norm_g[i]), conv_w_in[i], conv_w[i],
                                      conv_w_out[i], prev_p)
            ys, cs = short_conv_mixer(rms_norm(xs, mix_norm_g[i]), conv_w_in[i], conv_w[i],
                                      conv_w_out[i], state_conv[i])
            conv_p.append(cp)
            conv_s.append(cs)
        else:
            if i == N_A_LAYERS:
                kp, vp = shared_kv(xp, kv_norm_g, w_kv, k_norm_g, pos_p)
                ks_new, vs_new = shared_kv(xs, kv_norm_g, w_kv, k_norm_g, pos_s)
                ks_all = jnp.concatenate([cache_k.astype(ks_new.dtype), ks_new], axis=1)
                vs_all = jnp.concatenate([cache_v.astype(vs_new.dtype), vs_new], axis=1)
            j = i - N_A_LAYERS
            qp = queries(rms_norm(xp, mix_norm_g[i]), w_q[j], q_norm_g[j], pos_p)
            yp = window_attn_prompt(qp, kp, vp, sinks[j]) @ w_o[j]
            qs = queries(rms_norm(xs, mix_norm_g[i]), w_q[j], q_norm_g[j], pos_s)
            ys = window_attn_sample(qs, ks_all, vs_all, sinks[j]) @ w_o[j]
        xp = xp + yp
        xs = xs + ys
        xp = xp + squared_relu_mlp(rms_norm(xp, mlp_norm_g[i]), w_up[i], w_down[i])
        xs = xs + squared_relu_mlp(rms_norm(xs, mlp_norm_g[i]), w_up[i], w_down[i])
    return (xp, xs, jnp.stack(conv_p), jnp.stack(conv_s),
            kp[:, -WINDOW:], vp[:, -WINDOW:], ks_all[:, -WINDOW:], vs_all[:, -WINDOW:])
```

```python
import functools

import jax
import jax.numpy as jnp
from jax import lax
from jax.experimental import pallas as pl
from jax.experimental.pallas import tpu as pltpu

EPS = 1e-6
CHUNK = 64
WINDOW = 128
HEAD_DIM = 64
N_KV_HEADS = 4
GROUP = 8
ROT_DIM = 16
ROPE_THETA = 500000.0
PAST_LEN = 2048
SCALE = HEAD_DIM ** -0.5

LANES = 128
KEYS = 2 * WINDOW
VMEM_LIMIT_CAP = 56 * 2 ** 20

F32 = jnp.float32
BF16 = jnp.bfloat16


def _vmem_limit(nbytes):
    return int(min(VMEM_LIMIT_CAP, max(32 * 2 ** 20, nbytes * 5 // 4 + 4 * 2 ** 20)))


def _rms(x, g):
    return x * lax.rsqrt(jnp.mean(x * x, axis=-1, keepdims=True) + EPS) * g


def _half_mask(shape):
    return lax.broadcasted_iota(jnp.int32, shape, len(shape) - 1) < HEAD_DIM


def _head_norm_rope(t, gain, cos, sneg, spos):
    lo = _half_mask(t.shape)
    sq = t * t
    s_lo = jnp.sum(jnp.where(lo, sq, 0.0), axis=-1, keepdims=True)
    s_hi = jnp.sum(jnp.where(lo, 0.0, sq), axis=-1, keepdims=True)
    inv = jnp.where(lo, lax.rsqrt(s_lo / HEAD_DIM + EPS), lax.rsqrt(s_hi / HEAD_DIM + EPS))
    tn = t * inv * gain
    half = ROT_DIM // 2
    return tn * cos + pltpu.roll(tn, LANES - half, 1) * sneg + pltpu.roll(tn, half, 1) * spos


def _mlp_body(x_ref, g_ref, wu_ref, wd_ref, o_ref, xn_ref):
    @pl.when(pl.program_id(1) == 0)
    def _():
        x = x_ref[...]
        xn_ref[...] = _rms(x, g_ref[...]).astype(BF16)
        o_ref[...] = x

    h = jnp.dot(xn_ref[...], wu_ref[...], preferred_element_type=F32)
    h = jnp.square(jnp.maximum(h, 0.0)).astype(BF16)
    o_ref[...] += jnp.dot(h, wd_ref[...], preferred_element_type=F32)


def _mlp(x, g, w_up, w_down, *, tm, tf):
    t, d = x.shape
    f = w_up.shape[1]
    tm, tf = min(tm, t), min(tf, f)
    nbytes = 2 * (2 * tm * d * 4 + 2 * d * tf * 2) + tm * d * 2 + tm * tf * 6
    return pl.pallas_call(
        _mlp_body,
        grid=(t // tm, f // tf),
        in_specs=[
            pl.BlockSpec((tm, d), lambda i, j: (i, 0)),
            pl.BlockSpec((1, d), lambda i, j: (0, 0)),
            pl.BlockSpec((d, tf), lambda i, j: (0, j)),
            pl.BlockSpec((tf, d), lambda i, j: (j, 0)),
        ],
        out_specs=pl.BlockSpec((tm, d), lambda i, j: (i, 0)),
        out_shape=jax.ShapeDtypeStruct((t, d), F32),
        scratch_shapes=[pltpu.VMEM((tm, d), BF16)],
        compiler_params=pltpu.CompilerParams(
            dimension_semantics=("arbitrary", "arbitrary"), vmem_limit_bytes=_vmem_limit(nbytes)),
        name="mlp",
    )(x, g, w_up, w_down)


def _conv_body(*refs, seg, tn):
    if seg is None:
        x_ref, g_ref, win_ref, cw_ref, wout_ref, o_ref, zl_ref, xn_ref = refs
    else:
        x_ref, g_ref, win_ref, cw_ref, wout_ref, st_ref, o_ref, zl_ref, xn_ref = refs
    i, j = pl.program_id(0), pl.program_id(1)

    @pl.when(j == 0)
    def _():
        x = x_ref[...]
        xn_ref[...] = _rms(x, g_ref[...]).astype(BF16)
        o_ref[...] = x

    bcu = jnp.dot(xn_ref[...], win_ref[...], preferred_element_type=F32)
    gate_b = bcu[:, :tn]
    z = bcu[:, tn:2 * tn] * bcu[:, 2 * tn:]
    tm = z.shape[0]
    row = lax.broadcasted_iota(jnp.int32, z.shape, 0)
    r1 = pltpu.roll(z, 1, 0)
    r2 = pltpu.roll(z, 2, 0)
    if seg is None:
        @pl.when(i == 0)
        def _():
            zl_ref[j] = jnp.zeros((2, tn), F32)

        prev = zl_ref[j]
        p0, p1 = prev[0:1, :], prev[1:2, :]
        zl_ref[j] = z[tm - 2:, :]
    else:
        nb = tm // seg
        st = st_ref[...]
        p0 = jnp.broadcast_to(st[:, 0:1, :], (nb, seg, tn)).reshape(tm, tn)
        p1 = jnp.broadcast_to(st[:, 1:2, :], (nb, seg, tn)).reshape(tm, tn)
        row = row % seg
        zl_ref[...] = z.reshape(nb, seg, tn)[:, seg - 2:, :]
    zp1 = jnp.where(row == 0, p1, r1)
    zp2 = jnp.where(row == 0, p0, jnp.where(row == 1, p1, r2))
    cw = cw_ref[...]
    conv = zp2 * cw[0:1, :] + zp1 * cw[1:2, :] + z * cw[2:3, :]
    y = (gate_b * conv).astype(BF16)
    o_ref[...] += jnp.dot(y, wout_ref[...], preferred_element_type=F32)


def _conv_mixer(x, g, w_in_r, cw, w_out, state, *, tm, tn, seg):
    t, d = x.shape
    tm = min(tm, t)
    nj = d // tn
    in_specs = [
        pl.BlockSpec((tm, d), lambda i, j: (i, 0)),
        pl.BlockSpec((1, d), lambda i, j: (0, 0)),
        pl.BlockSpec((d, 3 * tn), lambda i, j: (0, j)),
        pl.BlockSpec((3, tn), lambda i, j: (0, j)),
        pl.BlockSpec((tn, d), lambda i, j: (j, 0)),
    ]
    args = [x, g, w_in_r, cw, w_out]
    if seg is None:
        zl_shape = (nj, 2, tn)
        zl_spec = pl.BlockSpec((nj, 2, tn), lambda i, j: (0, 0, 0))
    else:
        nb = tm // seg
        zl_shape = (t // seg, 2, d)
        zl_spec = pl.BlockSpec((nb, 2, tn), lambda i, j: (i, 0, j))
        in_specs.append(pl.BlockSpec((nb, 2, tn), lambda i, j: (i, 0, j)))
        args.append(state)
    nbytes = 2 * (2 * tm * d * 4 + d * 3 * tn * 2 + tn * d * 2) + tm * d * 2 + tm * tn * 4 * 8
    y, zl = pl.pallas_call(
        functools.partial(_conv_body, seg=seg, tn=tn),
        grid=(t // tm, nj),
        in_specs=in_specs,
        out_specs=[pl.BlockSpec((tm, d), lambda i, j: (i, 0)), zl_spec],
        out_shape=[jax.ShapeDtypeStruct((t, d), F32), jax.ShapeDtypeStruct(zl_shape, F32)],
        scratch_shapes=[pltpu.VMEM((tm, d), BF16)],
        compiler_params=pltpu.CompilerParams(
            dimension_semantics=("arbitrary", "arbitrary"), vmem_limit_bytes=_vmem_limit(nbytes)),
        name="conv_mixer",
    )(*args)
    if seg is None:
        zl = zl.transpose(1, 0, 2).reshape(1, 2, d)
    return y, zl


def _kv_body(x_ref, g_ref, wkv_ref, kg_ref, cos_ref, sneg_ref, spos_ref, k_ref, v_ref, k2_ref, v2_ref):
    xn = _rms(x_ref[...], g_ref[...]).astype(BF16)
    kv = jnp.dot(xn, wkv_ref[...], preferred_element_type=F32)
    nkv = N_KV_HEADS * HEAD_DIM
    lo = _half_mask((xn.shape[0], LANES))
    for p in range(nkv // LANES):
        sl = slice(p * LANES, (p + 1) * LANES)
        kr = _head_norm_rope(kv[:, sl], kg_ref[...], cos_ref[...], sneg_ref[...], spos_ref[...])
        vr = kv[:, nkv + p * LANES: nkv + (p + 1) * LANES]
        k_ref[:, sl] = kr
        v_ref[:, sl] = vr
        for src, dst in ((kr, k2_ref), (vr, v2_ref)):
            sw = pltpu.roll(src, HEAD_DIM, 1)
            dst[2 * p] = jnp.where(lo, src, sw).astype(BF16)
            dst[2 * p + 1] = jnp.where(lo, sw, src).astype(BF16)


def _shared_kv(x, g, w_kv, kg, rope, *, tm):
    t, d = x.shape
    tm = min(tm, t)
    nkv = N_KV_HEADS * HEAD_DIM
    row = lambda i: (i, 0)
    const = lambda i: (0, 0)
    nbytes = 2 * (tm * d * 4 + d * 2 * nkv * 2 + 5 * tm * nkv * 4) + tm * d * 8
    return pl.pallas_call(
        _kv_body,
        grid=(t // tm,),
        in_specs=[
            pl.BlockSpec((tm, d), row),
            pl.BlockSpec((1, d), const),
            pl.BlockSpec((d, 2 * nkv), const),
            pl.BlockSpec((1, LANES), const),
            pl.BlockSpec((tm, LANES), row),
            pl.BlockSpec((tm, LANES), row),
            pl.BlockSpec((tm, LANES), row),
        ],
        out_specs=[
            pl.BlockSpec((tm, nkv), row),
            pl.BlockSpec((tm, nkv), row),
            pl.BlockSpec((N_KV_HEADS, tm, LANES), lambda i: (0, i, 0)),
            pl.BlockSpec((N_KV_HEADS, tm, LANES), lambda i: (0, i, 0)),
        ],
        out_shape=[
            jax.ShapeDtypeStruct((t, nkv), F32),
            jax.ShapeDtypeStruct((t, nkv), F32),
            jax.ShapeDtypeStruct((N_KV_HEADS, t, LANES), BF16),
            jax.ShapeDtypeStruct((N_KV_HEADS, t, LANES), BF16),
        ],
        compiler_params=pltpu.CompilerParams(
            dimension_semantics=("arbitrary",), vmem_limit_bytes=_vmem_limit(nbytes)),
        name="shared_kv",
    )(x, g, w_kv, kg, *rope)


def _attend(qcat, kwin, vwin, valid, sink):
    s = lax.dot_general(qcat, kwin, (((1,), (1,)), ((), ())), preferred_element_type=F32)
    s = jnp.where(valid, s, -jnp.inf)
    m = jnp.maximum(jnp.max(s, axis=-1, keepdims=True), sink)
    e = jnp.exp(s - m)
    den = jnp.sum(e, axis=-1, keepdims=True) + jnp.exp(sink - m)
    o = jnp.dot(e.astype(BF16), vwin, preferred_element_type=F32)
    return o / den


def _project_q(x_ref, g_ref, wq_ref, qg_ref, cos_ref, sneg_ref, spos_ref, qe_ref, qo_ref):
    xn = _rms(x_ref[...], g_ref[...]).astype(BF16)
    q = jnp.dot(xn, wq_ref[...], preferred_element_type=F32)
    lo = _half_mask((q.shape[0], LANES))
    for p in range(q.shape[1] // LANES):
        sl = slice(p * LANES, (p + 1) * LANES)
        qr = _head_norm_rope(q[:, sl], qg_ref[...], cos_ref[...], sneg_ref[...], spos_ref[...]) * SCALE
        qe_ref[:, sl] = jnp.where(lo, qr, 0.0).astype(BF16)
        qo_ref[:, sl] = jnp.where(lo, 0.0, qr).astype(BF16)


def _attend_rows(qe_ref, qo_ref, att_ref, r0, nr, kh, kwin, vwin, valid, sink):
    pairs = GROUP // 2
    cols = [slice((pairs * kh + j) * LANES, (pairs * kh + j + 1) * LANES) for j in range(pairs)]
    qcat = jnp.concatenate([qe_ref[r0:r0 + nr, c] for c in cols] + [qo_ref[r0:r0 + nr, c] for c in cols], axis=0)
    o = _attend(qcat, kwin, vwin, valid, sink)
    lo = _half_mask((nr, LANES))
    for j, c in enumerate(cols):
        att_ref[r0:r0 + nr, c] = jnp.where(lo, o[j * nr:(j + 1) * nr], o[(pairs + j) * nr:(pairs + j + 1) * nr]).astype(BF16)


def _attn_prompt_body(x_ref, g_ref, wq_ref, qg_ref, cos_ref, sneg_ref, spos_ref,
                      kprev_ref, kcur_ref, vprev_ref, vcur_ref, sink_ref, wo_ref,
                      o_ref, qe_ref, qo_ref, att_ref, kw_ref, vw_ref):
    i = pl.program_id(0)
    tm = x_ref.shape[0]
    _project_q(x_ref, g_ref, wq_ref, qg_ref, cos_ref, sneg_ref, spos_ref, qe_ref, qo_ref)
    kw_ref[:, :WINDOW, :] = kprev_ref[...]
    kw_ref[:, WINDOW:, :] = kcur_ref[...]
    vw_ref[:, :WINDOW, :] = vprev_ref[...]
    vw_ref[:, WINDOW:, :] = vcur_ref[...]
    lane = lax.broadcasted_iota(jnp.int32, (1, KEYS), 1)
    for c in range(tm // CHUNK):
        w = c // 2
        valid = (lane < 3 * CHUNK) if c % 2 == 0 else (lane >= CHUNK)
        if w == 0:
            valid = valid & ((lane >= WINDOW) | (i > 0))
        for kh in range(N_KV_HEADS):
            _attend_rows(qe_ref, qo_ref, att_ref, c * CHUNK, CHUNK, kh,
                         kw_ref[kh, w * WINDOW:w * WINDOW + KEYS, :], vw_ref[kh, w * WINDOW:w * WINDOW + KEYS, :],
                         valid, sink_ref[kh])
    o_ref[...] = x_ref[...] + jnp.dot(att_ref[...], wo_ref[...], preferred_element_type=F32)


def _attn_sample_body(x_ref, g_ref, wq_ref, qg_ref, cos_ref, sneg_ref, spos_ref,
                      kc_ref, knew_ref, vc_ref, vnew_ref, sink_ref, wo_ref,
                      o_ref, qe_ref, qo_ref, att_ref, kw_ref, vw_ref, *, seg):
    tm = x_ref.shape[0]
    _project_q(x_ref, g_ref, wq_ref, qg_ref, cos_ref, sneg_ref, spos_ref, qe_ref, qo_ref)
    nkeys = WINDOW + seg
    valid = lax.broadcasted_iota(jnp.int32, (1, KEYS), 1) < nkeys
    kw_ref[nkeys:, :] = jnp.zeros((KEYS - nkeys, LANES), BF16)
    vw_ref[nkeys:, :] = jnp.zeros((KEYS - nkeys, LANES), BF16)
    for b in range(tm // seg):
        for kh in range(N_KV_HEADS):
            kw_ref[:WINDOW, :] = kc_ref[b, kh]
            kw_ref[WINDOW:nkeys, :] = knew_ref[kh, b * seg:(b + 1) * seg, :]
            vw_ref[:WINDOW, :] = vc_ref[b, kh]
            vw_ref[WINDOW:nkeys, :] = vnew_ref[kh, b * seg:(b + 1) * seg, :]
            _attend_rows(qe_ref, qo_ref, att_ref, b * seg, seg, kh, kw_ref[...], vw_ref[...], valid, sink_ref[kh])
    o_ref[...] = x_ref[...] + jnp.dot(att_ref[...], wo_ref[...], preferred_element_type=F32)


def _attn_mixer(x, g, w_q, qg, rope, k2, v2, cache, sink_col, w_o, *, tm, seg):
    t, d = x.shape
    tm = min(tm, t)
    row = lambda i: (i, 0)
    const = lambda i: (0, 0)
    resident = dict(pipeline_mode=pl.Buffered(1))
    head_specs = [
        pl.BlockSpec((tm, d), row),
        pl.BlockSpec((1, d), const),
        pl.BlockSpec((d, d), const, **resident),
        pl.BlockSpec((1, LANES), const),
        pl.BlockSpec((tm, LANES), row),
        pl.BlockSpec((tm, LANES), row),
        pl.BlockSpec((tm, LANES), row),
    ]
    cur = pl.BlockSpec((N_KV_HEADS, tm, LANES), lambda i: (0, i, 0))
    if seg is None:
        prev = pl.BlockSpec((N_KV_HEADS, WINDOW, LANES), lambda i: (0, jnp.maximum(i * (tm // WINDOW) - 1, 0), 0))
        kv_specs, kv_args = [prev, cur, prev, cur], [k2, k2, v2, v2]
        body = _attn_prompt_body
        win = (N_KV_HEADS, WINDOW + tm, LANES)
        rows = GROUP * CHUNK
    else:
        nb = tm // seg
        cspec = pl.BlockSpec((nb, N_KV_HEADS, WINDOW, LANES), lambda i: (i, 0, 0, 0))
        kv_specs, kv_args = [cspec, cur, cspec, cur], [cache[0], k2, cache[1], v2]
        body = functools.partial(_attn_sample_body, seg=seg)
        win = (KEYS, LANES)
        rows = GROUP * seg
    tail_specs = [
        pl.BlockSpec((N_KV_HEADS, rows, 1), lambda i: (0, 0, 0)),
        pl.BlockSpec((d, d), const, **resident),
    ]
    nbytes = 2 * d * d * 2 + 4 * tm * d * 4 + tm * d * (4 + 3 * 2) + 8 * tm * LANES * 4 * 2 + 4 * 2 ** 20
    return pl.pallas_call(
        body,
        grid=(t // tm,),
        in_specs=head_specs + kv_specs + tail_specs,
        out_specs=pl.BlockSpec((tm, d), row),
        out_shape=jax.ShapeDtypeStruct((t, d), F32),
        scratch_shapes=[pltpu.VMEM((tm, d), BF16), pltpu.VMEM((tm, d), BF16), pltpu.VMEM((tm, d), BF16),
                        pltpu.VMEM(win, BF16), pltpu.VMEM(win, BF16)],
        compiler_params=pltpu.CompilerParams(
            dimension_semantics=("arbitrary",), vmem_limit_bytes=_vmem_limit(nbytes)),
        name="attn_mixer",
    )(x, g, w_q, qg, *rope, *kv_args, sink_col, w_o)


def _rope_tables(pos):
    half = ROT_DIM // 2
    inv = ROPE_THETA ** (-jnp.arange(half, dtype=F32) / half)
    ang = pos.astype(F32)[:, None] * inv[None, :]
    cos, sin = jnp.cos(ang), jnp.sin(ang)
    n = pos.shape[0]
    one = jnp.ones((n, HEAD_DIM - ROT_DIM), F32)
    zero = jnp.zeros((n, HEAD_DIM - ROT_DIM), F32)
    zh = jnp.zeros((n, half), F32)
    c = jnp.concatenate([cos, cos, one], axis=1)
    sneg = jnp.concatenate([-sin, zh, zero], axis=1)
    spos = jnp.concatenate([zh, sin, zero], axis=1)
    return tuple(jnp.tile(a, (1, LANES // HEAD_DIM)) for a in (c, sneg, spos))


def _sink_column(sinks_l, rows_per_head):
    s = sinks_l.astype(F32).reshape(N_KV_HEADS, GROUP // 2, 2).transpose(0, 2, 1)
    return jnp.repeat(s.reshape(N_KV_HEADS, GROUP), rows_per_head, axis=1)[..., None]


def _dup_heads(t):
    t = t.transpose(0, 2, 1, 3)
    return jnp.concatenate([t, t], axis=-1).astype(BF16)


def _forward(x_prompt, x_sample, state_conv, cache_k, cache_v, mix_norm_g, mlp_norm_g, w_up, w_down,
             conv_w_in, conv_w, conv_w_out, kv_norm_g, w_kv, k_norm_g, w_q, q_norm_g, sinks, w_o,
             *, tm_mlp, tf, tm_conv, tn, tm_attn, tm_kv):
    _, s, d = x_prompt.shape
    b, l, _ = x_sample.shape
    n_a = conv_w_in.shape[0]
    depth = w_up.shape[0]
    xp = x_prompt.reshape(s, d)
    xs = x_sample.reshape(b * l, d)

    w_up_b, w_down_b = w_up.astype(BF16), w_down.astype(BF16)
    nj = d // tn
    w_in_r = conv_w_in.reshape(n_a, d, 3, nj, tn).transpose(0, 1, 3, 2, 4).reshape(n_a, d, 3 * d).astype(BF16)
    w_out_b = conv_w_out.astype(BF16)
    w_kv_b, w_q_b, w_o_b = w_kv.astype(BF16), w_q.astype(BF16), w_o.astype(BF16)
    mix_g = mix_norm_g.reshape(depth, 1, d)
    mlp_g = mlp_norm_g.reshape(depth, 1, d)

    rope_p = _rope_tables(jnp.arange(s))
    rope_s = _rope_tables(jnp.tile(PAST_LEN + jnp.arange(l), b))
    kg = jnp.tile(k_norm_g.astype(F32), LANES // HEAD_DIM).reshape(1, LANES)

    conv_p, conv_s = [], []
    for i in range(depth):
        if i < n_a:
            xp, cp = _conv_mixer(xp, mix_g[i], w_in_r[i], conv_w[i], w_out_b[i], None, tm=tm_conv, tn=tn, seg=None)
            xs, cs = _conv_mixer(xs, mix_g[i], w_in_r[i], conv_w[i], w_out_b[i], state_conv[i], tm=tm_conv, tn=tn, seg=l)
            conv_p.append(cp)
            conv_s.append(cs)
        else:
            if i == n_a:
                kp, vp, k2p, v2p = _shared_kv(xp, kv_norm_g.reshape(1, d), w_kv_b, kg, rope_p, tm=tm_kv)
                ks, vs, k2s, v2s = _shared_kv(xs, kv_norm_g.reshape(1, d), w_kv_b, kg, rope_s, tm=tm_kv)
                cache2 = (_dup_heads(cache_k), _dup_heads(cache_v))
            j = i - n_a
            qg = jnp.tile(q_norm_g[j].astype(F32), LANES // HEAD_DIM).reshape(1, LANES)
            xp = _attn_mixer(xp, mix_g[i], w_q_b[j], qg, rope_p, k2p, v2p, None, _sink_column(sinks[j], CHUNK),
                             w_o_b[j], tm=tm_attn, seg=None)
            xs = _attn_mixer(xs, mix_g[i], w_q_b[j], qg, rope_s, k2s, v2s, cache2, _sink_column(sinks[j], l),
                             w_o_b[j], tm=tm_attn, seg=l)
        xp = _mlp(xp, mlp_g[i], w_up_b[i], w_down_b[i], tm=tm_mlp, tf=tf)
        xs = _mlp(xs, mlp_g[i], w_up_b[i], w_down_b[i], tm=tm_mlp, tf=tf)

    hd = (N_KV_HEADS, HEAD_DIM)
    ks_new = ks.reshape(b, l, *hd)
    vs_new = vs.reshape(b, l, *hd)
    return (xp.reshape(1, s, d), xs.reshape(b, l, d), jnp.stack(conv_p), jnp.stack(conv_s),
            kp[s - WINDOW:].reshape(1, WINDOW, *hd), vp[s - WINDOW:].reshape(1, WINDOW, *hd),
            jnp.concatenate([cache_k[:, l:], ks_new], axis=1), jnp.concatenate([cache_v[:, l:], vs_new], axis=1))


def kernel(x_prompt, x_sample, state_conv, cache_k, cache_v, mix_norm_g, mlp_norm_g, w_up, w_down, conv_w_in, conv_w, conv_w_out, kv_norm_g, w_kv, k_norm_g, w_q, q_norm_g, sinks, w_o):
    return _forward(x_prompt, x_sample, state_conv, cache_k, cache_v, mix_norm_g, mlp_norm_g, w_up, w_down,
                    conv_w_in, conv_w, conv_w_out, kv_norm_g, w_kv, k_norm_g, w_q, q_norm_g, sinks, w_o,
                    tm_mlp=512, tf=1024, tm_conv=512, tn=512, tm_attn=256, tm_kv=512)
```

```python
import functools

import jax
import jax.numpy as jnp
from jax import lax
from jax.experimental import pallas as pl
from jax.experimental.pallas import tpu as pltpu

EPS = 1e-6
CHUNK = 64
WINDOW = 128
HEAD_DIM = 64
N_KV_HEADS = 4
GROUP = 8
ROT_DIM = 16
ROPE_THETA = 500000.0
PAST_LEN = 2048
SCALE = HEAD_DIM ** -0.5

LANES = 128
KEYS = 2 * WINDOW
VMEM_LIMIT_CAP = 56 * 2 ** 20

F32 = jnp.float32
BF16 = jnp.bfloat16


def _vmem_limit(nbytes):
    return int(min(VMEM_LIMIT_CAP, max(32 * 2 ** 20, nbytes * 5 // 4 + 4 * 2 ** 20)))


def _rms(x, g):
    return x * lax.rsqrt(jnp.mean(x * x, axis=-1, keepdims=True) + EPS) * g


def _half_mask(shape):
    return lax.broadcasted_iota(jnp.int32, shape, len(shape) - 1) < HEAD_DIM


def _head_norm_rope(t, gain, cos, sneg, spos):
    lo = _half_mask(t.shape)
    sq = t * t
    s_lo = jnp.sum(jnp.where(lo, sq, 0.0), axis=-1, keepdims=True)
    s_hi = jnp.sum(jnp.where(lo, 0.0, sq), axis=-1, keepdims=True)
    inv = jnp.where(lo, lax.rsqrt(s_lo / HEAD_DIM + EPS), lax.rsqrt(s_hi / HEAD_DIM + EPS))
    tn = t * inv * gain
    half = ROT_DIM // 2
    return tn * cos + pltpu.roll(tn, LANES - half, 1) * sneg + pltpu.roll(tn, half, 1) * spos


def _mlp_body(x_ref, g_ref, wu_ref, wd_ref, o_ref, xn_ref):
    @pl.when(pl.program_id(1) == 0)
    def _():
        x = x_ref[...]
        xn_ref[...] = _rms(x, g_ref[...]).astype(BF16)
        o_ref[...] = x

    h = jnp.dot(xn_ref[...], wu_ref[...], preferred_element_type=F32)
    h = jnp.square(jnp.maximum(h, 0.0)).astype(BF16)
    o_ref[...] += jnp.dot(h, wd_ref[...], preferred_element_type=F32)


def _mlp(x, g, w_up, w_down, layer, *, tm, tf):
    t, d = x.shape
    f = w_up.shape[2]
    tm, tf = min(tm, t), min(tf, f)
    nbytes = 2 * (2 * tm * d * 4 + 2 * d * tf * 2) + tm * d * 2 + tm * tf * 6
    return pl.pallas_call(
        _mlp_body,
        grid=(t // tm, f // tf),
        in_specs=[
            pl.BlockSpec((tm, d), lambda i, j: (i, 0)),
            pl.BlockSpec((None, 1, d), lambda i, j: (layer, 0, 0)),
            pl.BlockSpec((None, d, tf), lambda i, j: (layer, 0, j)),
            pl.BlockSpec((None, tf, d), lambda i, j: (layer, j, 0)),
        ],
        out_specs=pl.BlockSpec((tm, d), lambda i, j: (i, 0)),
        out_shape=jax.ShapeDtypeStruct((t, d), F32),
        scratch_shapes=[pltpu.VMEM((tm, d), BF16)],
        compiler_params=pltpu.CompilerParams(
            dimension_semantics=("arbitrary", "arbitrary"), vmem_limit_bytes=_vmem_limit(nbytes)),
        name="mlp",
    )(x, g, w_up, w_down)


def _conv_body(*refs, seg, tn):
    if seg is None:
        x_ref, g_ref, wb_ref, wc_ref, wu_ref, cw_ref, wout_ref, o_ref, zl_ref, xn_ref = refs
    else:
        x_ref, g_ref, wb_ref, wc_ref, wu_ref, cw_ref, wout_ref, st_ref, o_ref, zl_ref, xn_ref = refs
    i, j = pl.program_id(0), pl.program_id(1)

    @pl.when(j == 0)
    def _():
        x = x_ref[...]
        xn_ref[...] = _rms(x, g_ref[...]).astype(BF16)
        o_ref[...] = x

    xn = xn_ref[...]
    gate_b = jnp.dot(xn, wb_ref[...], preferred_element_type=F32)
    z = jnp.dot(xn, wc_ref[...], preferred_element_type=F32) * jnp.dot(xn, wu_ref[...], preferred_element_type=F32)
    tm = z.shape[0]
    row = lax.broadcasted_iota(jnp.int32, z.shape, 0)
    r1 = pltpu.roll(z, 1, 0)
    r2 = pltpu.roll(z, 2, 0)
    if seg is None:
        @pl.when(i == 0)
        def _():
            zl_ref[j] = jnp.zeros((2, tn), F32)

        prev = zl_ref[j]
        p0, p1 = prev[0:1, :], prev[1:2, :]
        zl_ref[j] = z[tm - 2:, :]
    else:
        nb = tm // seg
        st = st_ref[...]
        p0 = jnp.broadcast_to(st[:, 0:1, :], (nb, seg, tn)).reshape(tm, tn)
        p1 = jnp.broadcast_to(st[:, 1:2, :], (nb, seg, tn)).reshape(tm, tn)
        row = row % seg
        zl_ref[...] = z.reshape(nb, seg, tn)[:, seg - 2:, :]
    zp1 = jnp.where(row == 0, p1, r1)
    zp2 = jnp.where(row == 0, p0, jnp.where(row == 1, p1, r2))
    cw = cw_ref[...]
    conv = zp2 * cw[0:1, :] + zp1 * cw[1:2, :] + z * cw[2:3, :]
    y = (gate_b * conv).astype(BF16)
    o_ref[...] += jnp.dot(y, wout_ref[...], preferred_element_type=F32)


def _conv_mixer(x, g, w_in, cw, w_out, state, layer, *, tm, tn, seg):
    t, d = x.shape
    tm = min(tm, t)
    nj = d // tn
    in_specs = [
        pl.BlockSpec((tm, d), lambda i, j: (i, 0)),
        pl.BlockSpec((None, 1, d), lambda i, j: (layer, 0, 0)),
        pl.BlockSpec((None, d, tn), lambda i, j: (layer, 0, j)),
        pl.BlockSpec((None, d, tn), lambda i, j: (layer, 0, nj + j)),
        pl.BlockSpec((None, d, tn), lambda i, j: (layer, 0, 2 * nj + j)),
        pl.BlockSpec((None, 3, tn), lambda i, j: (layer, 0, j)),
        pl.BlockSpec((None, tn, d), lambda i, j: (layer, j, 0)),
    ]
    args = [x, g, w_in, w_in, w_in, cw, w_out]
    if seg is None:
        zl_shape = (nj, 2, tn)
        zl_spec = pl.BlockSpec((nj, 2, tn), lambda i, j: (0, 0, 0))
    else:
        nb = tm // seg
        zl_shape = (t // seg, 2, d)
        zl_spec = pl.BlockSpec((nb, 2, tn), lambda i, j: (i, 0, j))
        in_specs.append(pl.BlockSpec((None, nb, 2, tn), lambda i, j: (layer, i, 0, j)))
        args.append(state)
    nbytes = 2 * (2 * tm * d * 4 + d * 3 * tn * 2 + tn * d * 2) + tm * d * 2 + tm * tn * 4 * 8
    y, zl = pl.pallas_call(
        functools.partial(_conv_body, seg=seg, tn=tn),
        grid=(t // tm, nj),
        in_specs=in_specs,
        out_specs=[pl.BlockSpec((tm, d), lambda i, j: (i, 0)), zl_spec],
        out_shape=[jax.ShapeDtypeStruct((t, d), F32), jax.ShapeDtypeStruct(zl_shape, F32)],
        scratch_shapes=[pltpu.VMEM((tm, d), BF16)],
        compiler_params=pltpu.CompilerParams(
            dimension_semantics=("arbitrary", "arbitrary"), vmem_limit_bytes=_vmem_limit(nbytes)),
        name="conv_mixer",
    )(*args)
    if seg is None:
        zl = zl.transpose(1, 0, 2).reshape(1, 2, d)
    return y, zl


def _kv_body(x_ref, g_ref, wkv_ref, kg_ref, cos_ref, sneg_ref, spos_ref, k_ref, v_ref, k2_ref, v2_ref):
    xn = _rms(x_ref[...], g_ref[...]).astype(BF16)
    kv = jnp.dot(xn, wkv_ref[...], preferred_element_type=F32)
    nkv = N_KV_HEADS * HEAD_DIM
    lo = _half_mask((xn.shape[0], LANES))
    for p in range(nkv // LANES):
        sl = slice(p * LANES, (p + 1) * LANES)
        kr = _head_norm_rope(kv[:, sl], kg_ref[...], cos_ref[...], sneg_ref[...], spos_ref[...])
        vr = kv[:, nkv + p * LANES: nkv + (p + 1) * LANES]
        k_ref[:, sl] = kr
        v_ref[:, sl] = vr
        for src, dst in ((kr, k2_ref), (vr, v2_ref)):
            sw = pltpu.roll(src, HEAD_DIM, 1)
            dst[2 * p] = jnp.where(lo, src, sw).astype(BF16)
            dst[2 * p + 1] = jnp.where(lo, sw, src).astype(BF16)


def _shared_kv(x, g, w_kv, kg, rope, *, tm):
    t, d = x.shape
    tm = min(tm, t)
    nkv = N_KV_HEADS * HEAD_DIM
    row = lambda i: (i, 0)
    const = lambda i: (0, 0)
    nbytes = 2 * (tm * d * 4 + d * 2 * nkv * 2 + 5 * tm * nkv * 4) + tm * d * 8
    return pl.pallas_call(
        _kv_body,
        grid=(t // tm,),
        in_specs=[
            pl.BlockSpec((tm, d), row),
            pl.BlockSpec((1, d), const),
            pl.BlockSpec((d, 2 * nkv), const),
            pl.BlockSpec((1, LANES), const),
            pl.BlockSpec((tm, LANES), row),
            pl.BlockSpec((tm, LANES), row),
            pl.BlockSpec((tm, LANES), row),
        ],
        out_specs=[
            pl.BlockSpec((tm, nkv), row),
            pl.BlockSpec((tm, nkv), row),
            pl.BlockSpec((N_KV_HEADS, tm, LANES), lambda i: (0, i, 0)),
            pl.BlockSpec((N_KV_HEADS, tm, LANES), lambda i: (0, i, 0)),
        ],
        out_shape=[
            jax.ShapeDtypeStruct((t, nkv), F32),
            jax.ShapeDtypeStruct((t, nkv), F32),
            jax.ShapeDtypeStruct((N_KV_HEADS, t, LANES), BF16),
            jax.ShapeDtypeStruct((N_KV_HEADS, t, LANES), BF16),
        ],
        compiler_params=pltpu.CompilerParams(
            dimension_semantics=("arbitrary",), vmem_limit_bytes=_vmem_limit(nbytes)),
        name="shared_kv",
    )(x, g, w_kv, kg, *rope)


def _attend(qcat, kwin, vwin, valid, sink):
    s = lax.dot_general(qcat, kwin, (((1,), (1,)), ((), ())), preferred_element_type=F32)
    s = jnp.where(valid, s, -jnp.inf)
    m = jnp.maximum(jnp.max(s, axis=-1, keepdims=True), sink)
    e = jnp.exp(s - m)
    den = jnp.sum(e, axis=-1, keepdims=True) + jnp.exp(sink - m)
    o = jnp.dot(e.astype(BF16), vwin, preferred_element_type=F32)
    return o / den


def _project_q(x_ref, g_ref, wq_ref, qg_ref, cos_ref, sneg_ref, spos_ref, qe_ref, qo_ref):
    xn = _rms(x_ref[...], g_ref[...]).astype(BF16)
    q = jnp.dot(xn, wq_ref[...], preferred_element_type=F32)
    lo = _half_mask((q.shape[0], LANES))
    for p in range(q.shape[1] // LANES):
        sl = slice(p * LANES, (p + 1) * LANES)
        qr = _head_norm_rope(q[:, sl], qg_ref[...], cos_ref[...], sneg_ref[...], spos_ref[...]) * SCALE
        qe_ref[:, sl] = jnp.where(lo, qr, 0.0).astype(BF16)
        qo_ref[:, sl] = jnp.where(lo, 0.0, qr).astype(BF16)


def _attend_rows(qe_ref, qo_ref, att_ref, r0, nr, kh, kwin, vwin, valid, sink):
    pairs = GROUP // 2
    cols = [slice((pairs * kh + j) * LANES, (pairs * kh + j + 1) * LANES) for j in range(pairs)]
    qcat = jnp.concatenate([qe_ref[r0:r0 + nr, c] for c in cols] + [qo_ref[r0:r0 + nr, c] for c in cols], axis=0)
    o = _attend(qcat, kwin, vwin, valid, sink)
    lo = _half_mask((nr, LANES))
    for j, c in enumerate(cols):
        att_ref[r0:r0 + nr, c] = jnp.where(lo, o[j * nr:(j + 1) * nr], o[(pairs + j) * nr:(pairs + j + 1) * nr]).astype(BF16)


def _attn_prompt_body(x_ref, g_ref, wq_ref, qg_ref, cos_ref, sneg_ref, spos_ref,
                      kprev_ref, kcur_ref, vprev_ref, vcur_ref, sink_ref, wo_ref,
                      o_ref, qe_ref, qo_ref, att_ref, kw_ref, vw_ref):
    i = pl.program_id(0)
    tm = x_ref.shape[0]
    _project_q(x_ref, g_ref, wq_ref, qg_ref, cos_ref, sneg_ref, spos_ref, qe_ref, qo_ref)
    kw_ref[:, :WINDOW, :] = kprev_ref[...]
    kw_ref[:, WINDOW:, :] = kcur_ref[...]
    vw_ref[:, :WINDOW, :] = vprev_ref[...]
    vw_ref[:, WINDOW:, :] = vcur_ref[...]
    lane = lax.broadcasted_iota(jnp.int32, (1, KEYS), 1)
    for c in range(tm // CHUNK):
        w = c // 2
        valid = (lane < 3 * CHUNK) if c % 2 == 0 else (lane >= CHUNK)
        if w == 0:
            valid = valid & ((lane >= WINDOW) | (i > 0))
        for kh in range(N_KV_HEADS):
            _attend_rows(qe_ref, qo_ref, att_ref, c * CHUNK, CHUNK, kh,
                         kw_ref[kh, w * WINDOW:w * WINDOW + KEYS, :], vw_ref[kh, w * WINDOW:w * WINDOW + KEYS, :],
                         valid, sink_ref[kh])
    o_ref[...] = x_ref[...] + jnp.dot(att_ref[...], wo_ref[...], preferred_element_type=F32)


def _attn_sample_body(x_ref, g_ref, wq_ref, qg_ref, cos_ref, sneg_ref, spos_ref,
                      kc_ref, knew_ref, vc_ref, vnew_ref, sink_ref, wo_ref,
                      o_ref, qe_ref, qo_ref, att_ref, kw_ref, vw_ref, *, seg):
    tm = x_ref.shape[0]
    _project_q(x_ref, g_ref, wq_ref, qg_ref, cos_ref, sneg_ref, spos_ref, qe_ref, qo_ref)
    nkeys = WINDOW + seg
    valid = lax.broadcasted_iota(jnp.int32, (1, KEYS), 1) < nkeys
    kw_ref[nkeys:, :] = jnp.zeros((KEYS - nkeys, LANES), BF16)
    vw_ref[nkeys:, :] = jnp.zeros((KEYS - nkeys, LANES), BF16)
    for b in range(tm // seg):
        for kh in range(N_KV_HEADS):
            kw_ref[:WINDOW, :] = kc_ref[b, kh]
            kw_ref[WINDOW:nkeys, :] = knew_ref[kh, b * seg:(b + 1) * seg, :]
            vw_ref[:WINDOW, :] = vc_ref[b, kh]
            vw_ref[WINDOW:nkeys, :] = vnew_ref[kh, b * seg:(b + 1) * seg, :]
            _attend_rows(qe_ref, qo_ref, att_ref, b * seg, seg, kh, kw_ref[...], vw_ref[...], valid, sink_ref[kh])
    o_ref[...] = x_ref[...] + jnp.dot(att_ref[...], wo_ref[...], preferred_element_type=F32)


def _attn_mixer(x, g, w_q, qg, rope, k2, v2, cache, sink_col, w_o, layer, blayer, *, tm, seg):
    t, d = x.shape
    tm = min(tm, t)
    row = lambda i: (i, 0)
    const = lambda i: (0, 0)
    resident = dict(pipeline_mode=pl.Buffered(1))
    head_specs = [
        pl.BlockSpec((tm, d), row),
        pl.BlockSpec((None, 1, d), lambda i: (layer, 0, 0)),
        pl.BlockSpec((None, d, d), lambda i: (blayer, 0, 0), **resident),
        pl.BlockSpec((1, LANES), const),
        pl.BlockSpec((tm, LANES), row),
        pl.BlockSpec((tm, LANES), row),
        pl.BlockSpec((tm, LANES), row),
    ]
    cur = pl.BlockSpec((N_KV_HEADS, tm, LANES), lambda i: (0, i, 0))
    if seg is None:
        prev = pl.BlockSpec((N_KV_HEADS, WINDOW, LANES), lambda i: (0, jnp.maximum(i * (tm // WINDOW) - 1, 0), 0))
        kv_specs, kv_args = [prev, cur, prev, cur], [k2, k2, v2, v2]
        body = _attn_prompt_body
        kwin = vwin = (N_KV_HEADS, WINDOW + tm, LANES)
        rows = GROUP * CHUNK
    else:
        nb = tm // seg
        cspec = pl.BlockSpec((nb, N_KV_HEADS, WINDOW, LANES), lambda i: (i, 0, 0, 0))
        kv_specs, kv_args = [cspec, cur, cspec, cur], [cache[0], k2, cache[1], v2]
        body = functools.partial(_attn_sample_body, seg=seg)
        kwin = vwin = (KEYS, LANES)
        rows = GROUP * seg
    tail_specs = [
        pl.BlockSpec((N_KV_HEADS, rows, 1), lambda i: (0, 0, 0)),
        pl.BlockSpec((None, d, d), lambda i: (blayer, 0, 0), **resident),
    ]
    nbytes = 2 * d * d * 2 + 4 * tm * d * 4 + tm * d * (4 + 3 * 2) + 8 * tm * LANES * 4 * 2 + 4 * 2 ** 20
    return pl.pallas_call(
        body,
        grid=(t // tm,),
        in_specs=head_specs + kv_specs + tail_specs,
        out_specs=pl.BlockSpec((tm, d), row),
        out_shape=jax.ShapeDtypeStruct((t, d), F32),
        scratch_shapes=[pltpu.VMEM((tm, d), BF16), pltpu.VMEM((tm, d), BF16), pltpu.VMEM((tm, d), BF16),
                        pltpu.VMEM(kwin, BF16), pltpu.VMEM(vwin, BF16)],
        compiler_params=pltpu.CompilerParams(
            dimension_semantics=("arbitrary",), vmem_limit_bytes=_vmem_limit(nbytes)),
        name="attn_mixer",
    )(x, g, w_q, qg, *rope, *kv_args, sink_col, w_o)


def _rope_tables(pos):
    half = ROT_DIM // 2
    inv = ROPE_THETA ** (-jnp.arange(half, dtype=F32) / half)
    ang = pos.astype(F32)[:, None] * inv[None, :]
    cos, sin = jnp.cos(ang), jnp.sin(ang)
    n = pos.shape[0]
    one = jnp.ones((n, HEAD_DIM - ROT_DIM), F32)
    zero = jnp.zeros((n, HEAD_DIM - ROT_DIM), F32)
    zh = jnp.zeros((n, half), F32)
    c = jnp.concatenate([cos, cos, one], axis=1)
    sneg = jnp.concatenate([-sin, zh, zero], axis=1)
    spos = jnp.concatenate([zh, sin, zero], axis=1)
    return tuple(jnp.tile(a, (1, LANES // HEAD_DIM)) for a in (c, sneg, spos))


def _sink_column(sinks_l, rows_per_head):
    s = sinks_l.astype(F32).reshape(N_KV_HEADS, GROUP // 2, 2).transpose(0, 2, 1)
    return jnp.repeat(s.reshape(N_KV_HEADS, GROUP), rows_per_head, axis=1)[..., None]


def _dup_heads(t):
    t = t.transpose(0, 2, 1, 3)
    return jnp.concatenate([t, t], axis=-1).astype(BF16)


def _forward(x_prompt, x_sample, state_conv, cache_k, cache_v, mix_norm_g, mlp_norm_g, w_up, w_down,
             conv_w_in, conv_w, conv_w_out, kv_norm_g, w_kv, k_norm_g, w_q, q_norm_g, sinks, w_o,
             *, tm_mlp, tf, tm_conv, tn, tm_attn, tm_kv):
    _, s, d = x_prompt.shape
    b, l, _ = x_sample.shape
    n_a = conv_w_in.shape[0]
    depth = w_up.shape[0]
    xp = x_prompt.reshape(s, d)
    xs = x_sample.reshape(b * l, d)

    w_up_b, w_down_b = w_up.astype(BF16), w_down.astype(BF16)
    w_in_b, w_out_b = conv_w_in.astype(BF16), conv_w_out.astype(BF16)
    w_kv_b, w_q_b, w_o_b = w_kv.astype(BF16), w_q.astype(BF16), w_o.astype(BF16)
    mix_g = mix_norm_g.reshape(depth, 1, d)
    mlp_g = mlp_norm_g.reshape(depth, 1, d)

    rope_p = _rope_tables(jnp.arange(s))
    rope_s = _rope_tables(jnp.tile(PAST_LEN + jnp.arange(l), b))
    kg = jnp.tile(k_norm_g.astype(F32), LANES // HEAD_DIM).reshape(1, LANES)

    conv_p, conv_s = [], []
    for i in range(depth):
        if i < n_a:
            xp, cp = _conv_mixer(xp, mix_g, w_in_b, conv_w, w_out_b, None, i, tm=tm_conv, tn=tn, seg=None)
            xs, cs = _conv_mixer(xs, mix_g, w_in_b, conv_w, w_out_b, state_conv, i, tm=tm_conv, tn=tn, seg=l)
            conv_p.append(cp)
            conv_s.append(cs)
        else:
            if i == n_a:
                kp, vp, k2p, v2p = _shared_kv(xp, kv_norm_g.reshape(1, d), w_kv_b, kg, rope_p, tm=tm_kv)
                ks, vs, k2s, v2s = _shared_kv(xs, kv_norm_g.reshape(1, d), w_kv_b, kg, rope_s, tm=tm_kv)
                cache2 = (_dup_heads(cache_k), _dup_heads(cache_v))
            j = i - n_a
            qg = jnp.tile(q_norm_g[j].astype(F32), LANES // HEAD_DIM).reshape(1, LANES)
            xp = _attn_mixer(xp, mix_g, w_q_b, qg, rope_p, k2p, v2p, None, _sink_column(sinks[j], CHUNK),
                             w_o_b, i, j, tm=tm_attn, seg=None)
            xs = _attn_mixer(xs, mix_g, w_q_b, qg, rope_s, k2s, v2s, cache2, _sink_column(sinks[j], l),
                             w_o_b, i, j, tm=tm_attn, seg=l)
        xp = _mlp(xp, mlp_g, w_up_b, w_down_b, i, tm=tm_mlp, tf=tf)
        xs = _mlp(xs, mlp_g, w_up_b, w_down_b, i, tm=tm_mlp, tf=tf)

    hd = (N_KV_HEADS, HEAD_DIM)
    ks_new = ks.reshape(b, l, *hd)
    vs_new = vs.reshape(b, l, *hd)
    return (xp.reshape(1, s, d), xs.reshape(b, l, d), jnp.stack(conv_p), jnp.stack(conv_s),
            kp[s - WINDOW:].reshape(1, WINDOW, *hd), vp[s - WINDOW:].reshape(1, WINDOW, *hd),
            jnp.concatenate([cache_k[:, l:], ks_new], axis=1), jnp.concatenate([cache_v[:, l:], vs_new], axis=1))


def kernel(x_prompt, x_sample, state_conv, cache_k, cache_v, mix_norm_g, mlp_norm_g, w_up, w_down, conv_w_in, conv_w, conv_w_out, kv_norm_g, w_kv, k_norm_g, w_q, q_norm_g, sinks, w_o):
    return _forward(x_prompt, x_sample, state_conv, cache_k, cache_v, mix_norm_g, mlp_norm_g, w_up, w_down,
                    conv_w_in, conv_w, conv_w_out, kv_norm_g, w_kv, k_norm_g, w_q, q_norm_g, sinks, w_o,
                    tm_mlp=512, tf=1024, tm_conv=512, tn=512, tm_attn=256, tm_kv=512)
```

```python
import functools

import jax
import jax.numpy as jnp
from jax import lax
from jax.experimental import pallas as pl
from jax.experimental.pallas import tpu as pltpu

EPS = 1e-6
CHUNK = 64
WINDOW = 128
HEAD_DIM = 64
N_KV_HEADS = 4
GROUP = 8
ROT_DIM = 16
ROPE_THETA = 500000.0
PAST_LEN = 2048
SCALE = HEAD_DIM ** -0.5
LOG2E = 1.4426950408889634

LANES = 128
KEYS = 2 * WINDOW
VMEM_LIMIT_CAP = 56 * 2 ** 20

F32 = jnp.float32
BF16 = jnp.bfloat16


def _vmem_limit(nbytes):
    return int(min(VMEM_LIMIT_CAP, max(32 * 2 ** 20, nbytes * 5 // 4 + 4 * 2 ** 20)))


def _rms(x, g):
    return x * lax.rsqrt(jnp.mean(x * x, axis=-1, keepdims=True) + EPS) * g


def _half_mask(shape):
    return lax.broadcasted_iota(jnp.int32, shape, len(shape) - 1) < HEAD_DIM


def _head_norm_rope(t, gain, cos, sneg, spos):
    lo = _half_mask(t.shape)
    sq = t * t
    s_lo = jnp.sum(jnp.where(lo, sq, 0.0), axis=-1, keepdims=True)
    s_hi = jnp.sum(jnp.where(lo, 0.0, sq), axis=-1, keepdims=True)
    inv = jnp.where(lo, lax.rsqrt(s_lo / HEAD_DIM + EPS), lax.rsqrt(s_hi / HEAD_DIM + EPS))
    tn = t * inv * gain
    half = ROT_DIM // 2
    return tn * cos + pltpu.roll(tn, LANES - half, 1) * sneg + pltpu.roll(tn, half, 1) * spos


def _mlp_body(x_ref, g_ref, wu_ref, wd_ref, o_ref, xn_ref):
    @pl.when(pl.program_id(1) == 0)
    def _():
        x = x_ref[...]
        xn_ref[...] = _rms(x, g_ref[...]).astype(BF16)
        o_ref[...] = x

    h = jnp.dot(xn_ref[...], wu_ref[...], preferred_element_type=F32)
    h = jnp.square(jnp.maximum(h, 0.0)).astype(BF16)
    o_ref[...] += jnp.dot(h, wd_ref[...], preferred_element_type=F32)


def _mlp_cast_body(x_ref, g_ref, wu_ref, wd_ref, o_ref, wub_ref, wdb_ref, xn_ref):
    @pl.when(pl.program_id(0) == 0)
    def _():
        x = x_ref[...]
        xn_ref[...] = _rms(x, g_ref[...]).astype(BF16)
        o_ref[...] = x

    wu = wu_ref[...].astype(BF16)
    wd = wd_ref[...].astype(BF16)
    wub_ref[...] = wu
    wdb_ref[...] = wd
    h = jnp.dot(xn_ref[...], wu, preferred_element_type=F32)
    h = jnp.square(jnp.maximum(h, 0.0)).astype(BF16)
    o_ref[...] += jnp.dot(h, wd, preferred_element_type=F32)


def _mlp_cast(x, g, w_up, w_down, layer, *, tf):
    t, d = x.shape
    f = w_up.shape[2]
    tf = min(tf, f)
    nbytes = 2 * t * d * 4 + t * d * 2 + 2 * 2 * d * tf * (4 + 2) + t * tf * 6
    return pl.pallas_call(
        _mlp_cast_body,
        grid=(f // tf,),
        in_specs=[
            pl.BlockSpec((t, d), lambda j: (0, 0), pipeline_mode=pl.Buffered(1)),
            pl.BlockSpec((None, 1, d), lambda j: (layer, 0, 0)),
            pl.BlockSpec((None, d, tf), lambda j: (layer, 0, j)),
            pl.BlockSpec((None, tf, d), lambda j: (layer, j, 0)),
        ],
        out_specs=[
            pl.BlockSpec((t, d), lambda j: (0, 0), pipeline_mode=pl.Buffered(1)),
            pl.BlockSpec((d, tf), lambda j: (0, j)),
            pl.BlockSpec((tf, d), lambda j: (j, 0)),
        ],
        out_shape=[
            jax.ShapeDtypeStruct((t, d), F32),
            jax.ShapeDtypeStruct((d, f), BF16),
            jax.ShapeDtypeStruct((f, d), BF16),
        ],
        scratch_shapes=[pltpu.VMEM((t, d), BF16)],
        compiler_params=pltpu.CompilerParams(
            dimension_semantics=("arbitrary",), vmem_limit_bytes=_vmem_limit(nbytes)),
        name="mlp_cast",
    )(x, g, w_up, w_down)


def _mlp(x, g, w_up, w_down, layer, *, tm, tf):
    t, d = x.shape
    f = w_up.shape[1]
    tm, tf = min(tm, t), min(tf, f)
    nbytes = 2 * (2 * tm * d * 4 + 2 * d * tf * 2) + tm * d * 2 + tm * tf * 6
    return pl.pallas_call(
        _mlp_body,
        grid=(t // tm, f // tf),
        in_specs=[
            pl.BlockSpec((tm, d), lambda i, j: (i, 0)),
            pl.BlockSpec((None, 1, d), lambda i, j: (layer, 0, 0)),
            pl.BlockSpec((d, tf), lambda i, j: (0, j)),
            pl.BlockSpec((tf, d), lambda i, j: (j, 0)),
        ],
        out_specs=pl.BlockSpec((tm, d), lambda i, j: (i, 0)),
        out_shape=jax.ShapeDtypeStruct((t, d), F32),
        scratch_shapes=[pltpu.VMEM((tm, d), BF16)],
        compiler_params=pltpu.CompilerParams(
            dimension_semantics=("arbitrary", "arbitrary"), vmem_limit_bytes=_vmem_limit(nbytes)),
        name="mlp",
    )(x, g, w_up, w_down)


def _conv_body(*refs, seg, tn):
    if seg is None:
        x_ref, g_ref, wb_ref, wc_ref, wu_ref, cw_ref, wout_ref, o_ref, zl_ref, xn_ref = refs
    else:
        x_ref, g_ref, wb_ref, wc_ref, wu_ref, cw_ref, wout_ref, st_ref, o_ref, zl_ref, xn_ref = refs
    i, j = pl.program_id(0), pl.program_id(1)

    @pl.when(j == 0)
    def _():
        x = x_ref[...]
        xn_ref[...] = _rms(x, g_ref[...]).astype(BF16)
        o_ref[...] = x

    xn = xn_ref[...]
    gate_b = jnp.dot(xn, wb_ref[...], preferred_element_type=F32)
    z = jnp.dot(xn, wc_ref[...], preferred_element_type=F32) * jnp.dot(xn, wu_ref[...], preferred_element_type=F32)
    tm = z.shape[0]
    row = lax.broadcasted_iota(jnp.int32, z.shape, 0)
    r1 = pltpu.roll(z, 1, 0)
    r2 = pltpu.roll(z, 2, 0)
    if seg is None:
        @pl.when(i == 0)
        def _():
            zl_ref[j] = jnp.zeros((2, tn), F32)

        prev = zl_ref[j]
        p0, p1 = prev[0:1, :], prev[1:2, :]
        zl_ref[j] = z[tm - 2:, :]
    else:
        nb = tm // seg
        st = st_ref[...]
        p0 = jnp.broadcast_to(st[:, 0:1, :], (nb, seg, tn)).reshape(tm, tn)
        p1 = jnp.broadcast_to(st[:, 1:2, :], (nb, seg, tn)).reshape(tm, tn)
        row = row % seg
        zl_ref[...] = z.reshape(nb, seg, tn)[:, seg - 2:, :]
    zp1 = jnp.where(row == 0, p1, r1)
    zp2 = jnp.where(row == 0, p0, jnp.where(row == 1, p1, r2))
    cw = cw_ref[...]
    conv = zp2 * cw[0:1, :] + zp1 * cw[1:2, :] + z * cw[2:3, :]
    y = (gate_b * conv).astype(BF16)
    o_ref[...] += jnp.dot(y, wout_ref[...], preferred_element_type=F32)


def _conv_mixer(x, g, w_in, cw, w_out, state, layer, *, tm, tn, seg):
    t, d = x.shape
    tm = min(tm, t)
    nj = d // tn
    in_specs = [
        pl.BlockSpec((tm, d), lambda i, j: (i, 0)),
        pl.BlockSpec((None, 1, d), lambda i, j: (layer, 0, 0)),
        pl.BlockSpec((None, d, tn), lambda i, j: (layer, 0, j)),
        pl.BlockSpec((None, d, tn), lambda i, j: (layer, 0, nj + j)),
        pl.BlockSpec((None, d, tn), lambda i, j: (layer, 0, 2 * nj + j)),
        pl.BlockSpec((None, 3, tn), lambda i, j: (layer, 0, j)),
        pl.BlockSpec((None, tn, d), lambda i, j: (layer, j, 0)),
    ]
    args = [x, g, w_in, w_in, w_in, cw, w_out]
    if seg is None:
        zl_shape = (nj, 2, tn)
        zl_spec = pl.BlockSpec((nj, 2, tn), lambda i, j: (0, 0, 0))
    else:
        nb = tm // seg
        zl_shape = (t // seg, 2, d)
        zl_spec = pl.BlockSpec((nb, 2, tn), lambda i, j: (i, 0, j))
        in_specs.append(pl.BlockSpec((None, nb, 2, tn), lambda i, j: (layer, i, 0, j)))
        args.append(state)
    nbytes = 2 * (2 * tm * d * 4 + d * 3 * tn * 2 + tn * d * 2) + tm * d * 2 + tm * tn * 4 * 8
    y, zl = pl.pallas_call(
        functools.partial(_conv_body, seg=seg, tn=tn),
        grid=(t // tm, nj),
        in_specs=in_specs,
        out_specs=[pl.BlockSpec((tm, d), lambda i, j: (i, 0)), zl_spec],
        out_shape=[jax.ShapeDtypeStruct((t, d), F32), jax.ShapeDtypeStruct(zl_shape, F32)],
        scratch_shapes=[pltpu.VMEM((tm, d), BF16)],
        compiler_params=pltpu.CompilerParams(
            dimension_semantics=("arbitrary", "arbitrary"), vmem_limit_bytes=_vmem_limit(nbytes)),
        name="conv_mixer",
    )(*args)
    if seg is None:
        zl = zl.transpose(1, 0, 2).reshape(1, 2, d)
    return y, zl


def _kv_body(x_ref, g_ref, wkv_ref, kg_ref, cos_ref, sneg_ref, spos_ref, k_ref, v_ref, k2_ref, v2_ref):
    xn = _rms(x_ref[...], g_ref[...]).astype(BF16)
    kv = jnp.dot(xn, wkv_ref[...], preferred_element_type=F32)
    nkv = N_KV_HEADS * HEAD_DIM
    lo = _half_mask((xn.shape[0], LANES))
    for p in range(nkv // LANES):
        sl = slice(p * LANES, (p + 1) * LANES)
        kr = _head_norm_rope(kv[:, sl], kg_ref[...], cos_ref[...], sneg_ref[...], spos_ref[...])
        vr = kv[:, nkv + p * LANES: nkv + (p + 1) * LANES]
        k_ref[:, sl] = kr
        v_ref[:, sl] = vr
        for src, dst in ((kr, k2_ref), (vr, v2_ref)):
            sw = pltpu.roll(src, HEAD_DIM, 1)
            dst[2 * p] = jnp.where(lo, src, sw).astype(BF16)
            dst[2 * p + 1] = jnp.where(lo, sw, src).astype(BF16)


def _shared_kv(x, g, w_kv, kg, rope, *, tm):
    t, d = x.shape
    tm = min(tm, t)
    nkv = N_KV_HEADS * HEAD_DIM
    row = lambda i: (i, 0)
    const = lambda i: (0, 0)
    nbytes = 2 * (tm * d * 4 + d * 2 * nkv * 2 + 5 * tm * nkv * 4) + tm * d * 8
    return pl.pallas_call(
        _kv_body,
        grid=(t // tm,),
        in_specs=[
            pl.BlockSpec((tm, d), row),
            pl.BlockSpec((1, d), const),
            pl.BlockSpec((d, 2 * nkv), const),
            pl.BlockSpec((1, LANES), const),
            pl.BlockSpec((tm, LANES), row),
            pl.BlockSpec((tm, LANES), row),
            pl.BlockSpec((tm, LANES), row),
        ],
        out_specs=[
            pl.BlockSpec((tm, nkv), row),
            pl.BlockSpec((tm, nkv), row),
            pl.BlockSpec((N_KV_HEADS, tm, LANES), lambda i: (0, i, 0)),
            pl.BlockSpec((N_KV_HEADS, tm, LANES), lambda i: (0, i, 0)),
        ],
        out_shape=[
            jax.ShapeDtypeStruct((t, nkv), F32),
            jax.ShapeDtypeStruct((t, nkv), F32),
            jax.ShapeDtypeStruct((N_KV_HEADS, t, LANES), BF16),
            jax.ShapeDtypeStruct((N_KV_HEADS, t, LANES), BF16),
        ],
        compiler_params=pltpu.CompilerParams(
            dimension_semantics=("arbitrary",), vmem_limit_bytes=_vmem_limit(nbytes)),
        name="shared_kv",
    )(x, g, w_kv, kg, *rope)


def _attend(qcat, kwin, vwin, valid, sink):
    s = lax.dot_general(qcat, kwin, (((1,), (1,)), ((), ())), preferred_element_type=F32)
    s = jnp.where(valid, s, -jnp.inf)
    m = jnp.maximum(jnp.max(s, axis=-1, keepdims=True), sink)
    e = jnp.exp2(s - m)
    den = jnp.sum(e, axis=-1, keepdims=True) + jnp.exp2(sink - m)
    o = jnp.dot(e.astype(BF16), vwin, preferred_element_type=F32)
    return o / den


def _project_q(x_ref, g_ref, wq_ref, qg_ref, cos_ref, sneg_ref, spos_ref, qe_ref, qo_ref):
    xn = _rms(x_ref[...], g_ref[...]).astype(BF16)
    q = jnp.dot(xn, wq_ref[...], preferred_element_type=F32)
    lo = _half_mask((q.shape[0], LANES))
    for p in range(q.shape[1] // LANES):
        sl = slice(p * LANES, (p + 1) * LANES)
        qr = _head_norm_rope(q[:, sl], qg_ref[...], cos_ref[...], sneg_ref[...], spos_ref[...]) * (SCALE * LOG2E)
        qe_ref[:, sl] = jnp.where(lo, qr, 0.0).astype(BF16)
        qo_ref[:, sl] = jnp.where(lo, 0.0, qr).astype(BF16)


def _attend_rows(qe_ref, qo_ref, att_ref, r0, nr, kh, kwin, vwin, valid, sink):
    pairs = GROUP // 2
    cols = [slice((pairs * kh + j) * LANES, (pairs * kh + j + 1) * LANES) for j in range(pairs)]
    qcat = jnp.concatenate([qe_ref[r0:r0 + nr, c] for c in cols] + [qo_ref[r0:r0 + nr, c] for c in cols], axis=0)
    o = _attend(qcat, kwin, vwin, valid, sink)
    lo = _half_mask((nr, LANES))
    for j, c in enumerate(cols):
        att_ref[r0:r0 + nr, c] = jnp.where(lo, o[j * nr:(j + 1) * nr], o[(pairs + j) * nr:(pairs + j + 1) * nr]).astype(BF16)


def _attn_prompt_body(x_ref, g_ref, wq_ref, qg_ref, cos_ref, sneg_ref, spos_ref,
                      kprev_ref, kcur_ref, vprev_ref, vcur_ref, sink_ref, wo_ref,
                      o_ref, qe_ref, qo_ref, att_ref, kw_ref, vw_ref):
    i = pl.program_id(0)
    tm = x_ref.shape[0]
    _project_q(x_ref, g_ref, wq_ref, qg_ref, cos_ref, sneg_ref, spos_ref, qe_ref, qo_ref)
    kw_ref[:, :WINDOW, :] = kprev_ref[...]
    kw_ref[:, WINDOW:, :] = kcur_ref[...]
    vw_ref[:, :WINDOW, :] = vprev_ref[...]
    vw_ref[:, WINDOW:, :] = vcur_ref[...]
    lane = lax.broadcasted_iota(jnp.int32, (1, KEYS), 1)
    for c in range(tm // CHUNK):
        w = c // 2
        valid = (lane < 3 * CHUNK) if c % 2 == 0 else (lane >= CHUNK)
        if w == 0:
            valid = valid & ((lane >= WINDOW) | (i > 0))
        for kh in range(N_KV_HEADS):
            _attend_rows(qe_ref, qo_ref, att_ref, c * CHUNK, CHUNK, kh,
                         kw_ref[kh, w * WINDOW:w * WINDOW + KEYS, :], vw_ref[kh, w * WINDOW:w * WINDOW + KEYS, :],
                         valid, sink_ref[kh])
    o_ref[...] = x_ref[...] + jnp.dot(att_ref[...], wo_ref[...], preferred_element_type=F32)


def _attn_sample_body(x_ref, g_ref, wq_ref, qg_ref, cos_ref, sneg_ref, spos_ref,
                      kc_ref, knew_ref, vc_ref, vnew_ref, sink_ref, wo_ref,
                      o_ref, qe_ref, qo_ref, att_ref, kw_ref, vw_ref, *, seg):
    tm = x_ref.shape[0]
    _project_q(x_ref, g_ref, wq_ref, qg_ref, cos_ref, sneg_ref, spos_ref, qe_ref, qo_ref)
    nkeys = WINDOW + seg
    valid = lax.broadcasted_iota(jnp.int32, (1, KEYS), 1) < nkeys
    kw_ref[nkeys:, :] = jnp.zeros((KEYS - nkeys, LANES), BF16)
    vw_ref[nkeys:, :] = jnp.zeros((KEYS - nkeys, LANES), BF16)
    for b in range(tm // seg):
        for kh in range(N_KV_HEADS):
            kw_ref[:WINDOW, :] = kc_ref[b, kh]
            kw_ref[WINDOW:nkeys, :] = knew_ref[kh, b * seg:(b + 1) * seg, :]
            vw_ref[:WINDOW, :] = vc_ref[b, kh]
            vw_ref[WINDOW:nkeys, :] = vnew_ref[kh, b * seg:(b + 1) * seg, :]
            _attend_rows(qe_ref, qo_ref, att_ref, b * seg, seg, kh, kw_ref[...], vw_ref[...], valid, sink_ref[kh])
    o_ref[...] = x_ref[...] + jnp.dot(att_ref[...], wo_ref[...], preferred_element_type=F32)


def _attn_mixer(x, g, w_q, qg, rope, k2, v2, cache, sink_col, w_o, layer, blayer, *, tm, seg):
    t, d = x.shape
    tm = min(tm, t)
    row = lambda i: (i, 0)
    const = lambda i: (0, 0)
    resident = dict(pipeline_mode=pl.Buffered(1))
    head_specs = [
        pl.BlockSpec((tm, d), row),
        pl.BlockSpec((None, 1, d), lambda i: (layer, 0, 0)),
        pl.BlockSpec((None, d, d), lambda i: (blayer, 0, 0), **resident),
        pl.BlockSpec((1, LANES), const),
        pl.BlockSpec((tm, LANES), row),
        pl.BlockSpec((tm, LANES), row),
        pl.BlockSpec((tm, LANES), row),
    ]
    cur = pl.BlockSpec((N_KV_HEADS, tm, LANES), lambda i: (0, i, 0))
    if seg is None:
        prev = pl.BlockSpec((N_KV_HEADS, WINDOW, LANES), lambda i: (0, jnp.maximum(i * (tm // WINDOW) - 1, 0), 0))
        kv_specs, kv_args = [prev, cur, prev, cur], [k2, k2, v2, v2]
        body = _attn_prompt_body
        kwin = vwin = (N_KV_HEADS, WINDOW + tm, LANES)
        rows = GROUP * CHUNK
    else:
        nb = tm // seg
        cspec = pl.BlockSpec((nb, N_KV_HEADS, WINDOW, LANES), lambda i: (i, 0, 0, 0))
        kv_specs, kv_args = [cspec, cur, cspec, cur], [cache[0], k2, cache[1], v2]
        body = functools.partial(_attn_sample_body, seg=seg)
        kwin = vwin = (KEYS, LANES)
        rows = GROUP * seg
    tail_specs = [
        pl.BlockSpec((N_KV_HEADS, rows, 1), lambda i: (0, 0, 0)),
        pl.BlockSpec((None, d, d), lambda i: (blayer, 0, 0), **resident),
    ]
    nbytes = 2 * d * d * 2 + 4 * tm * d * 4 + tm * d * (4 + 3 * 2) + 8 * tm * LANES * 4 * 2 + 4 * 2 ** 20
    return pl.pallas_call(
        body,
        grid=(t // tm,),
        in_specs=head_specs + kv_specs + tail_specs,
        out_specs=pl.BlockSpec((tm, d), row),
        out_shape=jax.ShapeDtypeStruct((t, d), F32),
        scratch_shapes=[pltpu.VMEM((tm, d), BF16), pltpu.VMEM((tm, d), BF16), pltpu.VMEM((tm, d), BF16),
                        pltpu.VMEM(kwin, BF16), pltpu.VMEM(vwin, BF16)],
        compiler_params=pltpu.CompilerParams(
            dimension_semantics=("arbitrary",), vmem_limit_bytes=_vmem_limit(nbytes)),
        name="attn_mixer",
    )(x, g, w_q, qg, *rope, *kv_args, sink_col, w_o)


def _rope_tables(pos):
    half = ROT_DIM // 2
    inv = ROPE_THETA ** (-jnp.arange(half, dtype=F32) / half)
    ang = pos.astype(F32)[:, None] * inv[None, :]
    cos, sin = jnp.cos(ang), jnp.sin(ang)
    n = pos.shape[0]
    one = jnp.ones((n, HEAD_DIM - ROT_DIM), F32)
    zero = jnp.zeros((n, HEAD_DIM - ROT_DIM), F32)
    zh = jnp.zeros((n, half), F32)
    c = jnp.concatenate([cos, cos, one], axis=1)
    sneg = jnp.concatenate([-sin, zh, zero], axis=1)
    spos = jnp.concatenate([zh, sin, zero], axis=1)
    return tuple(jnp.tile(a, (1, LANES // HEAD_DIM)) for a in (c, sneg, spos))


def _sink_column(sinks_l, rows_per_head):
    s = (sinks_l.astype(F32) * LOG2E).reshape(N_KV_HEADS, GROUP // 2, 2).transpose(0, 2, 1)
    return jnp.repeat(s.reshape(N_KV_HEADS, GROUP), rows_per_head, axis=1)[..., None]


def _dup_heads(t):
    t = t.transpose(0, 2, 1, 3)
    return jnp.concatenate([t, t], axis=-1).astype(BF16)


def _forward(x_prompt, x_sample, state_conv, cache_k, cache_v, mix_norm_g, mlp_norm_g, w_up, w_down,
             conv_w_in, conv_w, conv_w_out, kv_norm_g, w_kv, k_norm_g, w_q, q_norm_g, sinks, w_o,
             *, tm_mlp, tf, tf_cast, tm_conv, tn, tm_attn, tm_kv):
    _, s, d = x_prompt.shape
    b, l, _ = x_sample.shape
    n_a = conv_w_in.shape[0]
    depth = w_up.shape[0]
    xp = x_prompt.reshape(s, d)
    xs = x_sample.reshape(b * l, d)

    w_in_b, w_out_b = conv_w_in.astype(BF16), conv_w_out.astype(BF16)
    w_kv_b, w_q_b, w_o_b = w_kv.astype(BF16), w_q.astype(BF16), w_o.astype(BF16)
    mix_g = mix_norm_g.reshape(depth, 1, d)
    mlp_g = mlp_norm_g.reshape(depth, 1, d)

    rope_p = _rope_tables(jnp.arange(s))
    rope_s = _rope_tables(jnp.tile(PAST_LEN + jnp.arange(l), b))
    kg = jnp.tile(k_norm_g.astype(F32), LANES // HEAD_DIM).reshape(1, LANES)

    conv_p, conv_s = [], []
    for i in range(depth):
        if i < n_a:
            xp, cp = _conv_mixer(xp, mix_g, w_in_b, conv_w, w_out_b, None, i, tm=tm_conv, tn=tn, seg=None)
            xs, cs = _conv_mixer(xs, mix_g, w_in_b, conv_w, w_out_b, state_conv, i, tm=tm_conv, tn=tn, seg=l)
            conv_p.append(cp)
            conv_s.append(cs)
        else:
            if i == n_a:
                kp, vp, k2p, v2p = _shared_kv(xp, kv_norm_g.reshape(1, d), w_kv_b, kg, rope_p, tm=tm_kv)
                ks, vs, k2s, v2s = _shared_kv(xs, kv_norm_g.reshape(1, d), w_kv_b, kg, rope_s, tm=tm_kv)
                cache2 = (_dup_heads(cache_k), _dup_heads(cache_v))
            j = i - n_a
            qg = jnp.tile(q_norm_g[j].astype(F32), LANES // HEAD_DIM).reshape(1, LANES)
            xp = _attn_mixer(xp, mix_g, w_q_b, qg, rope_p, k2p, v2p, None, _sink_column(sinks[j], CHUNK),
                             w_o_b, i, j, tm=tm_attn, seg=None)
            xs = _attn_mixer(xs, mix_g, w_q_b, qg, rope_s, k2s, v2s, cache2, _sink_column(sinks[j], l),
                             w_o_b, i, j, tm=tm_attn, seg=l)
        xs, w_up_b, w_down_b = _mlp_cast(xs, mlp_g, w_up, w_down, i, tf=tf_cast)
        xp = _mlp(xp, mlp_g, w_up_b, w_down_b, i, tm=tm_mlp, tf=tf)

    hd = (N_KV_HEADS, HEAD_DIM)
    ks_new = ks.reshape(b, l, *hd)
    vs_new = vs.reshape(b, l, *hd)
    return (xp.reshape(1, s, d), xs.reshape(b, l, d), jnp.stack(conv_p), jnp.stack(conv_s),
            kp[s - WINDOW:].reshape(1, WINDOW, *hd), vp[s - WINDOW:].reshape(1, WINDOW, *hd),
            jnp.concatenate([cache_k[:, l:], ks_new], axis=1), jnp.concatenate([cache_v[:, l:], vs_new], axis=1))


def kernel(x_prompt, x_sample, state_conv, cache_k, cache_v, mix_norm_g, mlp_norm_g, w_up, w_down, conv_w_in, conv_w, conv_w_out, kv_norm_g, w_kv, k_norm_g, w_q, q_norm_g, sinks, w_o):
    return _forward(x_prompt, x_sample, state_conv, cache_k, cache_v, mix_norm_g, mlp_norm_g, w_up, w_down,
                    conv_w_in, conv_w, conv_w_out, kv_norm_g, w_kv, k_norm_g, w_q, q_norm_g, sinks, w_o,
                    tm_mlp=512, tf=1024, tf_cast=512, tm_conv=512, tn=512, tm_attn=256, tm_kv=512)
```

```python
import functools

import jax
import jax.numpy as jnp
from jax import lax
from jax.experimental import pallas as pl
from jax.experimental.pallas import tpu as pltpu

EPS = 1e-6
CHUNK = 64
WINDOW = 128
HEAD_DIM = 64
N_KV_HEADS = 4
GROUP = 8
ROT_DIM = 16
ROPE_THETA = 500000.0
PAST_LEN = 2048
SCALE = HEAD_DIM ** -0.5
LOG2E = 1.4426950408889634

LANES = 128
KEYS = 2 * WINDOW
VMEM_LIMIT_CAP = 56 * 2 ** 20

F32 = jnp.float32
BF16 = jnp.bfloat16


def _vmem_limit(nbytes):
    return int(min(VMEM_LIMIT_CAP, max(32 * 2 ** 20, nbytes * 5 // 4 + 4 * 2 ** 20)))


def _rms(x, g):
    return x * lax.rsqrt(jnp.mean(x * x, axis=-1, keepdims=True) + EPS) * g


def _half_mask(shape):
    return lax.broadcasted_iota(jnp.int32, shape, len(shape) - 1) < HEAD_DIM


def _head_norm_rope(t, gain, cos, sneg, spos):
    lo = _half_mask(t.shape)
    sq = t * t
    s_lo = jnp.sum(jnp.where(lo, sq, 0.0), axis=-1, keepdims=True)
    s_hi = jnp.sum(jnp.where(lo, 0.0, sq), axis=-1, keepdims=True)
    inv = jnp.where(lo, lax.rsqrt(s_lo / HEAD_DIM + EPS), lax.rsqrt(s_hi / HEAD_DIM + EPS))
    tn = t * inv * gain
    half = ROT_DIM // 2
    return tn * cos + pltpu.roll(tn, LANES - half, 1) * sneg + pltpu.roll(tn, half, 1) * spos


def _mlp_body(x_ref, g_ref, wu_ref, wd_ref, o_ref, xn_ref):
    @pl.when(pl.program_id(1) == 0)
    def _():
        x = x_ref[...]
        xn_ref[...] = _rms(x, g_ref[...]).astype(BF16)
        o_ref[...] = x

    h = jnp.dot(xn_ref[...], wu_ref[...], preferred_element_type=F32)
    h = jnp.square(jnp.maximum(h, 0.0)).astype(BF16)
    o_ref[...] += jnp.dot(h, wd_ref[...], preferred_element_type=F32)


def _mlp_cast_body(x_ref, g_ref, wu_ref, wd_ref, o_ref, wub_ref, wdb_ref, xn_ref):
    @pl.when(pl.program_id(0) == 0)
    def _():
        x = x_ref[...]
        xn_ref[...] = _rms(x, g_ref[...]).astype(BF16)
        o_ref[...] = x

    wu = wu_ref[...].astype(BF16)
    wd = wd_ref[...].astype(BF16)
    wub_ref[...] = wu
    wdb_ref[...] = wd
    h = jnp.dot(xn_ref[...], wu, preferred_element_type=F32)
    h = jnp.square(jnp.maximum(h, 0.0)).astype(BF16)
    o_ref[...] += jnp.dot(h, wd, preferred_element_type=F32)


def _mlp_cast(x, g, w_up, w_down, layer, *, tf):
    t, d = x.shape
    f = w_up.shape[2]
    tf = min(tf, f)
    nbytes = 2 * t * d * 4 + t * d * 2 + 2 * 2 * d * tf * (4 + 2) + t * tf * 6
    return pl.pallas_call(
        _mlp_cast_body,
        grid=(f // tf,),
        in_specs=[
            pl.BlockSpec((t, d), lambda j: (0, 0), pipeline_mode=pl.Buffered(1)),
            pl.BlockSpec((None, 1, d), lambda j: (layer, 0, 0)),
            pl.BlockSpec((None, d, tf), lambda j: (layer, 0, j)),
            pl.BlockSpec((None, tf, d), lambda j: (layer, j, 0)),
        ],
        out_specs=[
            pl.BlockSpec((t, d), lambda j: (0, 0), pipeline_mode=pl.Buffered(1)),
            pl.BlockSpec((d, tf), lambda j: (0, j)),
            pl.BlockSpec((tf, d), lambda j: (j, 0)),
        ],
        out_shape=[
            jax.ShapeDtypeStruct((t, d), F32),
            jax.ShapeDtypeStruct((d, f), BF16),
            jax.ShapeDtypeStruct((f, d), BF16),
        ],
        scratch_shapes=[pltpu.VMEM((t, d), BF16)],
        compiler_params=pltpu.CompilerParams(
            dimension_semantics=("arbitrary",), vmem_limit_bytes=_vmem_limit(nbytes)),
        name="mlp_cast",
    )(x, g, w_up, w_down)


def _mlp(x, g, w_up, w_down, layer, *, tm, tf):
    t, d = x.shape
    f = w_up.shape[1]
    tm, tf = min(tm, t), min(tf, f)
    nbytes = 2 * (2 * tm * d * 4 + 2 * d * tf * 2) + tm * d * 2 + tm * tf * 6
    return pl.pallas_call(
        _mlp_body,
        grid=(t // tm, f // tf),
        in_specs=[
            pl.BlockSpec((tm, d), lambda i, j: (i, 0)),
            pl.BlockSpec((None, 1, d), lambda i, j: (layer, 0, 0)),
            pl.BlockSpec((d, tf), lambda i, j: (0, j)),
            pl.BlockSpec((tf, d), lambda i, j: (j, 0)),
        ],
        out_specs=pl.BlockSpec((tm, d), lambda i, j: (i, 0)),
        out_shape=jax.ShapeDtypeStruct((t, d), F32),
        scratch_shapes=[pltpu.VMEM((tm, d), BF16)],
        compiler_params=pltpu.CompilerParams(
            dimension_semantics=("arbitrary", "arbitrary"), vmem_limit_bytes=_vmem_limit(nbytes)),
        name="mlp",
    )(x, g, w_up, w_down)


def _conv_body(*refs, seg, tn):
    if seg is None:
        x_ref, g_ref, wb_ref, wc_ref, wu_ref, cw_ref, wout_ref, o_ref, zl_ref, xn_ref = refs
    else:
        x_ref, g_ref, wb_ref, wc_ref, wu_ref, cw_ref, wout_ref, st_ref, o_ref, zl_ref, xn_ref = refs
    i, j = pl.program_id(0), pl.program_id(1)

    @pl.when(j == 0)
    def _():
        x = x_ref[...]
        xn_ref[...] = _rms(x, g_ref[...]).astype(BF16)
        o_ref[...] = x

    xn = xn_ref[...]
    gate_b = jnp.dot(xn, wb_ref[...], preferred_element_type=F32)
    z = jnp.dot(xn, wc_ref[...], preferred_element_type=F32) * jnp.dot(xn, wu_ref[...], preferred_element_type=F32)
    tm = z.shape[0]
    row = lax.broadcasted_iota(jnp.int32, z.shape, 0)
    r1 = pltpu.roll(z, 1, 0)
    r2 = pltpu.roll(z, 2, 0)
    if seg is None:
        @pl.when(i == 0)
        def _():
            zl_ref[j] = jnp.zeros((2, tn), F32)

        prev = zl_ref[j]
        p0, p1 = prev[0:1, :], prev[1:2, :]
        zl_ref[j] = z[tm - 2:, :]
    else:
        nb = tm // seg
        st = st_ref[...]
        p0 = jnp.broadcast_to(st[:, 0:1, :], (nb, seg, tn)).reshape(tm, tn)
        p1 = jnp.broadcast_to(st[:, 1:2, :], (nb, seg, tn)).reshape(tm, tn)
        row = row % seg
        zl_ref[...] = z.reshape(nb, seg, tn)[:, seg - 2:, :]
    zp1 = jnp.where(row == 0, p1, r1)
    zp2 = jnp.where(row == 0, p0, jnp.where(row == 1, p1, r2))
    cw = cw_ref[...]
    conv = zp2 * cw[0:1, :] + zp1 * cw[1:2, :] + z * cw[2:3, :]
    y = (gate_b * conv).astype(BF16)
    o_ref[...] += jnp.dot(y, wout_ref[...], preferred_element_type=F32)


def _conv_mixer(x, g, w_in, cw, w_out, state, layer, *, tm, tn, seg):
    t, d = x.shape
    tm = min(tm, t)
    nj = d // tn
    in_specs = [
        pl.BlockSpec((tm, d), lambda i, j: (i, 0)),
        pl.BlockSpec((None, 1, d), lambda i, j: (layer, 0, 0)),
        pl.BlockSpec((None, d, tn), lambda i, j: (layer, 0, j)),
        pl.BlockSpec((None, d, tn), lambda i, j: (layer, 0, nj + j)),
        pl.BlockSpec((None, d, tn), lambda i, j: (layer, 0, 2 * nj + j)),
        pl.BlockSpec((None, 3, tn), lambda i, j: (layer, 0, j)),
        pl.BlockSpec((None, tn, d), lambda i, j: (layer, j, 0)),
    ]
    args = [x, g, w_in, w_in, w_in, cw, w_out]
    if seg is None:
        zl_shape = (nj, 2, tn)
        zl_spec = pl.BlockSpec((nj, 2, tn), lambda i, j: (0, 0, 0))
    else:
        nb = tm // seg
        zl_shape = (t // seg, 2, d)
        zl_spec = pl.BlockSpec((nb, 2, tn), lambda i, j: (i, 0, j))
        in_specs.append(pl.BlockSpec((None, nb, 2, tn), lambda i, j: (layer, i, 0, j)))
        args.append(state)
    nbytes = 2 * (2 * tm * d * 4 + d * 3 * tn * 2 + tn * d * 2) + tm * d * 2 + tm * tn * 4 * 8
    y, zl = pl.pallas_call(
        functools.partial(_conv_body, seg=seg, tn=tn),
        grid=(t // tm, nj),
        in_specs=in_specs,
        out_specs=[pl.BlockSpec((tm, d), lambda i, j: (i, 0)), zl_spec],
        out_shape=[jax.ShapeDtypeStruct((t, d), F32), jax.ShapeDtypeStruct(zl_shape, F32)],
        scratch_shapes=[pltpu.VMEM((tm, d), BF16)],
        compiler_params=pltpu.CompilerParams(
            dimension_semantics=("arbitrary", "arbitrary"), vmem_limit_bytes=_vmem_limit(nbytes)),
        name="conv_mixer",
    )(*args)
    if seg is None:
        zl = zl.transpose(1, 0, 2).reshape(1, 2, d)
    return y, zl


def _kv_body(x_ref, g_ref, wkv_ref, kg_ref, cos_ref, sneg_ref, spos_ref, k_ref, v_ref, ka_ref, va_ref, *, dup):
    xn = _rms(x_ref[...], g_ref[...]).astype(BF16)
    kv = jnp.dot(xn, wkv_ref[...], preferred_element_type=F32)
    nkv = N_KV_HEADS * HEAD_DIM
    lo = _half_mask((xn.shape[0], LANES))
    for p in range(nkv // LANES):
        sl = slice(p * LANES, (p + 1) * LANES)
        kr = _head_norm_rope(kv[:, sl], kg_ref[...], cos_ref[...], sneg_ref[...], spos_ref[...])
        vr = kv[:, nkv + p * LANES: nkv + (p + 1) * LANES]
        k_ref[:, sl] = kr
        v_ref[:, sl] = vr
        if dup:
            for src, dst in ((kr, ka_ref), (vr, va_ref)):
                sw = pltpu.roll(src, HEAD_DIM, 1)
                dst[2 * p] = jnp.where(lo, src, sw).astype(BF16)
                dst[2 * p + 1] = jnp.where(lo, sw, src).astype(BF16)
        else:
            ka_ref[:, sl] = kr.astype(BF16)
            va_ref[sl, :] = vr.T.astype(BF16)


def _shared_kv(x, g, w_kv, kg, rope, *, tm, dup):
    t, d = x.shape
    tm = min(tm, t)
    nkv = N_KV_HEADS * HEAD_DIM
    row = lambda i: (i, 0)
    const = lambda i: (0, 0)
    if dup:
        aux_specs = [pl.BlockSpec((N_KV_HEADS, tm, LANES), lambda i: (0, i, 0))] * 2
        aux_shapes = [jax.ShapeDtypeStruct((N_KV_HEADS, t, LANES), BF16)] * 2
    else:
        aux_specs = [pl.BlockSpec((tm, nkv), row), pl.BlockSpec((nkv, tm), lambda i: (0, i))]
        aux_shapes = [jax.ShapeDtypeStruct((t, nkv), BF16), jax.ShapeDtypeStruct((nkv, t), BF16)]
    nbytes = 2 * (tm * d * 4 + d * 2 * nkv * 2 + 5 * tm * nkv * 4) + tm * d * 8
    return pl.pallas_call(
        functools.partial(_kv_body, dup=dup),
        grid=(t // tm,),
        in_specs=[
            pl.BlockSpec((tm, d), row),
            pl.BlockSpec((1, d), const),
            pl.BlockSpec((d, 2 * nkv), const),
            pl.BlockSpec((1, LANES), const),
            pl.BlockSpec((tm, LANES), row),
            pl.BlockSpec((tm, LANES), row),
            pl.BlockSpec((tm, LANES), row),
        ],
        out_specs=[pl.BlockSpec((tm, nkv), row), pl.BlockSpec((tm, nkv), row)] + aux_specs,
        out_shape=[jax.ShapeDtypeStruct((t, nkv), F32), jax.ShapeDtypeStruct((t, nkv), F32)] + aux_shapes,
        compiler_params=pltpu.CompilerParams(
            dimension_semantics=("arbitrary",), vmem_limit_bytes=_vmem_limit(nbytes)),
        name="shared_kv",
    )(x, g, w_kv, kg, *rope)


def _attend(qcat, kwin, vwin, valid, sink):
    s = lax.dot_general(qcat, kwin, (((1,), (1,)), ((), ())), preferred_element_type=F32)
    s = jnp.where(valid, s, -jnp.inf)
    m = jnp.maximum(jnp.max(s, axis=-1, keepdims=True), sink)
    e = jnp.exp2(s - m)
    den = jnp.sum(e, axis=-1, keepdims=True) + jnp.exp2(sink - m)
    o = jnp.dot(e.astype(BF16), vwin, preferred_element_type=F32)
    return o / den


def _project_q(x_ref, g_ref, wq_ref, qg_ref, cos_ref, sneg_ref, spos_ref, qe_ref, qo_ref):
    xn = _rms(x_ref[...], g_ref[...]).astype(BF16)
    q = jnp.dot(xn, wq_ref[...], preferred_element_type=F32)
    lo = _half_mask((q.shape[0], LANES))
    for p in range(q.shape[1] // LANES):
        sl = slice(p * LANES, (p + 1) * LANES)
        qr = _head_norm_rope(q[:, sl], qg_ref[...], cos_ref[...], sneg_ref[...], spos_ref[...]) * (SCALE * LOG2E)
        qe_ref[:, sl] = jnp.where(lo, qr, 0.0).astype(BF16)
        qo_ref[:, sl] = jnp.where(lo, 0.0, qr).astype(BF16)


def _attend_rows(qe_ref, qo_ref, att_ref, r0, nr, kh, kwin, vwin, valid, sink):
    pairs = GROUP // 2
    cols = [slice((pairs * kh + j) * LANES, (pairs * kh + j + 1) * LANES) for j in range(pairs)]
    qcat = jnp.concatenate([qe_ref[r0:r0 + nr, c] for c in cols] + [qo_ref[r0:r0 + nr, c] for c in cols], axis=0)
    o = _attend(qcat, kwin, vwin, valid, sink)
    lo = _half_mask((nr, LANES))
    for j, c in enumerate(cols):
        att_ref[r0:r0 + nr, c] = jnp.where(lo, o[j * nr:(j + 1) * nr], o[(pairs + j) * nr:(pairs + j + 1) * nr]).astype(BF16)


def _attn_prompt_body(x_ref, g_ref, wqt_ref, qg_ref, cos_ref, sin_ref,
                      kprev_ref, kcur_ref, vprev_ref, vcur_ref, sink_ref, wot_ref,
                      o_ref, qt_ref, att_ref, pt_ref, kw_ref, vw_ref):
    i = pl.program_id(0)
    tm, d = x_ref.shape
    half = ROT_DIM // 2
    x = x_ref[...]
    xn = _rms(x, g_ref[...]).astype(BF16)
    qt = lax.dot_general(wqt_ref[...], xn, (((1,), (1,)), ((), ())), preferred_element_type=F32)
    cos, sin = cos_ref[...], sin_ref[...]
    gain = jnp.concatenate([qg_ref[...]] * (tm // LANES), axis=1)
    for h in range(d // HEAD_DIM):
        t = qt[h * HEAD_DIM:(h + 1) * HEAD_DIM, :]
        tn = t * lax.rsqrt(jnp.sum(t * t, axis=0, keepdims=True) / HEAD_DIM + EPS) * gain
        x1, x2 = tn[:half], tn[half:ROT_DIM]
        qt_ref[h * HEAD_DIM:(h + 1) * HEAD_DIM, :] = jnp.concatenate(
            [x1 * cos - x2 * sin, x2 * cos + x1 * sin, tn[ROT_DIM:]], axis=0).astype(BF16)

    kw_ref[:WINDOW, :] = kprev_ref[...]
    kw_ref[WINDOW:, :] = kcur_ref[...]
    vw_ref[:, :WINDOW] = vprev_ref[...]
    vw_ref[:, WINDOW:] = vcur_ref[...]

    first_query_chunk = lax.broadcasted_iota(jnp.int32, (CHUNK, LANES), 1) < CHUNK
    has_past = jnp.broadcast_to(i > 0, (CHUNK, LANES))
    zeros = jnp.zeros((HEAD_DIM, LANES), BF16)
    n = 0
    for p in range(tm // LANES):
        qs = slice(p * LANES, (p + 1) * LANES)
        masks = [first_query_chunk & has_past if p == 0 else first_query_chunk, has_past if p == 0 else None,
                 None, ~first_query_chunk]
        for kh in range(N_KV_HEADS):
            heads = range(GROUP * kh, GROUP * (kh + 1))
            kwin = kw_ref[p * LANES:p * LANES + KEYS, (kh // 2) * LANES:(kh // 2 + 1) * LANES]
            rhs = jnp.concatenate(
                [jnp.concatenate([qt_ref[h * HEAD_DIM:(h + 1) * HEAD_DIM, qs], zeros] if kh % 2 == 0 else
                                 [zeros, qt_ref[h * HEAD_DIM:(h + 1) * HEAD_DIM, qs]], axis=0)
                 for h in heads], axis=1)
            st = jnp.dot(kwin, rhs, preferred_element_type=F32)
            slot = n % 2
            n += 1
            dens = []
            for j, h in enumerate(heads):
                s = jnp.concatenate(
                    [st[c * CHUNK:(c + 1) * CHUNK, j * LANES:(j + 1) * LANES] if mask is None else
                     jnp.where(mask, st[c * CHUNK:(c + 1) * CHUNK, j * LANES:(j + 1) * LANES], -jnp.inf)
                     for c, mask in enumerate(masks)], axis=0)
                sink = sink_ref[h:h + 1, :]
                m = jnp.maximum(jnp.max(s, axis=0, keepdims=True), sink)
                e = jnp.exp2(s - m)
                dens.append(jnp.sum(e, axis=0, keepdims=True) + jnp.exp2(sink - m))
                pt_ref[slot, :, j * LANES:(j + 1) * LANES] = e.astype(BF16)
            ot = jnp.dot(vw_ref[kh * HEAD_DIM:(kh + 1) * HEAD_DIM, p * LANES:p * LANES + KEYS], pt_ref[slot],
                         preferred_element_type=F32)
            ot = ot * (1.0 / jnp.concatenate(dens, axis=1))
            for j, h in enumerate(heads):
                att_ref[h * HEAD_DIM:(h + 1) * HEAD_DIM, qs] = ot[:, j * LANES:(j + 1) * LANES].astype(BF16)
    out_t = jnp.dot(wot_ref[...], att_ref[...], preferred_element_type=F32)
    o_ref[...] = x + out_t.T


def _attn_sample_body(x_ref, g_ref, wq_ref, qg_ref, cos_ref, sneg_ref, spos_ref,
                      kc_ref, knew_ref, vc_ref, vnew_ref, sink_ref, wo_ref,
                      o_ref, qe_ref, qo_ref, att_ref, kw_ref, vw_ref, *, seg):
    tm = x_ref.shape[0]
    _project_q(x_ref, g_ref, wq_ref, qg_ref, cos_ref, sneg_ref, spos_ref, qe_ref, qo_ref)
    nkeys = WINDOW + seg
    valid = lax.broadcasted_iota(jnp.int32, (1, KEYS), 1) < nkeys
    kw_ref[nkeys:, :] = jnp.zeros((KEYS - nkeys, LANES), BF16)
    vw_ref[nkeys:, :] = jnp.zeros((KEYS - nkeys, LANES), BF16)
    for b in range(tm // seg):
        for kh in range(N_KV_HEADS):
            kw_ref[:WINDOW, :] = kc_ref[b, kh]
            kw_ref[WINDOW:nkeys, :] = knew_ref[kh, b * seg:(b + 1) * seg, :]
            vw_ref[:WINDOW, :] = vc_ref[b, kh]
            vw_ref[WINDOW:nkeys, :] = vnew_ref[kh, b * seg:(b + 1) * seg, :]
            _attend_rows(qe_ref, qo_ref, att_ref, b * seg, seg, kh, kw_ref[...], vw_ref[...], valid, sink_ref[kh])
    o_ref[...] = x_ref[...] + jnp.dot(att_ref[...], wo_ref[...], preferred_element_type=F32)


def _attn_mixer_sample(x, g, w_q, qg, rope, k2, v2, cache, sink_col, w_o, layer, blayer, *, tm, seg):
    t, d = x.shape
    tm = min(tm, t)
    nb = tm // seg
    row = lambda i: (i, 0)
    const = lambda i: (0, 0)
    resident = dict(pipeline_mode=pl.Buffered(1))
    cur = pl.BlockSpec((N_KV_HEADS, tm, LANES), lambda i: (0, i, 0))
    cspec = pl.BlockSpec((nb, N_KV_HEADS, WINDOW, LANES), lambda i: (i, 0, 0, 0))
    in_specs = [
        pl.BlockSpec((tm, d), row),
        pl.BlockSpec((None, 1, d), lambda i: (layer, 0, 0)),
        pl.BlockSpec((None, d, d), lambda i: (blayer, 0, 0), **resident),
        pl.BlockSpec((1, LANES), const),
        pl.BlockSpec((tm, LANES), row),
        pl.BlockSpec((tm, LANES), row),
        pl.BlockSpec((tm, LANES), row),
        cspec, cur, cspec, cur,
        pl.BlockSpec((N_KV_HEADS, GROUP * seg, 1), lambda i: (0, 0, 0)),
        pl.BlockSpec((None, d, d), lambda i: (blayer, 0, 0), **resident),
    ]
    nbytes = 2 * d * d * 2 + 4 * tm * d * 4 + tm * d * (4 + 3 * 2) + 8 * tm * LANES * 4 * 2 + 4 * 2 ** 20
    return pl.pallas_call(
        functools.partial(_attn_sample_body, seg=seg),
        grid=(t // tm,),
        in_specs=in_specs,
        out_specs=pl.BlockSpec((tm, d), row),
        out_shape=jax.ShapeDtypeStruct((t, d), F32),
        scratch_shapes=[pltpu.VMEM((tm, d), BF16), pltpu.VMEM((tm, d), BF16), pltpu.VMEM((tm, d), BF16),
                        pltpu.VMEM((KEYS, LANES), BF16), pltpu.VMEM((KEYS, LANES), BF16)],
        compiler_params=pltpu.CompilerParams(
            dimension_semantics=("arbitrary",), vmem_limit_bytes=_vmem_limit(nbytes)),
        name="attn_mixer_sample",
    )(x, g, w_q, qg, *rope, cache[0], k2, cache[1], v2, sink_col, w_o)


def _attn_mixer_prompt(x, g, w_qt, qg, cos_t, sin_t, kb, vt, sink_rows, w_ot, layer, blayer, *, tm):
    t, d = x.shape
    tm = min(tm, t)
    nkv = N_KV_HEADS * HEAD_DIM
    nh = d // HEAD_DIM
    half = ROT_DIM // 2
    prev_blk = lambda i: jnp.maximum(i * (tm // WINDOW) - 1, 0)
    resident = dict(pipeline_mode=pl.Buffered(1))
    in_specs = [
        pl.BlockSpec((tm, d), lambda i: (i, 0)),
        pl.BlockSpec((None, 1, d), lambda i: (layer, 0, 0)),
        pl.BlockSpec((None, d, d), lambda i: (blayer, 0, 0), **resident),
        pl.BlockSpec((HEAD_DIM, LANES), lambda i: (0, 0)),
        pl.BlockSpec((half, tm), lambda i: (0, i)),
        pl.BlockSpec((half, tm), lambda i: (0, i)),
        pl.BlockSpec((WINDOW, nkv), lambda i: (prev_blk(i), 0)),
        pl.BlockSpec((tm, nkv), lambda i: (i, 0)),
        pl.BlockSpec((nkv, WINDOW), lambda i: (0, prev_blk(i))),
        pl.BlockSpec((nkv, tm), lambda i: (0, i)),
        pl.BlockSpec((nh, LANES), lambda i: (0, 0)),
        pl.BlockSpec((None, d, d), lambda i: (blayer, 0, 0), **resident),
    ]
    nbytes = (2 * d * d * 2 + 4 * tm * d * 4 + 3 * tm * d * 4 + 2 * tm * d * 2 + 2 * KEYS * GROUP * LANES * 2
              + KEYS * GROUP * LANES * 4 * 2 + 4 * (WINDOW + tm) * nkv * 2)
    return pl.pallas_call(
        _attn_prompt_body,
        grid=(t // tm,),
        in_specs=in_specs,
        out_specs=pl.BlockSpec((tm, d), lambda i: (i, 0)),
        out_shape=jax.ShapeDtypeStruct((t, d), F32),
        scratch_shapes=[pltpu.VMEM((d, tm), BF16), pltpu.VMEM((d, tm), BF16),
                        pltpu.VMEM((2, KEYS, GROUP * LANES), BF16),
                        pltpu.VMEM((WINDOW + tm, nkv), BF16), pltpu.VMEM((nkv, WINDOW + tm), BF16)],
        compiler_params=pltpu.CompilerParams(
            dimension_semantics=("arbitrary",), vmem_limit_bytes=_vmem_limit(nbytes)),
        name="attn_mixer_prompt",
    )(x, g, w_qt, qg, cos_t, sin_t, kb, kb, vt, vt, sink_rows, w_ot)


def _rope_tables(pos):
    half = ROT_DIM // 2
    inv = ROPE_THETA ** (-jnp.arange(half, dtype=F32) / half)
    ang = pos.astype(F32)[:, None] * inv[None, :]
    cos, sin = jnp.cos(ang), jnp.sin(ang)
    n = pos.shape[0]
    one = jnp.ones((n, HEAD_DIM - ROT_DIM), F32)
    zero = jnp.zeros((n, HEAD_DIM - ROT_DIM), F32)
    zh = jnp.zeros((n, half), F32)
    c = jnp.concatenate([cos, cos, one], axis=1)
    sneg = jnp.concatenate([-sin, zh, zero], axis=1)
    spos = jnp.concatenate([zh, sin, zero], axis=1)
    return tuple(jnp.tile(a, (1, LANES // HEAD_DIM)) for a in (c, sneg, spos))


def _sink_column(sinks_l, rows_per_head):
    s = (sinks_l.astype(F32) * LOG2E).reshape(N_KV_HEADS, GROUP // 2, 2).transpose(0, 2, 1)
    return jnp.repeat(s.reshape(N_KV_HEADS, GROUP), rows_per_head, axis=1)[..., None]


def _dup_heads(t):
    t = t.transpose(0, 2, 1, 3)
    return jnp.concatenate([t, t], axis=-1).astype(BF16)


def _forward(x_prompt, x_sample, state_conv, cache_k, cache_v, mix_norm_g, mlp_norm_g, w_up, w_down,
             conv_w_in, conv_w, conv_w_out, kv_norm_g, w_kv, k_norm_g, w_q, q_norm_g, sinks, w_o,
             *, tm_mlp, tf, tf_cast, tm_conv, tn, tm_attn, tm_attn_s, tm_kv):
    _, s, d = x_prompt.shape
    b, l, _ = x_sample.shape
    n_a = conv_w_in.shape[0]
    depth = w_up.shape[0]
    xp = x_prompt.reshape(s, d)
    xs = x_sample.reshape(b * l, d)

    w_in_b, w_out_b = conv_w_in.astype(BF16), conv_w_out.astype(BF16)
    w_kv_b, w_q_b, w_o_b = w_kv.astype(BF16), w_q.astype(BF16), w_o.astype(BF16)
    w_qt_b, w_ot_b = w_q.transpose(0, 2, 1).astype(BF16), w_o.transpose(0, 2, 1).astype(BF16)
    mix_g = mix_norm_g.reshape(depth, 1, d)
    mlp_g = mlp_norm_g.reshape(depth, 1, d)

    half = ROT_DIM // 2
    ang_t = (ROPE_THETA ** (-jnp.arange(half, dtype=F32) / half))[:, None] * jnp.arange(s).astype(F32)[None, :]
    cos_t, sin_t = jnp.cos(ang_t), jnp.sin(ang_t)
    rope_p = _rope_tables(jnp.arange(s))
    rope_s = _rope_tables(jnp.tile(PAST_LEN + jnp.arange(l), b))
    kg = jnp.tile(k_norm_g.astype(F32), LANES // HEAD_DIM).reshape(1, LANES)

    conv_p, conv_s = [], []
    for i in range(depth):
        if i < n_a:
            xp, cp = _conv_mixer(xp, mix_g, w_in_b, conv_w, w_out_b, None, i, tm=tm_conv, tn=tn, seg=None)
            xs, cs = _conv_mixer(xs, mix_g, w_in_b, conv_w, w_out_b, state_conv, i, tm=tm_conv, tn=tn, seg=l)
            conv_p.append(cp)
            conv_s.append(cs)
        else:
            if i == n_a:
                kp, vp, kbp, vtp = _shared_kv(xp, kv_norm_g.reshape(1, d), w_kv_b, kg, rope_p, tm=tm_kv, dup=False)
                ks, vs, k2s, v2s = _shared_kv(xs, kv_norm_g.reshape(1, d), w_kv_b, kg, rope_s, tm=tm_kv, dup=True)
                cache2 = (_dup_heads(cache_k), _dup_heads(cache_v))
            j = i - n_a
            qg = jnp.tile(q_norm_g[j].astype(F32), LANES // HEAD_DIM).reshape(1, LANES)
            qg_t = jnp.broadcast_to((q_norm_g[j].astype(F32) * (SCALE * LOG2E))[:, None], (HEAD_DIM, LANES))
            sink_rows = jnp.broadcast_to((sinks[j].astype(F32) * LOG2E)[:, None], (sinks.shape[1], LANES))
            xp = _attn_mixer_prompt(xp, mix_g, w_qt_b, qg_t, cos_t, sin_t, kbp, vtp, sink_rows, w_ot_b, i, j, tm=tm_attn)
            xs = _attn_mixer_sample(xs, mix_g, w_q_b, qg, rope_s, k2s, v2s, cache2, _sink_column(sinks[j], l),
                                    w_o_b, i, j, tm=tm_attn_s, seg=l)
        xs, w_up_b, w_down_b = _mlp_cast(xs, mlp_g, w_up, w_down, i, tf=tf_cast)
        xp = _mlp(xp, mlp_g, w_up_b, w_down_b, i, tm=tm_mlp, tf=tf)

    hd = (N_KV_HEADS, HEAD_DIM)
    ks_new = ks.reshape(b, l, *hd)
    vs_new = vs.reshape(b, l, *hd)
    return (xp.reshape(1, s, d), xs.reshape(b, l, d), jnp.stack(conv_p), jnp.stack(conv_s),
            kp[s - WINDOW:].reshape(1, WINDOW, *hd), vp[s - WINDOW:].reshape(1, WINDOW, *hd),
            jnp.concatenate([cache_k[:, l:], ks_new], axis=1), jnp.concatenate([cache_v[:, l:], vs_new], axis=1))


def kernel(x_prompt, x_sample, state_conv, cache_k, cache_v, mix_norm_g, mlp_norm_g, w_up, w_down, conv_w_in, conv_w, conv_w_out, kv_norm_g, w_kv, k_norm_g, w_q, q_norm_g, sinks, w_o):
    return _forward(x_prompt, x_sample, state_conv, cache_k, cache_v, mix_norm_g, mlp_norm_g, w_up, w_down,
                    conv_w_in, conv_w, conv_w_out, kv_norm_g, w_kv, k_norm_g, w_q, q_norm_g, sinks, w_o,
                    tm_mlp=512, tf=1024, tf_cast=512, tm_conv=512, tn=512, tm_attn=512, tm_attn_s=256, tm_kv=512)
```

```python
import functools

import jax
import jax.numpy as jnp
from jax import lax
from jax.experimental import pallas as pl
from jax.experimental.pallas import tpu as pltpu

EPS = 1e-6
CHUNK = 64
WINDOW = 128
HEAD_DIM = 64
N_KV_HEADS = 4
GROUP = 8
ROT_DIM = 16
ROPE_THETA = 500000.0
PAST_LEN = 2048
SCALE = HEAD_DIM ** -0.5
LOG2E = 1.4426950408889634

LANES = 128
KEYS = 2 * WINDOW
VMEM_LIMIT_CAP = 56 * 2 ** 20

F32 = jnp.float32
BF16 = jnp.bfloat16


def _vmem_limit(nbytes):
    return int(min(VMEM_LIMIT_CAP, max(32 * 2 ** 20, nbytes * 5 // 4 + 4 * 2 ** 20)))


def _rms(x, g):
    return x * lax.rsqrt(jnp.mean(x * x, axis=-1, keepdims=True) + EPS) * g


def _half_mask(shape):
    return lax.broadcasted_iota(jnp.int32, shape, len(shape) - 1) < HEAD_DIM


def _head_norm_rope(t, gain, cos, sneg, spos):
    lo = _half_mask(t.shape)
    sq = t * t
    s_lo = jnp.sum(jnp.where(lo, sq, 0.0), axis=-1, keepdims=True)
    s_hi = jnp.sum(jnp.where(lo, 0.0, sq), axis=-1, keepdims=True)
    inv = jnp.where(lo, lax.rsqrt(s_lo / HEAD_DIM + EPS), lax.rsqrt(s_hi / HEAD_DIM + EPS))
    tn = t * inv * gain
    half = ROT_DIM // 2
    return tn * cos + pltpu.roll(tn, LANES - half, 1) * sneg + pltpu.roll(tn, half, 1) * spos


def _mlp_body(x_ref, g_ref, wu_ref, wd_ref, o_ref, xn_ref):
    @pl.when(pl.program_id(1) == 0)
    def _():
        x = x_ref[...]
        xn_ref[...] = _rms(x, g_ref[...]).astype(BF16)
        o_ref[...] = x

    h = jnp.dot(xn_ref[...], wu_ref[...], preferred_element_type=F32)
    h = jnp.square(jnp.maximum(h, 0.0)).astype(BF16)
    o_ref[...] += jnp.dot(h, wd_ref[...], preferred_element_type=F32)


def _mlp_cast_body(x_ref, g_ref, wu_ref, wd_ref, o_ref, wub_ref, wdb_ref, xn_ref):
    @pl.when(pl.program_id(0) == 0)
    def _():
        x = x_ref[...]
        xn_ref[...] = _rms(x, g_ref[...]).astype(BF16)
        o_ref[...] = x

    wu = wu_ref[...].astype(BF16)
    wd = wd_ref[...].astype(BF16)
    wub_ref[...] = wu
    wdb_ref[...] = wd
    h = jnp.dot(xn_ref[...], wu, preferred_element_type=F32)
    h = jnp.square(jnp.maximum(h, 0.0)).astype(BF16)
    o_ref[...] += jnp.dot(h, wd, preferred_element_type=F32)


def _mlp_cast(x, g, w_up, w_down, layer, *, tf):
    t, d = x.shape
    f = w_up.shape[2]
    tf = min(tf, f)
    nbytes = 2 * t * d * 4 + t * d * 2 + 2 * 2 * d * tf * (4 + 2) + t * tf * 6
    return pl.pallas_call(
        _mlp_cast_body,
        grid=(f // tf,),
        in_specs=[
            pl.BlockSpec((t, d), lambda j: (0, 0), pipeline_mode=pl.Buffered(1)),
            pl.BlockSpec((None, 1, d), lambda j: (layer, 0, 0)),
            pl.BlockSpec((None, d, tf), lambda j: (layer, 0, j)),
            pl.BlockSpec((None, tf, d), lambda j: (layer, j, 0)),
        ],
        out_specs=[
            pl.BlockSpec((t, d), lambda j: (0, 0), pipeline_mode=pl.Buffered(1)),
            pl.BlockSpec((d, tf), lambda j: (0, j)),
            pl.BlockSpec((tf, d), lambda j: (j, 0)),
        ],
        out_shape=[
            jax.ShapeDtypeStruct((t, d), F32),
            jax.ShapeDtypeStruct((d, f), BF16),
            jax.ShapeDtypeStruct((f, d), BF16),
        ],
        scratch_shapes=[pltpu.VMEM((t, d), BF16)],
        compiler_params=pltpu.CompilerParams(
            dimension_semantics=("arbitrary",), vmem_limit_bytes=_vmem_limit(nbytes)),
        name="mlp_cast",
    )(x, g, w_up, w_down)


def _mlp(x, g, w_up, w_down, layer, *, tm, tf):
    t, d = x.shape
    f = w_up.shape[1]
    tm, tf = min(tm, t), min(tf, f)
    nbytes = 2 * (2 * tm * d * 4 + 2 * d * tf * 2) + tm * d * 2 + tm * tf * 6
    return pl.pallas_call(
        _mlp_body,
        grid=(t // tm, f // tf),
        in_specs=[
            pl.BlockSpec((tm, d), lambda i, j: (i, 0)),
            pl.BlockSpec((None, 1, d), lambda i, j: (layer, 0, 0)),
            pl.BlockSpec((d, tf), lambda i, j: (0, j)),
            pl.BlockSpec((tf, d), lambda i, j: (j, 0)),
        ],
        out_specs=pl.BlockSpec((tm, d), lambda i, j: (i, 0)),
        out_shape=jax.ShapeDtypeStruct((t, d), F32),
        scratch_shapes=[pltpu.VMEM((tm, d), BF16)],
        compiler_params=pltpu.CompilerParams(
            dimension_semantics=("arbitrary", "arbitrary"), vmem_limit_bytes=_vmem_limit(nbytes)),
        name="mlp",
    )(x, g, w_up, w_down)


def _conv_body(*refs, seg, tn):
    if seg is None:
        x_ref, g_ref, wb_ref, wc_ref, wu_ref, cw_ref, wout_ref, o_ref, zl_ref, xn_ref = refs
    else:
        x_ref, g_ref, wb_ref, wc_ref, wu_ref, cw_ref, wout_ref, st_ref, o_ref, zl_ref, xn_ref = refs
    i, j = pl.program_id(0), pl.program_id(1)

    @pl.when(j == 0)
    def _():
        x = x_ref[...]
        xn_ref[...] = _rms(x, g_ref[...]).astype(BF16)
        o_ref[...] = x

    xn = xn_ref[...]
    gate_b = jnp.dot(xn, wb_ref[...], preferred_element_type=F32)
    z = jnp.dot(xn, wc_ref[...], preferred_element_type=F32) * jnp.dot(xn, wu_ref[...], preferred_element_type=F32)
    tm = z.shape[0]
    row = lax.broadcasted_iota(jnp.int32, z.shape, 0)
    r1 = pltpu.roll(z, 1, 0)
    r2 = pltpu.roll(z, 2, 0)
    if seg is None:
        @pl.when(i == 0)
        def _():
            zl_ref[j] = jnp.zeros((2, tn), F32)

        prev = zl_ref[j]
        p0, p1 = prev[0:1, :], prev[1:2, :]
        zl_ref[j] = z[tm - 2:, :]
    else:
        nb = tm // seg
        st = st_ref[...]
        p0 = jnp.broadcast_to(st[:, 0:1, :], (nb, seg, tn)).reshape(tm, tn)
        p1 = jnp.broadcast_to(st[:, 1:2, :], (nb, seg, tn)).reshape(tm, tn)
        row = row % seg
        zl_ref[...] = z.reshape(nb, seg, tn)[:, seg - 2:, :]
    zp1 = jnp.where(row == 0, p1, r1)
    zp2 = jnp.where(row == 0, p0, jnp.where(row == 1, p1, r2))
    cw = cw_ref[...]
    conv = zp2 * cw[0:1, :] + zp1 * cw[1:2, :] + z * cw[2:3, :]
    y = (gate_b * conv).astype(BF16)
    o_ref[...] += jnp.dot(y, wout_ref[...], preferred_element_type=F32)


def _conv_mixer(x, g, w_in, cw, w_out, state, layer, *, tm, tn, seg):
    t, d = x.shape
    tm = min(tm, t)
    nj = d // tn
    in_specs = [
        pl.BlockSpec((tm, d), lambda i, j: (i, 0)),
        pl.BlockSpec((None, 1, d), lambda i, j: (layer, 0, 0)),
        pl.BlockSpec((None, d, tn), lambda i, j: (layer, 0, j)),
        pl.BlockSpec((None, d, tn), lambda i, j: (layer, 0, nj + j)),
        pl.BlockSpec((None, d, tn), lambda i, j: (layer, 0, 2 * nj + j)),
        pl.BlockSpec((None, 3, tn), lambda i, j: (layer, 0, j)),
        pl.BlockSpec((None, tn, d), lambda i, j: (layer, j, 0)),
    ]
    args = [x, g, w_in, w_in, w_in, cw, w_out]
    if seg is None:
        zl_shape = (nj, 2, tn)
        zl_spec = pl.BlockSpec((nj, 2, tn), lambda i, j: (0, 0, 0))
    else:
        nb = tm // seg
        zl_shape = (t // seg, 2, d)
        zl_spec = pl.BlockSpec((nb, 2, tn), lambda i, j: (i, 0, j))
        in_specs.append(pl.BlockSpec((None, nb, 2, tn), lambda i, j: (layer, i, 0, j)))
        args.append(state)
    nbytes = 2 * (2 * tm * d * 4 + d * 3 * tn * 2 + tn * d * 2) + tm * d * 2 + tm * tn * 4 * 8
    y, zl = pl.pallas_call(
        functools.partial(_conv_body, seg=seg, tn=tn),
        grid=(t // tm, nj),
        in_specs=in_specs,
        out_specs=[pl.BlockSpec((tm, d), lambda i, j: (i, 0)), zl_spec],
        out_shape=[jax.ShapeDtypeStruct((t, d), F32), jax.ShapeDtypeStruct(zl_shape, F32)],
        scratch_shapes=[pltpu.VMEM((tm, d), BF16)],
        compiler_params=pltpu.CompilerParams(
            dimension_semantics=("arbitrary", "arbitrary"), vmem_limit_bytes=_vmem_limit(nbytes)),
        name="conv_mixer",
    )(*args)
    if seg is None:
        zl = zl.transpose(1, 0, 2).reshape(1, 2, d)
    return y, zl


def _kv_body(x_ref, g_ref, wkv_ref, kg_ref, cos_ref, sneg_ref, spos_ref, k_ref, v_ref, ka_ref, va_ref, *, dup):
    xn = _rms(x_ref[...], g_ref[...]).astype(BF16)
    kv = jnp.dot(xn, wkv_ref[...], preferred_element_type=F32)
    nkv = N_KV_HEADS * HEAD_DIM
    lo = _half_mask((xn.shape[0], LANES))
    for p in range(nkv // LANES):
        sl = slice(p * LANES, (p + 1) * LANES)
        kr = _head_norm_rope(kv[:, sl], kg_ref[...], cos_ref[...], sneg_ref[...], spos_ref[...])
        vr = kv[:, nkv + p * LANES: nkv + (p + 1) * LANES]
        k_ref[:, sl] = kr
        v_ref[:, sl] = vr
        if dup:
            for src, dst in ((kr, ka_ref), (vr, va_ref)):
                sw = pltpu.roll(src, HEAD_DIM, 1)
                dst[2 * p] = jnp.where(lo, src, sw).astype(BF16)
                dst[2 * p + 1] = jnp.where(lo, sw, src).astype(BF16)
        else:
            ka_ref[:, sl] = kr.astype(BF16)
            va_ref[sl, :] = vr.T.astype(BF16)


def _shared_kv(x, g, w_kv, kg, rope, *, tm, dup):
    t, d = x.shape
    tm = min(tm, t)
    nkv = N_KV_HEADS * HEAD_DIM
    row = lambda i: (i, 0)
    const = lambda i: (0, 0)
    if dup:
        aux_specs = [pl.BlockSpec((N_KV_HEADS, tm, LANES), lambda i: (0, i, 0))] * 2
        aux_shapes = [jax.ShapeDtypeStruct((N_KV_HEADS, t, LANES), BF16)] * 2
    else:
        aux_specs = [pl.BlockSpec((tm, nkv), row), pl.BlockSpec((nkv, tm), lambda i: (0, i))]
        aux_shapes = [jax.ShapeDtypeStruct((t, nkv), BF16), jax.ShapeDtypeStruct((nkv, t), BF16)]
    nbytes = 2 * (tm * d * 4 + d * 2 * nkv * 2 + 5 * tm * nkv * 4) + tm * d * 8
    return pl.pallas_call(
        functools.partial(_kv_body, dup=dup),
        grid=(t // tm,),
        in_specs=[
            pl.BlockSpec((tm, d), row),
            pl.BlockSpec((1, d), const),
            pl.BlockSpec((d, 2 * nkv), const),
            pl.BlockSpec((1, LANES), const),
            pl.BlockSpec((tm, LANES), row),
            pl.BlockSpec((tm, LANES), row),
            pl.BlockSpec((tm, LANES), row),
        ],
        out_specs=[pl.BlockSpec((tm, nkv), row), pl.BlockSpec((tm, nkv), row)] + aux_specs,
        out_shape=[jax.ShapeDtypeStruct((t, nkv), F32), jax.ShapeDtypeStruct((t, nkv), F32)] + aux_shapes,
        compiler_params=pltpu.CompilerParams(
            dimension_semantics=("arbitrary",), vmem_limit_bytes=_vmem_limit(nbytes)),
        name="shared_kv",
    )(x, g, w_kv, kg, *rope)


def _attend(qcat, kwin, vwin, valid, sink):
    s = lax.dot_general(qcat, kwin, (((1,), (1,)), ((), ())), preferred_element_type=F32)
    s = jnp.where(valid, s, -jnp.inf)
    m = jnp.maximum(jnp.max(s, axis=-1, keepdims=True), sink)
    e = jnp.exp2(s - m)
    den = jnp.sum(e, axis=-1, keepdims=True) + jnp.exp2(sink - m)
    o = jnp.dot(e.astype(BF16), vwin, preferred_element_type=F32)
    return o / den


def _project_q(x_ref, g_ref, wq_ref, qg_ref, cos_ref, sneg_ref, spos_ref, qe_ref, qo_ref):
    xn = _rms(x_ref[...], g_ref[...]).astype(BF16)
    q = jnp.dot(xn, wq_ref[...], preferred_element_type=F32)
    lo = _half_mask((q.shape[0], LANES))
    for p in range(q.shape[1] // LANES):
        sl = slice(p * LANES, (p + 1) * LANES)
        qr = _head_norm_rope(q[:, sl], qg_ref[...], cos_ref[...], sneg_ref[...], spos_ref[...]) * (SCALE * LOG2E)
        qe_ref[:, sl] = jnp.where(lo, qr, 0.0).astype(BF16)
        qo_ref[:, sl] = jnp.where(lo, 0.0, qr).astype(BF16)


def _attend_rows(qe_ref, qo_ref, att_ref, r0, nr, kh, kwin, vwin, valid, sink):
    pairs = GROUP // 2
    cols = [slice((pairs * kh + j) * LANES, (pairs * kh + j + 1) * LANES) for j in range(pairs)]
    qcat = jnp.concatenate([qe_ref[r0:r0 + nr, c] for c in cols] + [qo_ref[r0:r0 + nr, c] for c in cols], axis=0)
    o = _attend(qcat, kwin, vwin, valid, sink)
    lo = _half_mask((nr, LANES))
    for j, c in enumerate(cols):
        att_ref[r0:r0 + nr, c] = jnp.where(lo, o[j * nr:(j + 1) * nr], o[(pairs + j) * nr:(pairs + j + 1) * nr]).astype(BF16)


def _attn_prompt_body(x_ref, g_ref, wqt_ref, qg_ref, cos_ref, sin_ref,
                      kprev_ref, kcur_ref, vprev_ref, vcur_ref, sink_ref, wot_ref,
                      o_ref, qt_ref, att_ref, pt_ref, kw_ref, vw_ref):
    i = pl.program_id(0)
    tm, d = x_ref.shape
    half = ROT_DIM // 2
    x = x_ref[...]
    xn = _rms(x, g_ref[...]).astype(BF16)
    qt = lax.dot_general(wqt_ref[...], xn, (((1,), (1,)), ((), ())), preferred_element_type=F32)
    cos, sin = cos_ref[...], sin_ref[...]
    gain = jnp.concatenate([qg_ref[...]] * (tm // LANES), axis=1)
    for h in range(d // HEAD_DIM):
        t = qt[h * HEAD_DIM:(h + 1) * HEAD_DIM, :]
        tn = t * lax.rsqrt(jnp.sum(t * t, axis=0, keepdims=True) / HEAD_DIM + EPS) * gain
        x1, x2 = tn[:half], tn[half:ROT_DIM]
        qt_ref[h * HEAD_DIM:(h + 1) * HEAD_DIM, :] = jnp.concatenate(
            [x1 * cos - x2 * sin, x2 * cos + x1 * sin, tn[ROT_DIM:]], axis=0).astype(BF16)

    kw_ref[:WINDOW, :] = kprev_ref[...]
    kw_ref[WINDOW:, :] = kcur_ref[...]
    vw_ref[:, :WINDOW] = vprev_ref[...]
    vw_ref[:, WINDOW:] = vcur_ref[...]

    first_query_chunk = lax.broadcasted_iota(jnp.int32, (CHUNK, LANES), 1) < CHUNK
    has_past = jnp.broadcast_to(i > 0, (CHUNK, LANES))
    zeros = jnp.zeros((HEAD_DIM, LANES), BF16)
    n = 0
    for p in range(tm // LANES):
        qs = slice(p * LANES, (p + 1) * LANES)
        masks = [first_query_chunk & has_past if p == 0 else first_query_chunk, has_past if p == 0 else None,
                 None, ~first_query_chunk]
        for kh in range(N_KV_HEADS):
            heads = range(GROUP * kh, GROUP * (kh + 1))
            kwin = kw_ref[p * LANES:p * LANES + KEYS, (kh // 2) * LANES:(kh // 2 + 1) * LANES]
            rhs = jnp.concatenate(
                [jnp.concatenate([qt_ref[h * HEAD_DIM:(h + 1) * HEAD_DIM, qs], zeros] if kh % 2 == 0 else
                                 [zeros, qt_ref[h * HEAD_DIM:(h + 1) * HEAD_DIM, qs]], axis=0)
                 for h in heads], axis=1)
            st = jnp.dot(kwin, rhs, preferred_element_type=F32)
            slot = n % 2
            n += 1
            dens = []
            for j, h in enumerate(heads):
                s = jnp.concatenate(
                    [st[c * CHUNK:(c + 1) * CHUNK, j * LANES:(j + 1) * LANES] if mask is None else
                     jnp.where(mask, st[c * CHUNK:(c + 1) * CHUNK, j * LANES:(j + 1) * LANES], -jnp.inf)
                     for c, mask in enumerate(masks)], axis=0)
                sink = sink_ref[h:h + 1, :]
                m = jnp.maximum(jnp.max(s, axis=0, keepdims=True), sink)
                e = jnp.exp2(s - m)
                dens.append(jnp.sum(e, axis=0, keepdims=True) + jnp.exp2(sink - m))
                pt_ref[slot, :, j * LANES:(j + 1) * LANES] = e.astype(BF16)
            ot = jnp.dot(vw_ref[kh * HEAD_DIM:(kh + 1) * HEAD_DIM, p * LANES:p * LANES + KEYS], pt_ref[slot],
                         preferred_element_type=F32)
            ot = ot * (1.0 / jnp.concatenate(dens, axis=1))
            for j, h in enumerate(heads):
                att_ref[h * HEAD_DIM:(h + 1) * HEAD_DIM, qs] = ot[:, j * LANES:(j + 1) * LANES].astype(BF16)
    out_t = jnp.dot(wot_ref[...], att_ref[...], preferred_element_type=F32)
    o_ref[...] = x + out_t.T


def _attn_sample_body(x_ref, g_ref, wq_ref, qg_ref, cos_ref, sneg_ref, spos_ref,
                      kc_ref, knew_ref, vc_ref, vnew_ref, sink_ref, wo_ref,
                      o_ref, qe_ref, qo_ref, att_ref, kw_ref, vw_ref, *, seg):
    tm = x_ref.shape[0]
    _project_q(x_ref, g_ref, wq_ref, qg_ref, cos_ref, sneg_ref, spos_ref, qe_ref, qo_ref)
    nkeys = WINDOW + seg
    valid = lax.broadcasted_iota(jnp.int32, (1, KEYS), 1) < nkeys
    kw_ref[nkeys:, :] = jnp.zeros((KEYS - nkeys, LANES), BF16)
    vw_ref[nkeys:, :] = jnp.zeros((KEYS - nkeys, LANES), BF16)
    for b in range(tm // seg):
        for kh in range(N_KV_HEADS):
            kw_ref[:WINDOW, :] = kc_ref[b, kh]
            kw_ref[WINDOW:nkeys, :] = knew_ref[kh, b * seg:(b + 1) * seg, :]
            vw_ref[:WINDOW, :] = vc_ref[b, kh]
            vw_ref[WINDOW:nkeys, :] = vnew_ref[kh, b * seg:(b + 1) * seg, :]
            _attend_rows(qe_ref, qo_ref, att_ref, b * seg, seg, kh, kw_ref[...], vw_ref[...], valid, sink_ref[kh])
    o_ref[...] = x_ref[...] + jnp.dot(att_ref[...], wo_ref[...], preferred_element_type=F32)


def _attn_mixer_sample(x, g, w_q, qg, rope, k2, v2, cache, sink_col, w_o, layer, blayer, *, tm, seg):
    t, d = x.shape
    tm = min(tm, t)
    nb = tm // seg
    row = lambda i: (i, 0)
    const = lambda i: (0, 0)
    resident = dict(pipeline_mode=pl.Buffered(1))
    cur = pl.BlockSpec((N_KV_HEADS, tm, LANES), lambda i: (0, i, 0))
    cspec = pl.BlockSpec((nb, N_KV_HEADS, WINDOW, LANES), lambda i: (i, 0, 0, 0))
    in_specs = [
        pl.BlockSpec((tm, d), row),
        pl.BlockSpec((None, 1, d), lambda i: (layer, 0, 0)),
        pl.BlockSpec((None, d, d), lambda i: (blayer, 0, 0), **resident),
        pl.BlockSpec((1, LANES), const),
        pl.BlockSpec((tm, LANES), row),
        pl.BlockSpec((tm, LANES), row),
        pl.BlockSpec((tm, LANES), row),
        cspec, cur, cspec, cur,
        pl.BlockSpec((N_KV_HEADS, GROUP * seg, 1), lambda i: (0, 0, 0)),
        pl.BlockSpec((None, d, d), lambda i: (blayer, 0, 0), **resident),
    ]
    nbytes = 2 * d * d * 2 + 4 * tm * d * 4 + tm * d * (4 + 3 * 2) + 8 * tm * LANES * 4 * 2 + 4 * 2 ** 20
    return pl.pallas_call(
        functools.partial(_attn_sample_body, seg=seg),
        grid=(t // tm,),
        in_specs=in_specs,
        out_specs=pl.BlockSpec((tm, d), row),
        out_shape=jax.ShapeDtypeStruct((t, d), F32),
        scratch_shapes=[pltpu.VMEM((tm, d), BF16), pltpu.VMEM((tm, d), BF16), pltpu.VMEM((tm, d), BF16),
                        pltpu.VMEM((KEYS, LANES), BF16), pltpu.VMEM((KEYS, LANES), BF16)],
        compiler_params=pltpu.CompilerParams(
            dimension_semantics=("arbitrary",), vmem_limit_bytes=_vmem_limit(nbytes)),
        name="attn_mixer_sample",
    )(x, g, w_q, qg, *rope, cache[0], k2, cache[1], v2, sink_col, w_o)


def _attn_mixer_prompt(x, g, w_qt, qg, cos_t, sin_t, kb, vt, sink_rows, w_ot, layer, blayer, *, tm):
    t, d = x.shape
    tm = min(tm, t)
    nkv = N_KV_HEADS * HEAD_DIM
    nh = d // HEAD_DIM
    half = ROT_DIM // 2
    prev_blk = lambda i: jnp.maximum(i * (tm // WINDOW) - 1, 0)
    resident = dict(pipeline_mode=pl.Buffered(1))
    in_specs = [
        pl.BlockSpec((tm, d), lambda i: (i, 0)),
        pl.BlockSpec((None, 1, d), lambda i: (layer, 0, 0)),
        pl.BlockSpec((None, d, d), lambda i: (blayer, 0, 0), **resident),
        pl.BlockSpec((HEAD_DIM, LANES), lambda i: (0, 0)),
        pl.BlockSpec((half, tm), lambda i: (0, i)),
        pl.BlockSpec((half, tm), lambda i: (0, i)),
        pl.BlockSpec((WINDOW, nkv), lambda i: (prev_blk(i), 0)),
        pl.BlockSpec((tm, nkv), lambda i: (i, 0)),
        pl.BlockSpec((nkv, WINDOW), lambda i: (0, prev_blk(i))),
        pl.BlockSpec((nkv, tm), lambda i: (0, i)),
        pl.BlockSpec((nh, LANES), lambda i: (0, 0)),
        pl.BlockSpec((None, d, d), lambda i: (blayer, 0, 0), **resident),
    ]
    nbytes = (2 * d * d * 2 + 4 * tm * d * 4 + 3 * tm * d * 4 + 2 * tm * d * 2 + 2 * KEYS * GROUP * LANES * 2
              + KEYS * GROUP * LANES * 4 * 2 + 4 * (WINDOW + tm) * nkv * 2)
    return pl.pallas_call(
        _attn_prompt_body,
        grid=(t // tm,),
        in_specs=in_specs,
        out_specs=pl.BlockSpec((tm, d), lambda i: (i, 0)),
        out_shape=jax.ShapeDtypeStruct((t, d), F32),
        scratch_shapes=[pltpu.VMEM((d, tm), BF16), pltpu.VMEM((d, tm), BF16),
                        pltpu.VMEM((2, KEYS, GROUP * LANES), BF16),
                        pltpu.VMEM((WINDOW + tm, nkv), BF16), pltpu.VMEM((nkv, WINDOW + tm), BF16)],
        compiler_params=pltpu.CompilerParams(
            dimension_semantics=("arbitrary",), vmem_limit_bytes=_vmem_limit(nbytes)),
        name="attn_mixer_prompt",
    )(x, g, w_qt, qg, cos_t, sin_t, kb, kb, vt, vt, sink_rows, w_ot)


def _rope_tables(pos):
    half = ROT_DIM // 2
    inv = ROPE_THETA ** (-jnp.arange(half, dtype=F32) / half)
    ang = pos.astype(F32)[:, None] * inv[None, :]
    cos, sin = jnp.cos(ang), jnp.sin(ang)
    n = pos.shape[0]
    one = jnp.ones((n, HEAD_DIM - ROT_DIM), F32)
    zero = jnp.zeros((n, HEAD_DIM - ROT_DIM), F32)
    zh = jnp.zeros((n, half), F32)
    c = jnp.concatenate([cos, cos, one], axis=1)
    sneg = jnp.concatenate([-sin, zh, zero], axis=1)
    spos = jnp.concatenate([zh, sin, zero], axis=1)
    return tuple(jnp.tile(a, (1, LANES // HEAD_DIM)) for a in (c, sneg, spos))


def _sink_column(sinks_l, rows_per_head):
    s = (sinks_l.astype(F32) * LOG2E).reshape(N_KV_HEADS, GROUP // 2, 2).transpose(0, 2, 1)
    return jnp.repeat(s.reshape(N_KV_HEADS, GROUP), rows_per_head, axis=1)[..., None]


def _dup_heads(t):
    t = t.transpose(0, 2, 1, 3)
    return jnp.concatenate([t, t], axis=-1).astype(BF16)


def _forward(x_prompt, x_sample, state_conv, cache_k, cache_v, mix_norm_g, mlp_norm_g, w_up, w_down,
             conv_w_in, conv_w, conv_w_out, kv_norm_g, w_kv, k_norm_g, w_q, q_norm_g, sinks, w_o,
             *, tm_mlp, tf, tf_cast, tm_conv, tn, tm_attn, tm_attn_s, tm_kv):
    _, s, d = x_prompt.shape
    b, l, _ = x_sample.shape
    n_a = conv_w_in.shape[0]
    depth = w_up.shape[0]
    xp = x_prompt.reshape(s, d)
    xs = x_sample.reshape(b * l, d)

    w_in_b, w_out_b = conv_w_in.astype(BF16), conv_w_out.astype(BF16)
    w_kv_b, w_q_b, w_o_b = w_kv.astype(BF16), w_q.astype(BF16), w_o.astype(BF16)
    w_qt_b, w_ot_b = w_q_b.transpose(0, 2, 1), w_o_b.transpose(0, 2, 1)
    mix_g = mix_norm_g.reshape(depth, 1, d)
    mlp_g = mlp_norm_g.reshape(depth, 1, d)

    half = ROT_DIM // 2
    ang_t = (ROPE_THETA ** (-jnp.arange(half, dtype=F32) / half))[:, None] * jnp.arange(s).astype(F32)[None, :]
    cos_t, sin_t = jnp.cos(ang_t), jnp.sin(ang_t)
    rope_p = _rope_tables(jnp.arange(s))
    rope_s = _rope_tables(jnp.tile(PAST_LEN + jnp.arange(l), b))
    kg = jnp.tile(k_norm_g.astype(F32), LANES // HEAD_DIM).reshape(1, LANES)

    conv_p, conv_s = [], []
    for i in range(depth):
        if i < n_a:
            xp, cp = _conv_mixer(xp, mix_g, w_in_b, conv_w, w_out_b, None, i, tm=tm_conv, tn=tn, seg=None)
            xs, cs = _conv_mixer(xs, mix_g, w_in_b, conv_w, w_out_b, state_conv, i, tm=tm_conv, tn=tn, seg=l)
            conv_p.append(cp)
            conv_s.append(cs)
        else:
            if i == n_a:
                kp, vp, kbp, vtp = _shared_kv(xp, kv_norm_g.reshape(1, d), w_kv_b, kg, rope_p, tm=tm_kv, dup=False)
                ks, vs, k2s, v2s = _shared_kv(xs, kv_norm_g.reshape(1, d), w_kv_b, kg, rope_s, tm=tm_kv, dup=True)
                cache2 = (_dup_heads(cache_k), _dup_heads(cache_v))
            j = i - n_a
            qg = jnp.tile(q_norm_g[j].astype(F32), LANES // HEAD_DIM).reshape(1, LANES)
            qg_t = jnp.broadcast_to((q_norm_g[j].astype(F32) * (SCALE * LOG2E))[:, None], (HEAD_DIM, LANES))
            sink_rows = jnp.broadcast_to((sinks[j].astype(F32) * LOG2E)[:, None], (sinks.shape[1], LANES))
            xp = _attn_mixer_prompt(xp, mix_g, w_qt_b, qg_t, cos_t, sin_t, kbp, vtp, sink_rows, w_ot_b, i, j, tm=tm_attn)
            xs = _attn_mixer_sample(xs, mix_g, w_q_b, qg, rope_s, k2s, v2s, cache2, _sink_column(sinks[j], l),
                                    w_o_b, i, j, tm=tm_attn_s, seg=l)
        xs, w_up_b, w_down_b = _mlp_cast(xs, mlp_g, w_up, w_down, i, tf=tf_cast)
        xp = _mlp(xp, mlp_g, w_up_b, w_down_b, i, tm=tm_mlp, tf=tf)

    hd = (N_KV_HEADS, HEAD_DIM)
    ks_new = ks.reshape(b, l, *hd)
    vs_new = vs.reshape(b, l, *hd)
    return (xp.reshape(1, s, d), xs.reshape(b, l, d), jnp.stack(conv_p), jnp.stack(conv_s),
            kp[s - WINDOW:].reshape(1, WINDOW, *hd), vp[s - WINDOW:].reshape(1, WINDOW, *hd),
            jnp.concatenate([cache_k[:, l:], ks_new], axis=1), jnp.concatenate([cache_v[:, l:], vs_new], axis=1))


def kernel(x_prompt, x_sample, state_conv, cache_k, cache_v, mix_norm_g, mlp_norm_g, w_up, w_down, conv_w_in, conv_w, conv_w_out, kv_norm_g, w_kv, k_norm_g, w_q, q_norm_g, sinks, w_o):
    return _forward(x_prompt, x_sample, state_conv, cache_k, cache_v, mix_norm_g, mlp_norm_g, w_up, w_down,
                    conv_w_in, conv_w, conv_w_out, kv_norm_g, w_kv, k_norm_g, w_q, q_norm_g, sinks, w_o,
                    tm_mlp=512, tf=2048, tf_cast=512, tm_conv=512, tn=512, tm_attn=512, tm_attn_s=256, tm_kv=512)
```

```python
import functools

import jax
import jax.numpy as jnp
from jax import lax
from jax.experimental import pallas as pl
from jax.experimental.pallas import tpu as pltpu

EPS = 1e-6
CHUNK = 64
WINDOW = 128
HEAD_DIM = 64
N_KV_HEADS = 4
GROUP = 8
ROT_DIM = 16
ROPE_THETA = 500000.0
PAST_LEN = 2048
SCALE = HEAD_DIM ** -0.5
LOG2E = 1.4426950408889634

LANES = 128
KEYS = 2 * WINDOW
VMEM_LIMIT_CAP = 56 * 2 ** 20

F32 = jnp.float32
BF16 = jnp.bfloat16


def _vmem_limit(nbytes):
    return int(min(VMEM_LIMIT_CAP, max(32 * 2 ** 20, nbytes * 5 // 4 + 4 * 2 ** 20)))


def _rms(x, g):
    return x * lax.rsqrt(jnp.mean(x * x, axis=-1, keepdims=True) + EPS) * g


def _half_mask(shape):
    return lax.broadcasted_iota(jnp.int32, shape, len(shape) - 1) < HEAD_DIM


def _head_norm_rope(t, gain, cos, sneg, spos):
    lo = _half_mask(t.shape)
    sq = t * t
    s_lo = jnp.sum(jnp.where(lo, sq, 0.0), axis=-1, keepdims=True)
    s_hi = jnp.sum(jnp.where(lo, 0.0, sq), axis=-1, keepdims=True)
    inv = jnp.where(lo, lax.rsqrt(s_lo / HEAD_DIM + EPS), lax.rsqrt(s_hi / HEAD_DIM + EPS))
    tn = t * inv * gain
    half = ROT_DIM // 2
    return tn * cos + pltpu.roll(tn, LANES - half, 1) * sneg + pltpu.roll(tn, half, 1) * spos


def _mlp_body(x_ref, g_ref, wu_ref, wd_ref, o_ref, xn_ref):
    @pl.when(pl.program_id(1) == 0)
    def _():
        x = x_ref[...]
        xn_ref[...] = _rms(x, g_ref[...]).astype(BF16)
        o_ref[...] = x

    h = jnp.dot(xn_ref[...], wu_ref[...], preferred_element_type=F32)
    h = jnp.square(jnp.maximum(h, 0.0)).astype(BF16)
    o_ref[...] += jnp.dot(h, wd_ref[...], preferred_element_type=F32)


def _mlp_cast_body(x_ref, g_ref, wu_ref, wd_ref, o_ref, wub_ref, wdb_ref, xn_ref):
    @pl.when(pl.program_id(0) == 0)
    def _():
        x = x_ref[...]
        xn_ref[...] = _rms(x, g_ref[...]).astype(BF16)
        o_ref[...] = x

    wu = wu_ref[...].astype(BF16)
    wd = wd_ref[...].astype(BF16)
    wub_ref[...] = wu
    wdb_ref[...] = wd
    h = jnp.dot(xn_ref[...], wu, preferred_element_type=F32)
    h = jnp.square(jnp.maximum(h, 0.0)).astype(BF16)
    o_ref[...] += jnp.dot(h, wd, preferred_element_type=F32)


def _mlp_cast(x, g, w_up, w_down, layer, *, tf):
    t, d = x.shape
    f = w_up.shape[2]
    tf = min(tf, f)
    nbytes = 2 * t * d * 4 + t * d * 2 + 2 * 2 * d * tf * (4 + 2) + t * tf * 6
    return pl.pallas_call(
        _mlp_cast_body,
        grid=(f // tf,),
        in_specs=[
            pl.BlockSpec((t, d), lambda j: (0, 0), pipeline_mode=pl.Buffered(1)),
            pl.BlockSpec((None, 1, d), lambda j: (layer, 0, 0)),
            pl.BlockSpec((None, d, tf), lambda j: (layer, 0, j)),
            pl.BlockSpec((None, tf, d), lambda j: (layer, j, 0)),
        ],
        out_specs=[
            pl.BlockSpec((t, d), lambda j: (0, 0), pipeline_mode=pl.Buffered(1)),
            pl.BlockSpec((d, tf), lambda j: (0, j)),
            pl.BlockSpec((tf, d), lambda j: (j, 0)),
        ],
        out_shape=[
            jax.ShapeDtypeStruct((t, d), F32),
            jax.ShapeDtypeStruct((d, f), BF16),
            jax.ShapeDtypeStruct((f, d), BF16),
        ],
        scratch_shapes=[pltpu.VMEM((t, d), BF16)],
        compiler_params=pltpu.CompilerParams(
            dimension_semantics=("arbitrary",), vmem_limit_bytes=_vmem_limit(nbytes)),
        name="mlp_cast",
    )(x, g, w_up, w_down)


def _mlp(x, g, w_up, w_down, layer, *, tm, tf):
    t, d = x.shape
    f = w_up.shape[1]
    tm, tf = min(tm, t), min(tf, f)
    nbytes = 2 * (2 * tm * d * 4 + 2 * d * tf * 2) + tm * d * 2 + tm * tf * 6
    return pl.pallas_call(
        _mlp_body,
        grid=(t // tm, f // tf),
        in_specs=[
            pl.BlockSpec((tm, d), lambda i, j: (i, 0)),
            pl.BlockSpec((None, 1, d), lambda i, j: (layer, 0, 0)),
            pl.BlockSpec((d, tf), lambda i, j: (0, j)),
            pl.BlockSpec((tf, d), lambda i, j: (j, 0)),
        ],
        out_specs=pl.BlockSpec((tm, d), lambda i, j: (i, 0)),
        out_shape=jax.ShapeDtypeStruct((t, d), F32),
        scratch_shapes=[pltpu.VMEM((tm, d), BF16)],
        compiler_params=pltpu.CompilerParams(
            dimension_semantics=("arbitrary", "arbitrary"), vmem_limit_bytes=_vmem_limit(nbytes)),
        name="mlp",
    )(x, g, w_up, w_down)


def _conv_body(*refs, seg, tn):
    if seg is None:
        x_ref, g_ref, wb_ref, wc_ref, wu_ref, cw_ref, wout_ref, o_ref, zl_ref, xn_ref = refs
    else:
        x_ref, g_ref, wb_ref, wc_ref, wu_ref, cw_ref, wout_ref, st_ref, o_ref, zl_ref, xn_ref = refs
    i, j = pl.program_id(0), pl.program_id(1)

    @pl.when(j == 0)
    def _():
        x = x_ref[...]
        xn_ref[...] = _rms(x, g_ref[...]).astype(BF16)
        o_ref[...] = x

    xn = xn_ref[...]
    gate_b = jnp.dot(xn, wb_ref[...], preferred_element_type=F32)
    z = jnp.dot(xn, wc_ref[...], preferred_element_type=F32) * jnp.dot(xn, wu_ref[...], preferred_element_type=F32)
    tm = z.shape[0]
    row = lax.broadcasted_iota(jnp.int32, z.shape, 0)
    r1 = pltpu.roll(z, 1, 0)
    r2 = pltpu.roll(z, 2, 0)
    if seg is None:
        @pl.when(i == 0)
        def _():
            zl_ref[j] = jnp.zeros((2, tn), F32)

        prev = zl_ref[j]
        p0, p1 = prev[0:1, :], prev[1:2, :]
        zl_ref[j] = z[tm - 2:, :]
    else:
        nb = tm // seg
        st = st_ref[...]
        p0 = jnp.broadcast_to(st[:, 0:1, :], (nb, seg, tn)).reshape(tm, tn)
        p1 = jnp.broadcast_to(st[:, 1:2, :], (nb, seg, tn)).reshape(tm, tn)
        row = row % seg
        zl_ref[...] = z.reshape(nb, seg, tn)[:, seg - 2:, :]
    zp1 = jnp.where(row == 0, p1, r1)
    zp2 = jnp.where(row == 0, p0, jnp.where(row == 1, p1, r2))
    cw = cw_ref[...]
    conv = zp2 * cw[0:1, :] + zp1 * cw[1:2, :] + z * cw[2:3, :]
    y = (gate_b * conv).astype(BF16)
    o_ref[...] += jnp.dot(y, wout_ref[...], preferred_element_type=F32)


def _conv_mixer(x, g, w_in, cw, w_out, state, layer, *, tm, tn, seg):
    t, d = x.shape
    tm = min(tm, t)
    nj = d // tn
    in_specs = [
        pl.BlockSpec((tm, d), lambda i, j: (i, 0)),
        pl.BlockSpec((None, 1, d), lambda i, j: (layer, 0, 0)),
        pl.BlockSpec((None, d, tn), lambda i, j: (layer, 0, j)),
        pl.BlockSpec((None, d, tn), lambda i, j: (layer, 0, nj + j)),
        pl.BlockSpec((None, d, tn), lambda i, j: (layer, 0, 2 * nj + j)),
        pl.BlockSpec((None, 3, tn), lambda i, j: (layer, 0, j)),
        pl.BlockSpec((None, tn, d), lambda i, j: (layer, j, 0)),
    ]
    args = [x, g, w_in, w_in, w_in, cw, w_out]
    if seg is None:
        zl_shape = (nj, 2, tn)
        zl_spec = pl.BlockSpec((nj, 2, tn), lambda i, j: (0, 0, 0))
    else:
        nb = tm // seg
        zl_shape = (t // seg, 2, d)
        zl_spec = pl.BlockSpec((nb, 2, tn), lambda i, j: (i, 0, j))
        in_specs.append(pl.BlockSpec((None, nb, 2, tn), lambda i, j: (layer, i, 0, j)))
        args.append(state)
    nbytes = 2 * (2 * tm * d * 4 + d * 3 * tn * 2 + tn * d * 2) + tm * d * 2 + tm * tn * 4 * 8
    y, zl = pl.pallas_call(
        functools.partial(_conv_body, seg=seg, tn=tn),
        grid=(t // tm, nj),
        in_specs=in_specs,
        out_specs=[pl.BlockSpec((tm, d), lambda i, j: (i, 0)), zl_spec],
        out_shape=[jax.ShapeDtypeStruct((t, d), F32), jax.ShapeDtypeStruct(zl_shape, F32)],
        scratch_shapes=[pltpu.VMEM((tm, d), BF16)],
        compiler_params=pltpu.CompilerParams(
            dimension_semantics=("arbitrary", "arbitrary"), vmem_limit_bytes=_vmem_limit(nbytes)),
        name="conv_mixer",
    )(*args)
    if seg is None:
        zl = zl.transpose(1, 0, 2).reshape(1, 2, d)
    return y, zl


def _kv_body(x_ref, g_ref, wkv_ref, kg_ref, cos_ref, sneg_ref, spos_ref, k_ref, v_ref, ka_ref, va_ref, *, dup):
    xn = _rms(x_ref[...], g_ref[...]).astype(BF16)
    kv = jnp.dot(xn, wkv_ref[...], preferred_element_type=F32)
    nkv = N_KV_HEADS * HEAD_DIM
    lo = _half_mask((xn.shape[0], LANES))
    for p in range(nkv // LANES):
        sl = slice(p * LANES, (p + 1) * LANES)
        kr = _head_norm_rope(kv[:, sl], kg_ref[...], cos_ref[...], sneg_ref[...], spos_ref[...])
        vr = kv[:, nkv + p * LANES: nkv + (p + 1) * LANES]
        k_ref[:, sl] = kr
        v_ref[:, sl] = vr
        if dup:
            for src, dst in ((kr, ka_ref), (vr, va_ref)):
                sw = pltpu.roll(src, HEAD_DIM, 1)
                dst[2 * p] = jnp.where(lo, src, sw).astype(BF16)
                dst[2 * p + 1] = jnp.where(lo, sw, src).astype(BF16)
        else:
            ka_ref[:, sl] = kr.astype(BF16)
            va_ref[sl, :] = vr.T.astype(BF16)


def _shared_kv(x, g, w_kv, kg, rope, *, tm, dup):
    t, d = x.shape
    tm = min(tm, t)
    nkv = N_KV_HEADS * HEAD_DIM
    row = lambda i: (i, 0)
    const = lambda i: (0, 0)
    if dup:
        aux_specs = [pl.BlockSpec((N_KV_HEADS, tm, LANES), lambda i: (0, i, 0))] * 2
        aux_shapes = [jax.ShapeDtypeStruct((N_KV_HEADS, t, LANES), BF16)] * 2
    else:
        aux_specs = [pl.BlockSpec((tm, nkv), row), pl.BlockSpec((nkv, tm), lambda i: (0, i))]
        aux_shapes = [jax.ShapeDtypeStruct((t, nkv), BF16), jax.ShapeDtypeStruct((nkv, t), BF16)]
    nbytes = 2 * (tm * d * 4 + d * 2 * nkv * 2 + 5 * tm * nkv * 4) + tm * d * 8
    return pl.pallas_call(
        functools.partial(_kv_body, dup=dup),
        grid=(t // tm,),
        in_specs=[
            pl.BlockSpec((tm, d), row),
            pl.BlockSpec((1, d), const),
            pl.BlockSpec((d, 2 * nkv), const),
            pl.BlockSpec((1, LANES), const),
            pl.BlockSpec((tm, LANES), row),
            pl.BlockSpec((tm, LANES), row),
            pl.BlockSpec((tm, LANES), row),
        ],
        out_specs=[pl.BlockSpec((tm, nkv), row), pl.BlockSpec((tm, nkv), row)] + aux_specs,
        out_shape=[jax.ShapeDtypeStruct((t, nkv), F32), jax.ShapeDtypeStruct((t, nkv), F32)] + aux_shapes,
        compiler_params=pltpu.CompilerParams(
            dimension_semantics=("arbitrary",), vmem_limit_bytes=_vmem_limit(nbytes)),
        name="shared_kv",
    )(x, g, w_kv, kg, *rope)


def _attend(qcat, kwin, vwin, valid, sink):
    s = lax.dot_general(qcat, kwin, (((1,), (1,)), ((), ())), preferred_element_type=F32)
    s = jnp.where(valid, s, -jnp.inf)
    m = jnp.maximum(jnp.max(s, axis=-1, keepdims=True), sink)
    e = jnp.exp2(s - m)
    den = jnp.sum(e, axis=-1, keepdims=True) + jnp.exp2(sink - m)
    o = jnp.dot(e.astype(BF16), vwin, preferred_element_type=F32)
    return o / den


def _project_q(x_ref, g_ref, wq_ref, qg_ref, cos_ref, sneg_ref, spos_ref, qe_ref, qo_ref):
    xn = _rms(x_ref[...], g_ref[...]).astype(BF16)
    q = jnp.dot(xn, wq_ref[...], preferred_element_type=F32)
    lo = _half_mask((q.shape[0], LANES))
    for p in range(q.shape[1] // LANES):
        sl = slice(p * LANES, (p + 1) * LANES)
        qr = _head_norm_rope(q[:, sl], qg_ref[...], cos_ref[...], sneg_ref[...], spos_ref[...]) * (SCALE * LOG2E)
        qe_ref[:, sl] = jnp.where(lo, qr, 0.0).astype(BF16)
        qo_ref[:, sl] = jnp.where(lo, 0.0, qr).astype(BF16)


def _attend_rows(qe_ref, qo_ref, att_ref, r0, nr, kh, kwin, vwin, valid, sink):
    pairs = GROUP // 2
    cols = [slice((pairs * kh + j) * LANES, (pairs * kh + j + 1) * LANES) for j in range(pairs)]
    qcat = jnp.concatenate([qe_ref[r0:r0 + nr, c] for c in cols] + [qo_ref[r0:r0 + nr, c] for c in cols], axis=0)
    o = _attend(qcat, kwin, vwin, valid, sink)
    lo = _half_mask((nr, LANES))
    for j, c in enumerate(cols):
        att_ref[r0:r0 + nr, c] = jnp.where(lo, o[j * nr:(j + 1) * nr], o[(pairs + j) * nr:(pairs + j + 1) * nr]).astype(BF16)


def _attn_prompt_body(x_ref, g_ref, wqt_ref, qg_ref, cos_ref, sin_ref,
                      kprev_ref, kcur_ref, vprev_ref, vcur_ref, sink_ref, wot_ref,
                      o_ref, qt_ref, att_ref, pt_ref, kw_ref, vw_ref):
    i = pl.program_id(0)
    tm, d = x_ref.shape
    half = ROT_DIM // 2
    x = x_ref[...]
    xn = _rms(x, g_ref[...]).astype(BF16)
    cos, sin = cos_ref[...], sin_ref[...]
    gain = jnp.concatenate([qg_ref[...]] * (tm // LANES), axis=1)
    rows = GROUP * HEAD_DIM

    def project_q(kh):
        return lax.dot_general(wqt_ref[kh * rows:(kh + 1) * rows, :], xn, (((1,), (1,)), ((), ())),
                               preferred_element_type=F32)

    def norm_rope_q(qt, kh):
        for j in range(GROUP):
            t = qt[j * HEAD_DIM:(j + 1) * HEAD_DIM, :]
            tn = t * lax.rsqrt(jnp.sum(t * t, axis=0, keepdims=True) / HEAD_DIM + EPS) * gain
            x1, x2 = tn[:half], tn[half:ROT_DIM]
            h = GROUP * kh + j
            qt_ref[h * HEAD_DIM:(h + 1) * HEAD_DIM, :] = jnp.concatenate(
                [x1 * cos - x2 * sin, x2 * cos + x1 * sin, tn[ROT_DIM:]], axis=0).astype(BF16)

    def project_o(kh):
        return lax.dot_general(att_ref[kh * rows:(kh + 1) * rows, :], wot_ref[kh * rows:(kh + 1) * rows, :],
                               (((0,), (0,)), ((), ())), preferred_element_type=F32)

    kw_ref[:WINDOW, :] = kprev_ref[...]
    kw_ref[WINDOW:, :] = kcur_ref[...]
    vw_ref[:, :WINDOW] = vprev_ref[...]
    vw_ref[:, WINDOW:] = vcur_ref[...]

    first_query_chunk = lax.broadcasted_iota(jnp.int32, (CHUNK, LANES), 1) < CHUNK
    has_past = jnp.broadcast_to(i > 0, (CHUNK, LANES))
    zeros = jnp.zeros((HEAD_DIM, LANES), BF16)

    def scores_t(p, kh):
        qs = slice(p * LANES, (p + 1) * LANES)
        kwin = kw_ref[p * LANES:p * LANES + KEYS, (kh // 2) * LANES:(kh // 2 + 1) * LANES]
        rhs = jnp.concatenate(
            [jnp.concatenate([qt_ref[h * HEAD_DIM:(h + 1) * HEAD_DIM, qs], zeros] if kh % 2 == 0 else
                             [zeros, qt_ref[h * HEAD_DIM:(h + 1) * HEAD_DIM, qs]], axis=0)
             for h in range(GROUP * kh, GROUP * (kh + 1))], axis=1)
        return jnp.dot(kwin, rhs, preferred_element_type=F32)

    def softmax_t(st, p, kh, slot):
        masks = [first_query_chunk & has_past if p == 0 else first_query_chunk, has_past if p == 0 else None,
                 None, ~first_query_chunk]
        dens = []
        for j in range(GROUP):
            s = jnp.concatenate(
                [st[c * CHUNK:(c + 1) * CHUNK, j * LANES:(j + 1) * LANES] if mask is None else
                 jnp.where(mask, st[c * CHUNK:(c + 1) * CHUNK, j * LANES:(j + 1) * LANES], -jnp.inf)
                 for c, mask in enumerate(masks)], axis=0)
            sink = sink_ref[GROUP * kh + j:GROUP * kh + j + 1, :]
            m = jnp.maximum(jnp.max(s, axis=0, keepdims=True), sink)
            e = jnp.exp2(s - m)
            dens.append(jnp.sum(e, axis=0, keepdims=True) + jnp.exp2(sink - m))
            pt_ref[slot, :, j * LANES:(j + 1) * LANES] = e.astype(BF16)
        return jnp.concatenate(dens, axis=1)

    def weighted_values_t(p, kh, slot, den):
        qs = slice(p * LANES, (p + 1) * LANES)
        ot = jnp.dot(vw_ref[kh * HEAD_DIM:(kh + 1) * HEAD_DIM, p * LANES:p * LANES + KEYS], pt_ref[slot],
                     preferred_element_type=F32)
        ot = ot * (1.0 / den)
        for j in range(GROUP):
            h = GROUP * kh + j
            att_ref[h * HEAD_DIM:(h + 1) * HEAD_DIM, qs] = ot[:, j * LANES:(j + 1) * LANES].astype(BF16)

    def attend(kh):
        blocks = [(p, kh) for p in range(tm // LANES)]
        st = scores_t(*blocks[0])
        pending = None
        for n, blk in enumerate(blocks):
            st_next = scores_t(*blocks[n + 1]) if n + 1 < len(blocks) else None
            den = softmax_t(st, *blk, n % 2)
            if pending is not None:
                weighted_values_t(*pending)
            pending = (*blk, n % 2, den)
            st = st_next
        weighted_values_t(*pending)

    qts = {0: project_q(0)}
    qts[1] = project_q(1)
    norm_rope_q(qts.pop(0), 0)
    acc = x
    for kh in range(N_KV_HEADS):
        if kh + 2 < N_KV_HEADS:
            qts[kh + 2] = project_q(kh + 2)
        if kh >= 1:
            acc = acc + project_o(kh - 1)
        if kh + 1 < N_KV_HEADS:
            norm_rope_q(qts.pop(kh + 1), kh + 1)
        attend(kh)
    o_ref[...] = acc + project_o(N_KV_HEADS - 1)


def _attn_sample_body(x_ref, g_ref, wq_ref, qg_ref, cos_ref, sneg_ref, spos_ref,
                      kc_ref, knew_ref, vc_ref, vnew_ref, sink_ref, wo_ref,
                      o_ref, qe_ref, qo_ref, att_ref, kw_ref, vw_ref, *, seg):
    tm = x_ref.shape[0]
    _project_q(x_ref, g_ref, wq_ref, qg_ref, cos_ref, sneg_ref, spos_ref, qe_ref, qo_ref)
    nkeys = WINDOW + seg
    valid = lax.broadcasted_iota(jnp.int32, (1, KEYS), 1) < nkeys
    kw_ref[nkeys:, :] = jnp.zeros((KEYS - nkeys, LANES), BF16)
    vw_ref[nkeys:, :] = jnp.zeros((KEYS - nkeys, LANES), BF16)
    for b in range(tm // seg):
        for kh in range(N_KV_HEADS):
            kw_ref[:WINDOW, :] = kc_ref[b, kh]
            kw_ref[WINDOW:nkeys, :] = knew_ref[kh, b * seg:(b + 1) * seg, :]
            vw_ref[:WINDOW, :] = vc_ref[b, kh]
            vw_ref[WINDOW:nkeys, :] = vnew_ref[kh, b * seg:(b + 1) * seg, :]
            _attend_rows(qe_ref, qo_ref, att_ref, b * seg, seg, kh, kw_ref[...], vw_ref[...], valid, sink_ref[kh])
    o_ref[...] = x_ref[...] + jnp.dot(att_ref[...], wo_ref[...], preferred_element_type=F32)


def _attn_mixer_sample(x, g, w_q, qg, rope, k2, v2, cache, sink_col, w_o, layer, blayer, *, tm, seg):
    t, d = x.shape
    tm = min(tm, t)
    nb = tm // seg
    row = lambda i: (i, 0)
    const = lambda i: (0, 0)
    resident = dict(pipeline_mode=pl.Buffered(1))
    cur = pl.BlockSpec((N_KV_HEADS, tm, LANES), lambda i: (0, i, 0))
    cspec = pl.BlockSpec((nb, N_KV_HEADS, WINDOW, LANES), lambda i: (i, 0, 0, 0))
    in_specs = [
        pl.BlockSpec((tm, d), row),
        pl.BlockSpec((None, 1, d), lambda i: (layer, 0, 0)),
        pl.BlockSpec((None, d, d), lambda i: (blayer, 0, 0), **resident),
        pl.BlockSpec((1, LANES), const),
        pl.BlockSpec((tm, LANES), row),
        pl.BlockSpec((tm, LANES), row),
        pl.BlockSpec((tm, LANES), row),
        cspec, cur, cspec, cur,
        pl.BlockSpec((N_KV_HEADS, GROUP * seg, 1), lambda i: (0, 0, 0)),
        pl.BlockSpec((None, d, d), lambda i: (blayer, 0, 0), **resident),
    ]
    nbytes = 2 * d * d * 2 + 4 * tm * d * 4 + tm * d * (4 + 3 * 2) + 8 * tm * LANES * 4 * 2 + 4 * 2 ** 20
    return pl.pallas_call(
        functools.partial(_attn_sample_body, seg=seg),
        grid=(t // tm,),
        in_specs=in_specs,
        out_specs=pl.BlockSpec((tm, d), row),
        out_shape=jax.ShapeDtypeStruct((t, d), F32),
        scratch_shapes=[pltpu.VMEM((tm, d), BF16), pltpu.VMEM((tm, d), BF16), pltpu.VMEM((tm, d), BF16),
                        pltpu.VMEM((KEYS, LANES), BF16), pltpu.VMEM((KEYS, LANES), BF16)],
        compiler_params=pltpu.CompilerParams(
            dimension_semantics=("arbitrary",), vmem_limit_bytes=_vmem_limit(nbytes)),
        name="attn_mixer_sample",
    )(x, g, w_q, qg, *rope, cache[0], k2, cache[1], v2, sink_col, w_o)


def _attn_mixer_prompt(x, g, w_qt, qg, cos_t, sin_t, kb, vt, sink_rows, w_ot, layer, blayer, *, tm):
    t, d = x.shape
    tm = min(tm, t)
    nkv = N_KV_HEADS * HEAD_DIM
    nh = d // HEAD_DIM
    half = ROT_DIM // 2
    prev_blk = lambda i: jnp.maximum(i * (tm // WINDOW) - 1, 0)
    resident = dict(pipeline_mode=pl.Buffered(1))
    in_specs = [
        pl.BlockSpec((tm, d), lambda i: (i, 0)),
        pl.BlockSpec((None, 1, d), lambda i: (layer, 0, 0)),
        pl.BlockSpec((None, d, d), lambda i: (blayer, 0, 0), **resident),
        pl.BlockSpec((HEAD_DIM, LANES), lambda i: (0, 0)),
        pl.BlockSpec((half, tm), lambda i: (0, i)),
        pl.BlockSpec((half, tm), lambda i: (0, i)),
        pl.BlockSpec((WINDOW, nkv), lambda i: (prev_blk(i), 0)),
        pl.BlockSpec((tm, nkv), lambda i: (i, 0)),
        pl.BlockSpec((nkv, WINDOW), lambda i: (0, prev_blk(i))),
        pl.BlockSpec((nkv, tm), lambda i: (0, i)),
        pl.BlockSpec((nh, LANES), lambda i: (0, 0)),
        pl.BlockSpec((None, d, d), lambda i: (blayer, 0, 0), **resident),
    ]
    nbytes = (2 * d * d * 2 + 4 * tm * d * 4 + 3 * tm * d * 4 + 2 * tm * d * 2 + 2 * KEYS * GROUP * LANES * 2
              + KEYS * GROUP * LANES * 4 * 2 + 4 * (WINDOW + tm) * nkv * 2)
    return pl.pallas_call(
        _attn_prompt_body,
        grid=(t // tm,),
        in_specs=in_specs,
        out_specs=pl.BlockSpec((tm, d), lambda i: (i, 0)),
        out_shape=jax.ShapeDtypeStruct((t, d), F32),
        scratch_shapes=[pltpu.VMEM((d, tm), BF16), pltpu.VMEM((d, tm), BF16),
                        pltpu.VMEM((2, KEYS, GROUP * LANES), BF16),
                        pltpu.VMEM((WINDOW + tm, nkv), BF16), pltpu.VMEM((nkv, WINDOW + tm), BF16)],
        compiler_params=pltpu.CompilerParams(
            dimension_semantics=("arbitrary",), vmem_limit_bytes=_vmem_limit(nbytes)),
        name="attn_mixer_prompt",
    )(x, g, w_qt, qg, cos_t, sin_t, kb, kb, vt, vt, sink_rows, w_ot)


def _rope_tables(pos):
    half = ROT_DIM // 2
    inv = ROPE_THETA ** (-jnp.arange(half, dtype=F32) / half)
    ang = pos.astype(F32)[:, None] * inv[None, :]
    cos, sin = jnp.cos(ang), jnp.sin(ang)
    n = pos.shape[0]
    one = jnp.ones((n, HEAD_DIM - ROT_DIM), F32)
    zero = jnp.zeros((n, HEAD_DIM - ROT_DIM), F32)
    zh = jnp.zeros((n, half), F32)
    c = jnp.concatenate([cos, cos, one], axis=1)
    sneg = jnp.concatenate([-sin, zh, zero], axis=1)
    spos = jnp.concatenate([zh, sin, zero], axis=1)
    return tuple(jnp.tile(a, (1, LANES // HEAD_DIM)) for a in (c, sneg, spos))


def _sink_column(sinks_l, rows_per_head):
    s = (sinks_l.astype(F32) * LOG2E).reshape(N_KV_HEADS, GROUP // 2, 2).transpose(0, 2, 1)
    return jnp.repeat(s.reshape(N_KV_HEADS, GROUP), rows_per_head, axis=1)[..., None]


def _dup_heads(t):
    t = t.transpose(0, 2, 1, 3)
    return jnp.concatenate([t, t], axis=-1).astype(BF16)


def _forward(x_prompt, x_sample, state_conv, cache_k, cache_v, mix_norm_g, mlp_norm_g, w_up, w_down,
             conv_w_in, conv_w, conv_w_out, kv_norm_g, w_kv, k_norm_g, w_q, q_norm_g, sinks, w_o,
             *, tm_mlp, tf, tf_cast, tm_conv, tn, tm_attn, tm_attn_s, tm_kv):
    _, s, d = x_prompt.shape
    b, l, _ = x_sample.shape
    n_a = conv_w_in.shape[0]
    depth = w_up.shape[0]
    xp = x_prompt.reshape(s, d)
    xs = x_sample.reshape(b * l, d)

    w_in_b, w_out_b = conv_w_in.astype(BF16), conv_w_out.astype(BF16)
    w_kv_b, w_q_b, w_o_b = w_kv.astype(BF16), w_q.astype(BF16), w_o.astype(BF16)
    w_qt_b, w_ot_b = w_q_b.transpose(0, 2, 1), w_o_b.transpose(0, 2, 1)
    mix_g = mix_norm_g.reshape(depth, 1, d)
    mlp_g = mlp_norm_g.reshape(depth, 1, d)

    half = ROT_DIM // 2
    ang_t = (ROPE_THETA ** (-jnp.arange(half, dtype=F32) / half))[:, None] * jnp.arange(s).astype(F32)[None, :]
    cos_t, sin_t = jnp.cos(ang_t), jnp.sin(ang_t)
    rope_p = _rope_tables(jnp.arange(s))
    rope_s = _rope_tables(jnp.tile(PAST_LEN + jnp.arange(l), b))
    kg = jnp.tile(k_norm_g.astype(F32), LANES // HEAD_DIM).reshape(1, LANES)

    conv_p, conv_s = [], []
    for i in range(depth):
        if i < n_a:
            xp, cp = _conv_mixer(xp, mix_g, w_in_b, conv_w, w_out_b, None, i, tm=tm_conv, tn=tn, seg=None)
            xs, cs = _conv_mixer(xs, mix_g, w_in_b, conv_w, w_out_b, state_conv, i, tm=tm_conv, tn=tn, seg=l)
            conv_p.append(cp)
            conv_s.append(cs)
        else:
            if i == n_a:
                kp, vp, kbp, vtp = _shared_kv(xp, kv_norm_g.reshape(1, d), w_kv_b, kg, rope_p, tm=tm_kv, dup=False)
                ks, vs, k2s, v2s = _shared_kv(xs, kv_norm_g.reshape(1, d), w_kv_b, kg, rope_s, tm=tm_kv, dup=True)
                cache2 = (_dup_heads(cache_k), _dup_heads(cache_v))
            j = i - n_a
            qg = jnp.tile(q_norm_g[j].astype(F32), LANES // HEAD_DIM).reshape(1, LANES)
            qg_t = jnp.broadcast_to((q_norm_g[j].astype(F32) * (SCALE * LOG2E))[:, None], (HEAD_DIM, LANES))
            sink_rows = jnp.broadcast_to((sinks[j].astype(F32) * LOG2E)[:, None], (sinks.shape[1], LANES))
            xp = _attn_mixer_prompt(xp, mix_g, w_qt_b, qg_t, cos_t, sin_t, kbp, vtp, sink_rows, w_o_b, i, j, tm=tm_attn)
            xs = _attn_mixer_sample(xs, mix_g, w_q_b, qg, rope_s, k2s, v2s, cache2, _sink_column(sinks[j], l),
                                    w_o_b, i, j, tm=tm_attn_s, seg=l)
        xs, w_up_b, w_down_b = _mlp_cast(xs, mlp_g, w_up, w_down, i, tf=tf_cast)
        xp = _mlp(xp, mlp_g, w_up_b, w_down_b, i, tm=tm_mlp, tf=tf)

    hd = (N_KV_HEADS, HEAD_DIM)
    ks_new = ks.reshape(b, l, *hd)
    vs_new = vs.reshape(b, l, *hd)
    return (xp.reshape(1, s, d), xs.reshape(b, l, d), jnp.stack(conv_p), jnp.stack(conv_s),
            kp[s - WINDOW:].reshape(1, WINDOW, *hd), vp[s - WINDOW:].reshape(1, WINDOW, *hd),
            jnp.concatenate([cache_k[:, l:], ks_new], axis=1), jnp.concatenate([cache_v[:, l:], vs_new], axis=1))


def kernel(x_prompt, x_sample, state_conv, cache_k, cache_v, mix_norm_g, mlp_norm_g, w_up, w_down, conv_w_in, conv_w, conv_w_out, kv_norm_g, w_kv, k_norm_g, w_q, q_norm_g, sinks, w_o):
    return _forward(x_prompt, x_sample, state_conv, cache_k, cache_v, mix_norm_g, mlp_norm_g, w_up, w_down,
                    conv_w_in, conv_w, conv_w_out, kv_norm_g, w_kv, k_norm_g, w_q, q_norm_g, sinks, w_o,
                    tm_mlp=512, tf=2048, tf_cast=512, tm_conv=512, tn=512, tm_attn=512, tm_attn_s=256, tm_kv=512)
```

```python
import functools

import jax
import jax.numpy as jnp
from jax import lax
from jax.experimental import pallas as pl
from jax.experimental.pallas import tpu as pltpu

EPS = 1e-6
CHUNK = 64
WINDOW = 128
HEAD_DIM = 64
N_KV_HEADS = 4
GROUP = 8
ROT_DIM = 16
ROPE_THETA = 500000.0
PAST_LEN = 2048
SCALE = HEAD_DIM ** -0.5
LOG2E = 1.4426950408889634

LANES = 128
KEYS = 2 * WINDOW
VMEM_LIMIT_CAP = 56 * 2 ** 20

F32 = jnp.float32
BF16 = jnp.bfloat16


def _vmem_limit(nbytes):
    return int(min(VMEM_LIMIT_CAP, max(32 * 2 ** 20, nbytes * 5 // 4 + 4 * 2 ** 20)))


def _rms(x, g):
    return x * lax.rsqrt(jnp.mean(x * x, axis=-1, keepdims=True) + EPS) * g


def _half_mask(shape):
    return lax.broadcasted_iota(jnp.int32, shape, len(shape) - 1) < HEAD_DIM


def _head_norm_rope(t, gain, cos, sneg, spos):
    lo = _half_mask(t.shape)
    sq = t * t
    s_lo = jnp.sum(jnp.where(lo, sq, 0.0), axis=-1, keepdims=True)
    s_hi = jnp.sum(jnp.where(lo, 0.0, sq), axis=-1, keepdims=True)
    inv = jnp.where(lo, lax.rsqrt(s_lo / HEAD_DIM + EPS), lax.rsqrt(s_hi / HEAD_DIM + EPS))
    tn = t * inv * gain
    half = ROT_DIM // 2
    return tn * cos + pltpu.roll(tn, LANES - half, 1) * sneg + pltpu.roll(tn, half, 1) * spos


def _mlp_body(x_ref, g_ref, wu_ref, wd_ref, o_ref, xn_ref):
    @pl.when(pl.program_id(1) == 0)
    def _():
        x = x_ref[...]
        xn_ref[...] = _rms(x, g_ref[...]).astype(BF16)
        o_ref[...] = x

    h = jnp.dot(xn_ref[...], wu_ref[...], preferred_element_type=F32)
    h = jnp.square(jnp.maximum(h, 0.0)).astype(BF16)
    o_ref[...] += jnp.dot(h, wd_ref[...], preferred_element_type=F32)


def _mlp_cast_body(x_ref, g_ref, wu_ref, wd_ref, o_ref, wub_ref, wdb_ref, xn_ref):
    @pl.when(pl.program_id(0) == 0)
    def _():
        x = x_ref[...]
        xn_ref[...] = _rms(x, g_ref[...]).astype(BF16)
        o_ref[...] = x

    wu = wu_ref[...].astype(BF16)
    wd = wd_ref[...].astype(BF16)
    wub_ref[...] = wu
    wdb_ref[...] = wd
    h = jnp.dot(xn_ref[...], wu, preferred_element_type=F32)
    h = jnp.square(jnp.maximum(h, 0.0)).astype(BF16)
    o_ref[...] += jnp.dot(h, wd, preferred_element_type=F32)


def _mlp_cast(x, g, w_up, w_down, layer, *, tf):
    t, d = x.shape
    f = w_up.shape[2]
    tf = min(tf, f)
    nbytes = 2 * t * d * 4 + t * d * 2 + 2 * 2 * d * tf * (4 + 2) + t * tf * 6
    return pl.pallas_call(
        _mlp_cast_body,
        grid=(f // tf,),
        in_specs=[
            pl.BlockSpec((t, d), lambda j: (0, 0), pipeline_mode=pl.Buffered(1)),
            pl.BlockSpec((None, 1, d), lambda j: (layer, 0, 0)),
            pl.BlockSpec((None, d, tf), lambda j: (layer, 0, j)),
            pl.BlockSpec((None, tf, d), lambda j: (layer, j, 0)),
        ],
        out_specs=[
            pl.BlockSpec((t, d), lambda j: (0, 0), pipeline_mode=pl.Buffered(1)),
            pl.BlockSpec((d, tf), lambda j: (0, j)),
            pl.BlockSpec((tf, d), lambda j: (j, 0)),
        ],
        out_shape=[
            jax.ShapeDtypeStruct((t, d), F32),
            jax.ShapeDtypeStruct((d, f), BF16),
            jax.ShapeDtypeStruct((f, d), BF16),
        ],
        scratch_shapes=[pltpu.VMEM((t, d), BF16)],
        compiler_params=pltpu.CompilerParams(
            dimension_semantics=("arbitrary",), vmem_limit_bytes=_vmem_limit(nbytes)),
        name="mlp_cast",
    )(x, g, w_up, w_down)


def _mlp(x, g, w_up, w_down, layer, *, tm, tf):
    t, d = x.shape
    f = w_up.shape[1]
    tm, tf = min(tm, t), min(tf, f)
    nbytes = 2 * (2 * tm * d * 4 + 2 * d * tf * 2) + tm * d * 2 + tm * tf * 6
    return pl.pallas_call(
        _mlp_body,
        grid=(t // tm, f // tf),
        in_specs=[
            pl.BlockSpec((tm, d), lambda i, j: (i, 0)),
            pl.BlockSpec((None, 1, d), lambda i, j: (layer, 0, 0)),
            pl.BlockSpec((d, tf), lambda i, j: (0, j)),
            pl.BlockSpec((tf, d), lambda i, j: (j, 0)),
        ],
        out_specs=pl.BlockSpec((tm, d), lambda i, j: (i, 0)),
        out_shape=jax.ShapeDtypeStruct((t, d), F32),
        scratch_shapes=[pltpu.VMEM((tm, d), BF16)],
        compiler_params=pltpu.CompilerParams(
            dimension_semantics=("arbitrary", "arbitrary"), vmem_limit_bytes=_vmem_limit(nbytes)),
        name="mlp",
    )(x, g, w_up, w_down)


def _conv_body(*refs, seg, tn):
    if seg is None:
        x_ref, g_ref, wb_ref, wc_ref, wu_ref, cw_ref, wout_ref, o_ref, zl_ref, xn_ref = refs
    else:
        x_ref, g_ref, wb_ref, wc_ref, wu_ref, cw_ref, wout_ref, st_ref, o_ref, zl_ref, xn_ref = refs
    i, j = pl.program_id(0), pl.program_id(1)

    @pl.when(j == 0)
    def _():
        x = x_ref[...]
        xn_ref[...] = _rms(x, g_ref[...]).astype(BF16)
        o_ref[...] = x

    xn = xn_ref[...]
    gate_b = jnp.dot(xn, wb_ref[...], preferred_element_type=F32)
    z = jnp.dot(xn, wc_ref[...], preferred_element_type=F32) * jnp.dot(xn, wu_ref[...], preferred_element_type=F32)
    tm = z.shape[0]
    row = lax.broadcasted_iota(jnp.int32, z.shape, 0)
    r1 = pltpu.roll(z, 1, 0)
    r2 = pltpu.roll(z, 2, 0)
    if seg is None:
        @pl.when(i == 0)
        def _():
            zl_ref[j] = jnp.zeros((2, tn), F32)

        prev = zl_ref[j]
        p0, p1 = prev[0:1, :], prev[1:2, :]
        zl_ref[j] = z[tm - 2:, :]
    else:
        nb = tm // seg
        st = st_ref[...]
        p0 = jnp.broadcast_to(st[:, 0:1, :], (nb, seg, tn)).reshape(tm, tn)
        p1 = jnp.broadcast_to(st[:, 1:2, :], (nb, seg, tn)).reshape(tm, tn)
        row = row % seg
        zl_ref[...] = z.reshape(nb, seg, tn)[:, seg - 2:, :]
    zp1 = jnp.where(row == 0, p1, r1)
    zp2 = jnp.where(row == 0, p0, jnp.where(row == 1, p1, r2))
    cw = cw_ref[...]
    conv = zp2 * cw[0:1, :] + zp1 * cw[1:2, :] + z * cw[2:3, :]
    y = (gate_b * conv).astype(BF16)
    o_ref[...] += jnp.dot(y, wout_ref[...], preferred_element_type=F32)


def _conv_mixer(x, g, w_in, cw, w_out, state, layer, *, tm, tn, seg):
    t, d = x.shape
    tm = min(tm, t)
    nj = d // tn
    in_specs = [
        pl.BlockSpec((tm, d), lambda i, j: (i, 0)),
        pl.BlockSpec((None, 1, d), lambda i, j: (layer, 0, 0)),
        pl.BlockSpec((None, d, tn), lambda i, j: (layer, 0, j)),
        pl.BlockSpec((None, d, tn), lambda i, j: (layer, 0, nj + j)),
        pl.BlockSpec((None, d, tn), lambda i, j: (layer, 0, 2 * nj + j)),
        pl.BlockSpec((None, 3, tn), lambda i, j: (layer, 0, j)),
        pl.BlockSpec((None, tn, d), lambda i, j: (layer, j, 0)),
    ]
    args = [x, g, w_in, w_in, w_in, cw, w_out]
    if seg is None:
        zl_shape = (nj, 2, tn)
        zl_spec = pl.BlockSpec((nj, 2, tn), lambda i, j: (0, 0, 0))
    else:
        nb = tm // seg
        zl_shape = (t // seg, 2, d)
        zl_spec = pl.BlockSpec((nb, 2, tn), lambda i, j: (i, 0, j))
        in_specs.append(pl.BlockSpec((None, nb, 2, tn), lambda i, j: (layer, i, 0, j)))
        args.append(state)
    nbytes = 2 * (2 * tm * d * 4 + d * 3 * tn * 2 + tn * d * 2) + tm * d * 2 + tm * tn * 4 * 8
    y, zl = pl.pallas_call(
        functools.partial(_conv_body, seg=seg, tn=tn),
        grid=(t // tm, nj),
        in_specs=in_specs,
        out_specs=[pl.BlockSpec((tm, d), lambda i, j: (i, 0)), zl_spec],
        out_shape=[jax.ShapeDtypeStruct((t, d), F32), jax.ShapeDtypeStruct(zl_shape, F32)],
        scratch_shapes=[pltpu.VMEM((tm, d), BF16)],
        compiler_params=pltpu.CompilerParams(
            dimension_semantics=("arbitrary", "arbitrary"), vmem_limit_bytes=_vmem_limit(nbytes)),
        name="conv_mixer",
    )(*args)
    if seg is None:
        zl = zl.transpose(1, 0, 2).reshape(1, 2, d)
    return y, zl


def _kv_body(x_ref, g_ref, wkv_ref, kg_ref, cos_ref, sneg_ref, spos_ref, k_ref, v_ref, ka_ref, va_ref, *, dup):
    xn = _rms(x_ref[...], g_ref[...]).astype(BF16)
    kv = jnp.dot(xn, wkv_ref[...], preferred_element_type=F32)
    nkv = N_KV_HEADS * HEAD_DIM
    lo = _half_mask((xn.shape[0], LANES))
    for p in range(nkv // LANES):
        sl = slice(p * LANES, (p + 1) * LANES)
        kr = _head_norm_rope(kv[:, sl], kg_ref[...], cos_ref[...], sneg_ref[...], spos_ref[...])
        vr = kv[:, nkv + p * LANES: nkv + (p + 1) * LANES]
        k_ref[:, sl] = kr
        v_ref[:, sl] = vr
        if dup:
            for src, dst in ((kr, ka_ref), (vr, va_ref)):
                sw = pltpu.roll(src, HEAD_DIM, 1)
                dst[2 * p] = jnp.where(lo, src, sw).astype(BF16)
                dst[2 * p + 1] = jnp.where(lo, sw, src).astype(BF16)
        else:
            ka_ref[:, sl] = kr.astype(BF16)
            va_ref[sl, :] = vr.T.astype(BF16)


def _shared_kv(x, g, w_kv, kg, rope, *, tm, dup):
    t, d = x.shape
    tm = min(tm, t)
    nkv = N_KV_HEADS * HEAD_DIM
    row = lambda i: (i, 0)
    const = lambda i: (0, 0)
    if dup:
        aux_specs = [pl.BlockSpec((N_KV_HEADS, tm, LANES), lambda i: (0, i, 0))] * 2
        aux_shapes = [jax.ShapeDtypeStruct((N_KV_HEADS, t, LANES), BF16)] * 2
    else:
        aux_specs = [pl.BlockSpec((tm, nkv), row), pl.BlockSpec((nkv, tm), lambda i: (0, i))]
        aux_shapes = [jax.ShapeDtypeStruct((t, nkv), BF16), jax.ShapeDtypeStruct((nkv, t), BF16)]
    nbytes = 2 * (tm * d * 4 + d * 2 * nkv * 2 + 5 * tm * nkv * 4) + tm * d * 8
    return pl.pallas_call(
        functools.partial(_kv_body, dup=dup),
        grid=(t // tm,),
        in_specs=[
            pl.BlockSpec((tm, d), row),
            pl.BlockSpec((1, d), const),
            pl.BlockSpec((d, 2 * nkv), const),
            pl.BlockSpec((1, LANES), const),
            pl.BlockSpec((tm, LANES), row),
            pl.BlockSpec((tm, LANES), row),
            pl.BlockSpec((tm, LANES), row),
        ],
        out_specs=[pl.BlockSpec((tm, nkv), row), pl.BlockSpec((tm, nkv), row)] + aux_specs,
        out_shape=[jax.ShapeDtypeStruct((t, nkv), F32), jax.ShapeDtypeStruct((t, nkv), F32)] + aux_shapes,
        compiler_params=pltpu.CompilerParams(
            dimension_semantics=("arbitrary",), vmem_limit_bytes=_vmem_limit(nbytes)),
        name="shared_kv",
    )(x, g, w_kv, kg, *rope)


def _attend(qcat, kwin, vwin, valid, sink):
    s = lax.dot_general(qcat, kwin, (((1,), (1,)), ((), ())), preferred_element_type=F32)
    s = jnp.where(valid, s, -jnp.inf)
    m = jnp.maximum(jnp.max(s, axis=-1, keepdims=True), sink)
    e = jnp.exp2(s - m)
    den = jnp.sum(e, axis=-1, keepdims=True) + jnp.exp2(sink - m)
    o = jnp.dot(e.astype(BF16), vwin, preferred_element_type=F32)
    return o / den


def _project_q(x_ref, g_ref, wq_ref, qg_ref, cos_ref, sneg_ref, spos_ref, qe_ref, qo_ref):
    xn = _rms(x_ref[...], g_ref[...]).astype(BF16)
    q = jnp.dot(xn, wq_ref[...], preferred_element_type=F32)
    lo = _half_mask((q.shape[0], LANES))
    for p in range(q.shape[1] // LANES):
        sl = slice(p * LANES, (p + 1) * LANES)
        qr = _head_norm_rope(q[:, sl], qg_ref[...], cos_ref[...], sneg_ref[...], spos_ref[...]) * (SCALE * LOG2E)
        qe_ref[:, sl] = jnp.where(lo, qr, 0.0).astype(BF16)
        qo_ref[:, sl] = jnp.where(lo, 0.0, qr).astype(BF16)


def _attend_rows(qe_ref, qo_ref, att_ref, r0, nr, kh, kwin, vwin, valid, sink):
    pairs = GROUP // 2
    cols = [slice((pairs * kh + j) * LANES, (pairs * kh + j + 1) * LANES) for j in range(pairs)]
    qcat = jnp.concatenate([qe_ref[r0:r0 + nr, c] for c in cols] + [qo_ref[r0:r0 + nr, c] for c in cols], axis=0)
    o = _attend(qcat, kwin, vwin, valid, sink)
    lo = _half_mask((nr, LANES))
    for j, c in enumerate(cols):
        att_ref[r0:r0 + nr, c] = jnp.where(lo, o[j * nr:(j + 1) * nr], o[(pairs + j) * nr:(pairs + j + 1) * nr]).astype(BF16)


def _attn_prompt_body(x_ref, g_ref, wqt_ref, qg_ref, cos_ref, sin_ref,
                      kprev_ref, kcur_ref, vprev_ref, vcur_ref, sink_ref, wot_ref,
                      o_ref, qt_ref, att_ref, pt_ref, kw_ref, vw_ref):
    i = pl.program_id(0)
    tm, d = x_ref.shape
    half = ROT_DIM // 2
    x = x_ref[...]
    xn = _rms(x, g_ref[...]).astype(BF16)
    cos, sin = cos_ref[...], sin_ref[...]
    gain = jnp.concatenate([qg_ref[...]] * (tm // LANES), axis=1)
    rows = GROUP * HEAD_DIM

    nblk = tm // LANES
    dk = d // nblk

    def project_q(kh, c):
        return lax.dot_general(wqt_ref[kh * rows:(kh + 1) * rows, c * dk:(c + 1) * dk], xn[:, c * dk:(c + 1) * dk],
                               (((1,), (1,)), ((), ())), preferred_element_type=F32)

    def norm_rope_q(qt, kh, c):
        for j in range(c * (GROUP // nblk), (c + 1) * (GROUP // nblk)):
            t = qt[j * HEAD_DIM:(j + 1) * HEAD_DIM, :]
            tn = t * lax.rsqrt(jnp.sum(t * t, axis=0, keepdims=True) / HEAD_DIM + EPS) * gain
            x1, x2 = tn[:half], tn[half:ROT_DIM]
            h = GROUP * kh + j
            qt_ref[h * HEAD_DIM:(h + 1) * HEAD_DIM, :] = jnp.concatenate(
                [x1 * cos - x2 * sin, x2 * cos + x1 * sin, tn[ROT_DIM:]], axis=0).astype(BF16)

    def project_o(kh, c):
        return lax.dot_general(att_ref[kh * rows:(kh + 1) * rows, :], wot_ref[kh * rows:(kh + 1) * rows, c * dk:(c + 1) * dk],
                               (((0,), (0,)), ((), ())), preferred_element_type=F32)

    kw_ref[:WINDOW, :] = kprev_ref[...]
    kw_ref[WINDOW:, :] = kcur_ref[...]
    vw_ref[:, :WINDOW] = vprev_ref[...]
    vw_ref[:, WINDOW:] = vcur_ref[...]

    first_query_chunk = lax.broadcasted_iota(jnp.int32, (CHUNK, LANES), 1) < CHUNK
    has_past = jnp.broadcast_to(i > 0, (CHUNK, LANES))
    zeros = jnp.zeros((HEAD_DIM, LANES), BF16)

    def scores_t(p, kh):
        qs = slice(p * LANES, (p + 1) * LANES)
        kwin = kw_ref[p * LANES:p * LANES + KEYS, (kh // 2) * LANES:(kh // 2 + 1) * LANES]
        rhs = jnp.concatenate(
            [jnp.concatenate([qt_ref[h * HEAD_DIM:(h + 1) * HEAD_DIM, qs], zeros] if kh % 2 == 0 else
                             [zeros, qt_ref[h * HEAD_DIM:(h + 1) * HEAD_DIM, qs]], axis=0)
             for h in range(GROUP * kh, GROUP * (kh + 1))], axis=1)
        return jnp.dot(kwin, rhs, preferred_element_type=F32)

    def softmax_t(st, p, kh, slot):
        masks = [first_query_chunk & has_past if p == 0 else first_query_chunk, has_past if p == 0 else None,
                 None, ~first_query_chunk]
        dens = []
        for j in range(GROUP):
            s = jnp.concatenate(
                [st[c * CHUNK:(c + 1) * CHUNK, j * LANES:(j + 1) * LANES] if mask is None else
                 jnp.where(mask, st[c * CHUNK:(c + 1) * CHUNK, j * LANES:(j + 1) * LANES], -jnp.inf)
                 for c, mask in enumerate(masks)], axis=0)
            sink = sink_ref[GROUP * kh + j:GROUP * kh + j + 1, :]
            m = jnp.maximum(jnp.max(s, axis=0, keepdims=True), sink)
            e = jnp.exp2(s - m)
            dens.append(jnp.sum(e, axis=0, keepdims=True) + jnp.exp2(sink - m))
            pt_ref[slot, :, j * LANES:(j + 1) * LANES] = e.astype(BF16)
        return jnp.concatenate(dens, axis=1)

    def weighted_values_t(p, kh, slot, den):
        qs = slice(p * LANES, (p + 1) * LANES)
        ot = jnp.dot(vw_ref[kh * HEAD_DIM:(kh + 1) * HEAD_DIM, p * LANES:p * LANES + KEYS], pt_ref[slot],
                     preferred_element_type=F32)
        ot = ot * (1.0 / den)
        for j in range(GROUP):
            h = GROUP * kh + j
            att_ref[h * HEAD_DIM:(h + 1) * HEAD_DIM, qs] = ot[:, j * LANES:(j + 1) * LANES].astype(BF16)

    def full_q(kh):
        qt = project_q(kh, 0)
        for c in range(1, nblk):
            qt = qt + project_q(kh, c)
        return qt

    qts = {0: full_q(0)}
    for c in range(nblk):
        norm_rope_q(qts[0], 0, c)
    qts[1] = full_q(1)
    acc = [x[:, c * dk:(c + 1) * dk] for c in range(nblk)]
    for kh in range(N_KV_HEADS):
        st = scores_t(0, kh)
        pending = None
        for n in range(nblk):
            st_next = scores_t(n + 1, kh) if n + 1 < nblk else None
            if kh + 2 < N_KV_HEADS:
                piece = project_q(kh + 2, n)
                qts[kh + 2] = piece if n == 0 else qts[kh + 2] + piece
            if kh >= 1:
                acc[n] = acc[n] + project_o(kh - 1, n)
            den = softmax_t(st, n, kh, n % 2)
            if kh + 1 < N_KV_HEADS:
                norm_rope_q(qts[kh + 1], kh + 1, n)
            if pending is not None:
                weighted_values_t(*pending)
            pending = (n, kh, n % 2, den)
            st = st_next
        weighted_values_t(*pending)
    for c in range(nblk):
        o_ref[:, c * dk:(c + 1) * dk] = acc[c] + project_o(N_KV_HEADS - 1, c)


def _attn_sample_body(x_ref, g_ref, wq_ref, qg_ref, cos_ref, sneg_ref, spos_ref,
                      kc_ref, knew_ref, vc_ref, vnew_ref, sink_ref, wo_ref,
                      o_ref, qe_ref, qo_ref, att_ref, kw_ref, vw_ref, *, seg):
    tm = x_ref.shape[0]
    _project_q(x_ref, g_ref, wq_ref, qg_ref, cos_ref, sneg_ref, spos_ref, qe_ref, qo_ref)
    nkeys = WINDOW + seg
    valid = lax.broadcasted_iota(jnp.int32, (1, KEYS), 1) < nkeys
    kw_ref[nkeys:, :] = jnp.zeros((KEYS - nkeys, LANES), BF16)
    vw_ref[nkeys:, :] = jnp.zeros((KEYS - nkeys, LANES), BF16)
    for b in range(tm // seg):
        for kh in range(N_KV_HEADS):
            kw_ref[:WINDOW, :] = kc_ref[b, kh]
            kw_ref[WINDOW:nkeys, :] = knew_ref[kh, b * seg:(b + 1) * seg, :]
            vw_ref[:WINDOW, :] = vc_ref[b, kh]
            vw_ref[WINDOW:nkeys, :] = vnew_ref[kh, b * seg:(b + 1) * seg, :]
            _attend_rows(qe_ref, qo_ref, att_ref, b * seg, seg, kh, kw_ref[...], vw_ref[...], valid, sink_ref[kh])
    o_ref[...] = x_ref[...] + jnp.dot(att_ref[...], wo_ref[...], preferred_element_type=F32)


def _attn_mixer_sample(x, g, w_q, qg, rope, k2, v2, cache, sink_col, w_o, layer, blayer, *, tm, seg):
    t, d = x.shape
    tm = min(tm, t)
    nb = tm // seg
    row = lambda i: (i, 0)
    const = lambda i: (0, 0)
    resident = dict(pipeline_mode=pl.Buffered(1))
    cur = pl.BlockSpec((N_KV_HEADS, tm, LANES), lambda i: (0, i, 0))
    cspec = pl.BlockSpec((nb, N_KV_HEADS, WINDOW, LANES), lambda i: (i, 0, 0, 0))
    in_specs = [
        pl.BlockSpec((tm, d), row),
        pl.BlockSpec((None, 1, d), lambda i: (layer, 0, 0)),
        pl.BlockSpec((None, d, d), lambda i: (blayer, 0, 0), **resident),
        pl.BlockSpec((1, LANES), const),
        pl.BlockSpec((tm, LANES), row),
        pl.BlockSpec((tm, LANES), row),
        pl.BlockSpec((tm, LANES), row),
        cspec, cur, cspec, cur,
        pl.BlockSpec((N_KV_HEADS, GROUP * seg, 1), lambda i: (0, 0, 0)),
        pl.BlockSpec((None, d, d), lambda i: (blayer, 0, 0), **resident),
    ]
    nbytes = 2 * d * d * 2 + 4 * tm * d * 4 + tm * d * (4 + 3 * 2) + 8 * tm * LANES * 4 * 2 + 4 * 2 ** 20
    return pl.pallas_call(
        functools.partial(_attn_sample_body, seg=seg),
        grid=(t // tm,),
        in_specs=in_specs,
        out_specs=pl.BlockSpec((tm, d), row),
        out_shape=jax.ShapeDtypeStruct((t, d), F32),
        scratch_shapes=[pltpu.VMEM((tm, d), BF16), pltpu.VMEM((tm, d), BF16), pltpu.VMEM((tm, d), BF16),
                        pltpu.VMEM((KEYS, LANES), BF16), pltpu.VMEM((KEYS, LANES), BF16)],
        compiler_params=pltpu.CompilerParams(
            dimension_semantics=("arbitrary",), vmem_limit_bytes=_vmem_limit(nbytes)),
        name="attn_mixer_sample",
    )(x, g, w_q, qg, *rope, cache[0], k2, cache[1], v2, sink_col, w_o)


def _attn_mixer_prompt(x, g, w_qt, qg, cos_t, sin_t, kb, vt, sink_rows, w_ot, layer, blayer, *, tm):
    t, d = x.shape
    tm = min(tm, t)
    nkv = N_KV_HEADS * HEAD_DIM
    nh = d // HEAD_DIM
    half = ROT_DIM // 2
    prev_blk = lambda i: jnp.maximum(i * (tm // WINDOW) - 1, 0)
    resident = dict(pipeline_mode=pl.Buffered(1))
    in_specs = [
        pl.BlockSpec((tm, d), lambda i: (i, 0)),
        pl.BlockSpec((None, 1, d), lambda i: (layer, 0, 0)),
        pl.BlockSpec((None, d, d), lambda i: (blayer, 0, 0), **resident),
        pl.BlockSpec((HEAD_DIM, LANES), lambda i: (0, 0)),
        pl.BlockSpec((half, tm), lambda i: (0, i)),
        pl.BlockSpec((half, tm), lambda i: (0, i)),
        pl.BlockSpec((WINDOW, nkv), lambda i: (prev_blk(i), 0)),
        pl.BlockSpec((tm, nkv), lambda i: (i, 0)),
        pl.BlockSpec((nkv, WINDOW), lambda i: (0, prev_blk(i))),
        pl.BlockSpec((nkv, tm), lambda i: (0, i)),
        pl.BlockSpec((nh, LANES), lambda i: (0, 0)),
        pl.BlockSpec((None, d, d), lambda i: (blayer, 0, 0), **resident),
    ]
    nbytes = (2 * d * d * 2 + 4 * tm * d * 4 + 3 * tm * d * 4 + 2 * tm * d * 2 + 2 * KEYS * GROUP * LANES * 2
              + KEYS * GROUP * LANES * 4 * 2 + 4 * (WINDOW + tm) * nkv * 2)
    return pl.pallas_call(
        _attn_prompt_body,
        grid=(t // tm,),
        in_specs=in_specs,
        out_specs=pl.BlockSpec((tm, d), lambda i: (i, 0)),
        out_shape=jax.ShapeDtypeStruct((t, d), F32),
        scratch_shapes=[pltpu.VMEM((d, tm), BF16), pltpu.VMEM((d, tm), BF16),
                        pltpu.VMEM((2, KEYS, GROUP * LANES), BF16),
                        pltpu.VMEM((WINDOW + tm, nkv), BF16), pltpu.VMEM((nkv, WINDOW + tm), BF16)],
        compiler_params=pltpu.CompilerParams(
            dimension_semantics=("arbitrary",), vmem_limit_bytes=_vmem_limit(nbytes)),
        name="attn_mixer_prompt",
    )(x, g, w_qt, qg, cos_t, sin_t, kb, kb, vt, vt, sink_rows, w_ot)


def _rope_tables(pos):
    half = ROT_DIM // 2
    inv = ROPE_THETA ** (-jnp.arange(half, dtype=F32) / half)
    ang = pos.astype(F32)[:, None] * inv[None, :]
    cos, sin = jnp.cos(ang), jnp.sin(ang)
    n = pos.shape[0]
    one = jnp.ones((n, HEAD_DIM - ROT_DIM), F32)
    zero = jnp.zeros((n, HEAD_DIM - ROT_DIM), F32)
    zh = jnp.zeros((n, half), F32)
    c = jnp.concatenate([cos, cos, one], axis=1)
    sneg = jnp.concatenate([-sin, zh, zero], axis=1)
    spos = jnp.concatenate([zh, sin, zero], axis=1)
    return tuple(jnp.tile(a, (1, LANES // HEAD_DIM)) for a in (c, sneg, spos))


def _sink_column(sinks_l, rows_per_head):
    s = (sinks_l.astype(F32) * LOG2E).reshape(N_KV_HEADS, GROUP // 2, 2).transpose(0, 2, 1)
    return jnp.repeat(s.reshape(N_KV_HEADS, GROUP), rows_per_head, axis=1)[..., None]


def _dup_heads(t):
    t = t.transpose(0, 2, 1, 3)
    return jnp.concatenate([t, t], axis=-1).astype(BF16)


def _forward(x_prompt, x_sample, state_conv, cache_k, cache_v, mix_norm_g, mlp_norm_g, w_up, w_down,
             conv_w_in, conv_w, conv_w_out, kv_norm_g, w_kv, k_norm_g, w_q, q_norm_g, sinks, w_o,
             *, tm_mlp, tf, tf_cast, tm_conv, tn, tm_attn, tm_attn_s, tm_kv):
    _, s, d = x_prompt.shape
    b, l, _ = x_sample.shape
    n_a = conv_w_in.shape[0]
    depth = w_up.shape[0]
    xp = x_prompt.reshape(s, d)
    xs = x_sample.reshape(b * l, d)

    w_in_b, w_out_b = conv_w_in.astype(BF16), conv_w_out.astype(BF16)
    w_kv_b, w_q_b, w_o_b = w_kv.astype(BF16), w_q.astype(BF16), w_o.astype(BF16)
    w_qt_b, w_ot_b = w_q_b.transpose(0, 2, 1), w_o_b.transpose(0, 2, 1)
    mix_g = mix_norm_g.reshape(depth, 1, d)
    mlp_g = mlp_norm_g.reshape(depth, 1, d)

    half = ROT_DIM // 2
    ang_t = (ROPE_THETA ** (-jnp.arange(half, dtype=F32) / half))[:, None] * jnp.arange(s).astype(F32)[None, :]
    cos_t, sin_t = jnp.cos(ang_t), jnp.sin(ang_t)
    rope_p = _rope_tables(jnp.arange(s))
    rope_s = _rope_tables(jnp.tile(PAST_LEN + jnp.arange(l), b))
    kg = jnp.tile(k_norm_g.astype(F32), LANES // HEAD_DIM).reshape(1, LANES)

    conv_p, conv_s = [], []
    for i in range(depth):
        if i < n_a:
            xp, cp = _conv_mixer(xp, mix_g, w_in_b, conv_w, w_out_b, None, i, tm=tm_conv, tn=tn, seg=None)
            xs, cs = _conv_mixer(xs, mix_g, w_in_b, conv_w, w_out_b, state_conv, i, tm=tm_conv, tn=tn, seg=l)
            conv_p.append(cp)
            conv_s.append(cs)
        else:
            if i == n_a:
                kp, vp, kbp, vtp = _shared_kv(xp, kv_norm_g.reshape(1, d), w_kv_b, kg, rope_p, tm=tm_kv, dup=False)
                ks, vs, k2s, v2s = _shared_kv(xs, kv_norm_g.reshape(1, d), w_kv_b, kg, rope_s, tm=tm_kv, dup=True)
                cache2 = (_dup_heads(cache_k), _dup_heads(cache_v))
            j = i - n_a
            qg = jnp.tile(q_norm_g[j].astype(F32), LANES // HEAD_DIM).reshape(1, LANES)
            qg_t = jnp.broadcast_to((q_norm_g[j].astype(F32) * (SCALE * LOG2E))[:, None], (HEAD_DIM, LANES))
            sink_rows = jnp.broadcast_to((sinks[j].astype(F32) * LOG2E)[:, None], (sinks.shape[1], LANES))
            xp = _attn_mixer_prompt(xp, mix_g, w_qt_b, qg_t, cos_t, sin_t, kbp, vtp, sink_rows, w_o_b, i, j, tm=tm_attn)
            xs = _attn_mixer_sample(xs, mix_g, w_q_b, qg, rope_s, k2s, v2s, cache2, _sink_column(sinks[j], l),
                                    w_o_b, i, j, tm=tm_attn_s, seg=l)
        xs, w_up_b, w_down_b = _mlp_cast(xs, mlp_g, w_up, w_down, i, tf=tf_cast)
        xp = _mlp(xp, mlp_g, w_up_b, w_down_b, i, tm=tm_mlp, tf=tf)

    hd = (N_KV_HEADS, HEAD_DIM)
    ks_new = ks.reshape(b, l, *hd)
    vs_new = vs.reshape(b, l, *hd)
    return (xp.reshape(1, s, d), xs.reshape(b, l, d), jnp.stack(conv_p), jnp.stack(conv_s),
            kp[s - WINDOW:].reshape(1, WINDOW, *hd), vp[s - WINDOW:].reshape(1, WINDOW, *hd),
            jnp.concatenate([cache_k[:, l:], ks_new], axis=1), jnp.concatenate([cache_v[:, l:], vs_new], axis=1))


def kernel(x_prompt, x_sample, state_conv, cache_k, cache_v, mix_norm_g, mlp_norm_g, w_up, w_down, conv_w_in, conv_w, conv_w_out, kv_norm_g, w_kv, k_norm_g, w_q, q_norm_g, sinks, w_o):
    return _forward(x_prompt, x_sample, state_conv, cache_k, cache_v, mix_norm_g, mlp_norm_g, w_up, w_down,
                    conv_w_in, conv_w, conv_w_out, kv_norm_g, w_kv, k_norm_g, w_q, q_norm_g, sinks, w_o,
                    tm_mlp=512, tf=2048, tf_cast=512, tm_conv=512, tn=512, tm_attn=512, tm_attn_s=256, tm_kv=512)
```

```python
import functools

import jax
import jax.numpy as jnp
from jax import lax
from jax.experimental import pallas as pl
from jax.experimental.pallas import tpu as pltpu

EPS = 1e-6
CHUNK = 64
WINDOW = 128
HEAD_DIM = 64
N_KV_HEADS = 4
GROUP = 8
ROT_DIM = 16
ROPE_THETA = 500000.0
PAST_LEN = 2048
SCALE = HEAD_DIM ** -0.5
LOG2E = 1.4426950408889634

LANES = 128
KEYS = 2 * WINDOW
VMEM_LIMIT_CAP = 56 * 2 ** 20

F32 = jnp.float32
BF16 = jnp.bfloat16


def _vmem_limit(nbytes):
    return int(min(VMEM_LIMIT_CAP, max(32 * 2 ** 20, nbytes * 5 // 4 + 4 * 2 ** 20)))


def _rms(x, g):
    return x * lax.rsqrt(jnp.mean(x * x, axis=-1, keepdims=True) + EPS) * g


def _half_mask(shape):
    return lax.broadcasted_iota(jnp.int32, shape, len(shape) - 1) < HEAD_DIM


def _head_norm_rope(t, gain, cos, sneg, spos):
    lo = _half_mask(t.shape)
    sq = t * t
    s_lo = jnp.sum(jnp.where(lo, sq, 0.0), axis=-1, keepdims=True)
    s_hi = jnp.sum(jnp.where(lo, 0.0, sq), axis=-1, keepdims=True)
    inv = jnp.where(lo, lax.rsqrt(s_lo / HEAD_DIM + EPS), lax.rsqrt(s_hi / HEAD_DIM + EPS))
    tn = t * inv * gain
    half = ROT_DIM // 2
    return tn * cos + pltpu.roll(tn, LANES - half, 1) * sneg + pltpu.roll(tn, half, 1) * spos


def _mlp_body(x_ref, g_ref, wu_ref, wd_ref, o_ref, xn_ref):
    @pl.when(pl.program_id(1) == 0)
    def _():
        x = x_ref[...]
        xn_ref[...] = _rms(x, g_ref[...]).astype(BF16)
        o_ref[...] = x

    h = jnp.dot(xn_ref[...], wu_ref[...], preferred_element_type=F32)
    h = jnp.square(jnp.maximum(h, 0.0)).astype(BF16)
    o_ref[...] += jnp.dot(h, wd_ref[...], preferred_element_type=F32)


def _mlp_cast_body(x_ref, g_ref, wu_ref, wd_ref, o_ref, wub_ref, wdb_ref, xn_ref):
    @pl.when(pl.program_id(0) == 0)
    def _():
        x = x_ref[...]
        xn_ref[...] = _rms(x, g_ref[...]).astype(BF16)
        o_ref[...] = x

    wu = wu_ref[...].astype(BF16)
    wd = wd_ref[...].astype(BF16)
    wub_ref[...] = wu
    wdb_ref[...] = wd
    h = jnp.dot(xn_ref[...], wu, preferred_element_type=F32)
    h = jnp.square(jnp.maximum(h, 0.0)).astype(BF16)
    o_ref[...] += jnp.dot(h, wd, preferred_element_type=F32)


def _mlp_cast(x, g, w_up, w_down, layer, *, tf):
    t, d = x.shape
    f = w_up.shape[2]
    tf = min(tf, f)
    nbytes = 2 * t * d * 4 + t * d * 2 + 2 * 2 * d * tf * (4 + 2) + t * tf * 6
    return pl.pallas_call(
        _mlp_cast_body,
        grid=(f // tf,),
        in_specs=[
            pl.BlockSpec((t, d), lambda j: (0, 0), pipeline_mode=pl.Buffered(1)),
            pl.BlockSpec((None, 1, d), lambda j: (layer, 0, 0)),
            pl.BlockSpec((None, d, tf), lambda j: (layer, 0, j)),
            pl.BlockSpec((None, tf, d), lambda j: (layer, j, 0)),
        ],
        out_specs=[
            pl.BlockSpec((t, d), lambda j: (0, 0), pipeline_mode=pl.Buffered(1)),
            pl.BlockSpec((d, tf), lambda j: (0, j)),
            pl.BlockSpec((tf, d), lambda j: (j, 0)),
        ],
        out_shape=[
            jax.ShapeDtypeStruct((t, d), F32),
            jax.ShapeDtypeStruct((d, f), BF16),
            jax.ShapeDtypeStruct((f, d), BF16),
        ],
        scratch_shapes=[pltpu.VMEM((t, d), BF16)],
        compiler_params=pltpu.CompilerParams(
            dimension_semantics=("arbitrary",), vmem_limit_bytes=_vmem_limit(nbytes)),
        name="mlp_cast",
    )(x, g, w_up, w_down)


def _mlp(x, g, w_up, w_down, layer, *, tm, tf):
    t, d = x.shape
    f = w_up.shape[1]
    tm, tf = min(tm, t), min(tf, f)
    nbytes = 2 * (2 * tm * d * 4 + 2 * d * tf * 2) + tm * d * 2 + tm * tf * 6
    return pl.pallas_call(
        _mlp_body,
        grid=(t // tm, f // tf),
        in_specs=[
            pl.BlockSpec((tm, d), lambda i, j: (i, 0)),
            pl.BlockSpec((None, 1, d), lambda i, j: (layer, 0, 0)),
            pl.BlockSpec((d, tf), lambda i, j: (0, j)),
            pl.BlockSpec((tf, d), lambda i, j: (j, 0)),
        ],
        out_specs=pl.BlockSpec((tm, d), lambda i, j: (i, 0)),
        out_shape=jax.ShapeDtypeStruct((t, d), F32),
        scratch_shapes=[pltpu.VMEM((tm, d), BF16)],
        compiler_params=pltpu.CompilerParams(
            dimension_semantics=("arbitrary", "arbitrary"), vmem_limit_bytes=_vmem_limit(nbytes)),
        name="mlp",
    )(x, g, w_up, w_down)


def _conv_body(*refs, seg, tn):
    if seg is None:
        x_ref, g_ref, wb_ref, wc_ref, wu_ref, cw_ref, wout_ref, o_ref, zl_ref, xn_ref = refs
    else:
        x_ref, g_ref, wb_ref, wc_ref, wu_ref, cw_ref, wout_ref, st_ref, o_ref, zl_ref, xn_ref = refs
    i, j = pl.program_id(0), pl.program_id(1)

    @pl.when(j == 0)
    def _():
        x = x_ref[...]
        xn_ref[...] = _rms(x, g_ref[...]).astype(BF16)
        o_ref[...] = x

    xn = xn_ref[...]
    gate_b = jnp.dot(xn, wb_ref[...], preferred_element_type=F32)
    z = jnp.dot(xn, wc_ref[...], preferred_element_type=F32) * jnp.dot(xn, wu_ref[...], preferred_element_type=F32)
    tm = z.shape[0]
    row = lax.broadcasted_iota(jnp.int32, z.shape, 0)
    r1 = pltpu.roll(z, 1, 0)
    r2 = pltpu.roll(z, 2, 0)
    if seg is None:
        @pl.when(i == 0)
        def _():
            zl_ref[j] = jnp.zeros((2, tn), F32)

        prev = zl_ref[j]
        p0, p1 = prev[0:1, :], prev[1:2, :]
        zl_ref[j] = z[tm - 2:, :]
    else:
        nb = tm // seg
        st = st_ref[...]
        p0 = jnp.broadcast_to(st[:, 0:1, :], (nb, seg, tn)).reshape(tm, tn)
        p1 = jnp.broadcast_to(st[:, 1:2, :], (nb, seg, tn)).reshape(tm, tn)
        row = row % seg
        zl_ref[...] = z.reshape(nb, seg, tn)[:, seg - 2:, :]
    zp1 = jnp.where(row == 0, p1, r1)
    zp2 = jnp.where(row == 0, p0, jnp.where(row == 1, p1, r2))
    cw = cw_ref[...]
    conv = zp2 * cw[0:1, :] + zp1 * cw[1:2, :] + z * cw[2:3, :]
    y = (gate_b * conv).astype(BF16)
    o_ref[...] += jnp.dot(y, wout_ref[...], preferred_element_type=F32)


def _conv_mixer(x, g, w_in, cw, w_out, state, layer, *, tm, tn, seg):
    t, d = x.shape
    tm = min(tm, t)
    nj = d // tn
    in_specs = [
        pl.BlockSpec((tm, d), lambda i, j: (i, 0)),
        pl.BlockSpec((None, 1, d), lambda i, j: (layer, 0, 0)),
        pl.BlockSpec((None, d, tn), lambda i, j: (layer, 0, j)),
        pl.BlockSpec((None, d, tn), lambda i, j: (layer, 0, nj + j)),
        pl.BlockSpec((None, d, tn), lambda i, j: (layer, 0, 2 * nj + j)),
        pl.BlockSpec((None, 3, tn), lambda i, j: (layer, 0, j)),
        pl.BlockSpec((None, tn, d), lambda i, j: (layer, j, 0)),
    ]
    args = [x, g, w_in, w_in, w_in, cw, w_out]
    if seg is None:
        zl_shape = (nj, 2, tn)
        zl_spec = pl.BlockSpec((nj, 2, tn), lambda i, j: (0, 0, 0))
    else:
        nb = tm // seg
        zl_shape = (t // seg, 2, d)
        zl_spec = pl.BlockSpec((nb, 2, tn), lambda i, j: (i, 0, j))
        in_specs.append(pl.BlockSpec((None, nb, 2, tn), lambda i, j: (layer, i, 0, j)))
        args.append(state)
    nbytes = 2 * (2 * tm * d * 4 + d * 3 * tn * 2 + tn * d * 2) + tm * d * 2 + tm * tn * 4 * 8
    y, zl = pl.pallas_call(
        functools.partial(_conv_body, seg=seg, tn=tn),
        grid=(t // tm, nj),
        in_specs=in_specs,
        out_specs=[pl.BlockSpec((tm, d), lambda i, j: (i, 0)), zl_spec],
        out_shape=[jax.ShapeDtypeStruct((t, d), F32), jax.ShapeDtypeStruct(zl_shape, F32)],
        scratch_shapes=[pltpu.VMEM((tm, d), BF16)],
        compiler_params=pltpu.CompilerParams(
            dimension_semantics=("arbitrary", "arbitrary"), vmem_limit_bytes=_vmem_limit(nbytes)),
        name="conv_mixer",
    )(*args)
    if seg is None:
        zl = zl.transpose(1, 0, 2).reshape(1, 2, d)
    return y, zl


def _kv_body(x_ref, g_ref, wkv_ref, kg_ref, cos_ref, sneg_ref, spos_ref, k_ref, v_ref, ka_ref, va_ref, *, dup):
    xn = _rms(x_ref[...], g_ref[...]).astype(BF16)
    kv = jnp.dot(xn, wkv_ref[...], preferred_element_type=F32)
    nkv = N_KV_HEADS * HEAD_DIM
    lo = _half_mask((xn.shape[0], LANES))
    for p in range(nkv // LANES):
        sl = slice(p * LANES, (p + 1) * LANES)
        kr = _head_norm_rope(kv[:, sl], kg_ref[...], cos_ref[...], sneg_ref[...], spos_ref[...])
        vr = kv[:, nkv + p * LANES: nkv + (p + 1) * LANES]
        k_ref[:, sl] = kr
        v_ref[:, sl] = vr
        if dup:
            for src, dst in ((kr, ka_ref), (vr, va_ref)):
                sw = pltpu.roll(src, HEAD_DIM, 1)
                dst[2 * p] = jnp.where(lo, src, sw).astype(BF16)
                dst[2 * p + 1] = jnp.where(lo, sw, src).astype(BF16)
        else:
            ka_ref[:, sl] = kr.astype(BF16)
            va_ref[sl, :] = vr.T.astype(BF16)


def _shared_kv(x, g, w_kv, kg, rope, *, tm, dup):
    t, d = x.shape
    tm = min(tm, t)
    nkv = N_KV_HEADS * HEAD_DIM
    row = lambda i: (i, 0)
    const = lambda i: (0, 0)
    if dup:
        aux_specs = [pl.BlockSpec((N_KV_HEADS, tm, LANES), lambda i: (0, i, 0))] * 2
        aux_shapes = [jax.ShapeDtypeStruct((N_KV_HEADS, t, LANES), BF16)] * 2
    else:
        aux_specs = [pl.BlockSpec((tm, nkv), row), pl.BlockSpec((nkv, tm), lambda i: (0, i))]
        aux_shapes = [jax.ShapeDtypeStruct((t, nkv), BF16), jax.ShapeDtypeStruct((nkv, t), BF16)]
    nbytes = 2 * (tm * d * 4 + d * 2 * nkv * 2 + 5 * tm * nkv * 4) + tm * d * 8
    return pl.pallas_call(
        functools.partial(_kv_body, dup=dup),
        grid=(t // tm,),
        in_specs=[
            pl.BlockSpec((tm, d), row),
            pl.BlockSpec((1, d), const),
            pl.BlockSpec((d, 2 * nkv), const),
            pl.BlockSpec((1, LANES), const),
            pl.BlockSpec((tm, LANES), row),
            pl.BlockSpec((tm, LANES), row),
            pl.BlockSpec((tm, LANES), row),
        ],
        out_specs=[pl.BlockSpec((tm, nkv), row), pl.BlockSpec((tm, nkv), row)] + aux_specs,
        out_shape=[jax.ShapeDtypeStruct((t, nkv), F32), jax.ShapeDtypeStruct((t, nkv), F32)] + aux_shapes,
        compiler_params=pltpu.CompilerParams(
            dimension_semantics=("arbitrary",), vmem_limit_bytes=_vmem_limit(nbytes)),
        name="shared_kv",
    )(x, g, w_kv, kg, *rope)


def _attend(qcat, kwin, vwin, valid, sink):
    s = lax.dot_general(qcat, kwin, (((1,), (1,)), ((), ())), preferred_element_type=F32)
    s = jnp.where(valid, s, -jnp.inf)
    m = jnp.maximum(jnp.max(s, axis=-1, keepdims=True), sink)
    e = jnp.exp2(s - m)
    den = jnp.sum(e, axis=-1, keepdims=True) + jnp.exp2(sink - m)
    o = jnp.dot(e.astype(BF16), vwin, preferred_element_type=F32)
    return o / den


def _project_q(x_ref, g_ref, wq_ref, qg_ref, cos_ref, sneg_ref, spos_ref, qe_ref, qo_ref):
    xn = _rms(x_ref[...], g_ref[...]).astype(BF16)
    q = jnp.dot(xn, wq_ref[...], preferred_element_type=F32)
    lo = _half_mask((q.shape[0], LANES))
    for p in range(q.shape[1] // LANES):
        sl = slice(p * LANES, (p + 1) * LANES)
        qr = _head_norm_rope(q[:, sl], qg_ref[...], cos_ref[...], sneg_ref[...], spos_ref[...]) * (SCALE * LOG2E)
        qe_ref[:, sl] = jnp.where(lo, qr, 0.0).astype(BF16)
        qo_ref[:, sl] = jnp.where(lo, 0.0, qr).astype(BF16)


def _attend_rows(qe_ref, qo_ref, att_ref, r0, nr, kh, kwin, vwin, valid, sink):
    pairs = GROUP // 2
    cols = [slice((pairs * kh + j) * LANES, (pairs * kh + j + 1) * LANES) for j in range(pairs)]
    qcat = jnp.concatenate([qe_ref[r0:r0 + nr, c] for c in cols] + [qo_ref[r0:r0 + nr, c] for c in cols], axis=0)
    o = _attend(qcat, kwin, vwin, valid, sink)
    lo = _half_mask((nr, LANES))
    for j, c in enumerate(cols):
        att_ref[r0:r0 + nr, c] = jnp.where(lo, o[j * nr:(j + 1) * nr], o[(pairs + j) * nr:(pairs + j + 1) * nr]).astype(BF16)


def _attn_prompt_body(x_ref, g_ref, wqt_ref, qg_ref, cos_ref, sin_ref,
                      kprev_ref, kcur_ref, vprev_ref, vcur_ref, sink_ref, wot_ref,
                      o_ref, qt_ref, att_ref, pt_ref, kw_ref, vw_ref):
    i = pl.program_id(0)
    tm, d = x_ref.shape
    half = ROT_DIM // 2
    x = x_ref[...]
    xn = _rms(x, g_ref[...]).astype(BF16)
    cos, sin = cos_ref[...], sin_ref[...]
    gain = jnp.concatenate([qg_ref[...]] * (tm // LANES), axis=1)
    rows = GROUP * HEAD_DIM

    nblk = tm // LANES
    dk = d // nblk

    def project_q(kh, c):
        return lax.dot_general(wqt_ref[kh * rows:(kh + 1) * rows, c * dk:(c + 1) * dk], xn[:, c * dk:(c + 1) * dk],
                               (((1,), (1,)), ((), ())), preferred_element_type=F32)

    def norm_rope_q(qt, kh, c):
        for j in range(c * (GROUP // nblk), (c + 1) * (GROUP // nblk)):
            t = qt[j * HEAD_DIM:(j + 1) * HEAD_DIM, :]
            tn = t * lax.rsqrt(jnp.sum(t * t, axis=0, keepdims=True) / HEAD_DIM + EPS) * gain
            x1, x2 = tn[:half], tn[half:ROT_DIM]
            h = GROUP * kh + j
            qt_ref[h * HEAD_DIM:(h + 1) * HEAD_DIM, :] = jnp.concatenate(
                [x1 * cos - x2 * sin, x2 * cos + x1 * sin, tn[ROT_DIM:]], axis=0).astype(BF16)


    kw_ref[:WINDOW, :] = kprev_ref[...]
    kw_ref[WINDOW:, :] = kcur_ref[...]
    vw_ref[:, :WINDOW] = vprev_ref[...]
    vw_ref[:, WINDOW:] = vcur_ref[...]

    first_query_chunk = lax.broadcasted_iota(jnp.int32, (CHUNK, LANES), 1) < CHUNK
    has_past = jnp.broadcast_to(i > 0, (CHUNK, LANES))
    zeros = jnp.zeros((HEAD_DIM, LANES), BF16)

    def scores_t(p, kh):
        qs = slice(p * LANES, (p + 1) * LANES)
        kwin = kw_ref[p * LANES:p * LANES + KEYS, (kh // 2) * LANES:(kh // 2 + 1) * LANES]
        rhs = jnp.concatenate(
            [jnp.concatenate([qt_ref[h * HEAD_DIM:(h + 1) * HEAD_DIM, qs], zeros] if kh % 2 == 0 else
                             [zeros, qt_ref[h * HEAD_DIM:(h + 1) * HEAD_DIM, qs]], axis=0)
             for h in range(GROUP * kh, GROUP * (kh + 1))], axis=1)
        return jnp.dot(kwin, rhs, preferred_element_type=F32)

    def softmax_t(st, p, kh, slot):
        masks = [first_query_chunk & has_past if p == 0 else first_query_chunk, has_past if p == 0 else None,
                 None, ~first_query_chunk]
        sink_terms = []
        for j in range(GROUP):
            s = jnp.concatenate(
                [st[c * CHUNK:(c + 1) * CHUNK, j * LANES:(j + 1) * LANES] if mask is None else
                 jnp.where(mask, st[c * CHUNK:(c + 1) * CHUNK, j * LANES:(j + 1) * LANES], -jnp.inf)
                 for c, mask in enumerate(masks)], axis=0)
            sink = sink_ref[GROUP * kh + j:GROUP * kh + j + 1, :]
            m = jnp.maximum(jnp.max(s, axis=0, keepdims=True), sink)
            sink_terms.append(jnp.exp2(sink - m))
            pt_ref[slot, :, j * LANES:(j + 1) * LANES] = jnp.exp2(s - m).astype(BF16)
        return jnp.concatenate(sink_terms, axis=1)

    ones_rows = jnp.ones((16, KEYS), BF16)

    def weighted_values_t(p, kh, slot, sink_term):
        qs = slice(p * LANES, (p + 1) * LANES)
        v_ones = jnp.concatenate([vw_ref[kh * HEAD_DIM:(kh + 1) * HEAD_DIM, p * LANES:p * LANES + KEYS], ones_rows], axis=0)
        ot = jnp.dot(v_ones, pt_ref[slot], preferred_element_type=F32)
        ot = ot[:HEAD_DIM] * (1.0 / (ot[HEAD_DIM:HEAD_DIM + 1] + sink_term))
        for j in range(GROUP):
            h = GROUP * kh + j
            att_ref[h * HEAD_DIM:(h + 1) * HEAD_DIM, qs] = ot[:, j * LANES:(j + 1) * LANES].astype(BF16)

    def full_q(kh):
        qt = project_q(kh, 0)
        for c in range(1, nblk):
            qt = qt + project_q(kh, c)
        return qt

    qts = {0: full_q(0)}
    for c in range(nblk):
        norm_rope_q(qts[0], 0, c)
    qts[1] = full_q(1)
    for kh in range(N_KV_HEADS):
        st = scores_t(0, kh)
        pending = None
        for n in range(nblk):
            st_next = scores_t(n + 1, kh) if n + 1 < nblk else None
            if kh + 2 < N_KV_HEADS:
                piece = project_q(kh + 2, n)
                qts[kh + 2] = piece if n == 0 else qts[kh + 2] + piece
            den = softmax_t(st, n, kh, n % 2)
            if kh + 1 < N_KV_HEADS:
                norm_rope_q(qts[kh + 1], kh + 1, n)
            if pending is not None:
                weighted_values_t(*pending)
            pending = (n, kh, n % 2, den)
            st = st_next
        weighted_values_t(*pending)
    o_ref[...] = x + lax.dot_general(att_ref[...], wot_ref[...], (((0,), (0,)), ((), ())), preferred_element_type=F32)


def _attn_sample_body(x_ref, g_ref, wq_ref, qg_ref, cos_ref, sneg_ref, spos_ref,
                      kc_ref, knew_ref, vc_ref, vnew_ref, sink_ref, wo_ref,
                      o_ref, qe_ref, qo_ref, att_ref, kw_ref, vw_ref, *, seg):
    tm = x_ref.shape[0]
    _project_q(x_ref, g_ref, wq_ref, qg_ref, cos_ref, sneg_ref, spos_ref, qe_ref, qo_ref)
    nkeys = WINDOW + seg
    valid = lax.broadcasted_iota(jnp.int32, (1, KEYS), 1) < nkeys
    kw_ref[nkeys:, :] = jnp.zeros((KEYS - nkeys, LANES), BF16)
    vw_ref[nkeys:, :] = jnp.zeros((KEYS - nkeys, LANES), BF16)
    for b in range(tm // seg):
        for kh in range(N_KV_HEADS):
            kw_ref[:WINDOW, :] = kc_ref[b, kh]
            kw_ref[WINDOW:nkeys, :] = knew_ref[kh, b * seg:(b + 1) * seg, :]
            vw_ref[:WINDOW, :] = vc_ref[b, kh]
            vw_ref[WINDOW:nkeys, :] = vnew_ref[kh, b * seg:(b + 1) * seg, :]
            _attend_rows(qe_ref, qo_ref, att_ref, b * seg, seg, kh, kw_ref[...], vw_ref[...], valid, sink_ref[kh])
    o_ref[...] = x_ref[...] + jnp.dot(att_ref[...], wo_ref[...], preferred_element_type=F32)


def _attn_mixer_sample(x, g, w_q, qg, rope, k2, v2, cache, sink_col, w_o, layer, blayer, *, tm, seg):
    t, d = x.shape
    tm = min(tm, t)
    nb = tm // seg
    row = lambda i: (i, 0)
    const = lambda i: (0, 0)
    resident = dict(pipeline_mode=pl.Buffered(1))
    cur = pl.BlockSpec((N_KV_HEADS, tm, LANES), lambda i: (0, i, 0))
    cspec = pl.BlockSpec((nb, N_KV_HEADS, WINDOW, LANES), lambda i: (i, 0, 0, 0))
    in_specs = [
        pl.BlockSpec((tm, d), row),
        pl.BlockSpec((None, 1, d), lambda i: (layer, 0, 0)),
        pl.BlockSpec((None, d, d), lambda i: (blayer, 0, 0), **resident),
        pl.BlockSpec((1, LANES), const),
        pl.BlockSpec((tm, LANES), row),
        pl.BlockSpec((tm, LANES), row),
        pl.BlockSpec((tm, LANES), row),
        cspec, cur, cspec, cur,
        pl.BlockSpec((N_KV_HEADS, GROUP * seg, 1), lambda i: (0, 0, 0)),
        pl.BlockSpec((None, d, d), lambda i: (blayer, 0, 0), **resident),
    ]
    nbytes = 2 * d * d * 2 + 4 * tm * d * 4 + tm * d * (4 + 3 * 2) + 8 * tm * LANES * 4 * 2 + 4 * 2 ** 20
    return pl.pallas_call(
        functools.partial(_attn_sample_body, seg=seg),
        grid=(t // tm,),
        in_specs=in_specs,
        out_specs=pl.BlockSpec((tm, d), row),
        out_shape=jax.ShapeDtypeStruct((t, d), F32),
        scratch_shapes=[pltpu.VMEM((tm, d), BF16), pltpu.VMEM((tm, d), BF16), pltpu.VMEM((tm, d), BF16),
                        pltpu.VMEM((KEYS, LANES), BF16), pltpu.VMEM((KEYS, LANES), BF16)],
        compiler_params=pltpu.CompilerParams(
            dimension_semantics=("arbitrary",), vmem_limit_bytes=_vmem_limit(nbytes)),
        name="attn_mixer_sample",
    )(x, g, w_q, qg, *rope, cache[0], k2, cache[1], v2, sink_col, w_o)


def _attn_mixer_prompt(x, g, w_qt, qg, cos_t, sin_t, kb, vt, sink_rows, w_ot, layer, blayer, *, tm):
    t, d = x.shape
    tm = min(tm, t)
    nkv = N_KV_HEADS * HEAD_DIM
    nh = d // HEAD_DIM
    half = ROT_DIM // 2
    prev_blk = lambda i: jnp.maximum(i * (tm // WINDOW) - 1, 0)
    resident = dict(pipeline_mode=pl.Buffered(1))
    in_specs = [
        pl.BlockSpec((tm, d), lambda i: (i, 0)),
        pl.BlockSpec((None, 1, d), lambda i: (layer, 0, 0)),
        pl.BlockSpec((None, d, d), lambda i: (blayer, 0, 0), **resident),
        pl.BlockSpec((HEAD_DIM, LANES), lambda i: (0, 0)),
        pl.BlockSpec((half, tm), lambda i: (0, i)),
        pl.BlockSpec((half, tm), lambda i: (0, i)),
        pl.BlockSpec((WINDOW, nkv), lambda i: (prev_blk(i), 0)),
        pl.BlockSpec((tm, nkv), lambda i: (i, 0)),
        pl.BlockSpec((nkv, WINDOW), lambda i: (0, prev_blk(i))),
        pl.BlockSpec((nkv, tm), lambda i: (0, i)),
        pl.BlockSpec((nh, LANES), lambda i: (0, 0)),
        pl.BlockSpec((None, d, d), lambda i: (blayer, 0, 0), **resident),
    ]
    nbytes = (2 * d * d * 2 + 4 * tm * d * 4 + 3 * tm * d * 4 + 2 * tm * d * 2 + 2 * KEYS * GROUP * LANES * 2
              + KEYS * GROUP * LANES * 4 * 2 + 4 * (WINDOW + tm) * nkv * 2)
    return pl.pallas_call(
        _attn_prompt_body,
        grid=(t // tm,),
        in_specs=in_specs,
        out_specs=pl.BlockSpec((tm, d), lambda i: (i, 0)),
        out_shape=jax.ShapeDtypeStruct((t, d), F32),
        scratch_shapes=[pltpu.VMEM((d, tm), BF16), pltpu.VMEM((d, tm), BF16),
                        pltpu.VMEM((2, KEYS, GROUP * LANES), BF16),
                        pltpu.VMEM((WINDOW + tm, nkv), BF16), pltpu.VMEM((nkv, WINDOW + tm), BF16)],
        compiler_params=pltpu.CompilerParams(
            dimension_semantics=("arbitrary",), vmem_limit_bytes=_vmem_limit(nbytes)),
        name="attn_mixer_prompt",
    )(x, g, w_qt, qg, cos_t, sin_t, kb, kb, vt, vt, sink_rows, w_ot)


def _rope_tables(pos):
    half = ROT_DIM // 2
    inv = ROPE_THETA ** (-jnp.arange(half, dtype=F32) / half)
    ang = pos.astype(F32)[:, None] * inv[None, :]
    cos, sin = jnp.cos(ang), jnp.sin(ang)
    n = pos.shape[0]
    one = jnp.ones((n, HEAD_DIM - ROT_DIM), F32)
    zero = jnp.zeros((n, HEAD_DIM - ROT_DIM), F32)
    zh = jnp.zeros((n, half), F32)
    c = jnp.concatenate([cos, cos, one], axis=1)
    sneg = jnp.concatenate([-sin, zh, zero], axis=1)
    spos = jnp.concatenate([zh, sin, zero], axis=1)
    return tuple(jnp.tile(a, (1, LANES // HEAD_DIM)) for a in (c, sneg, spos))


def _sink_column(sinks_l, rows_per_head):
    s = (sinks_l.astype(F32) * LOG2E).reshape(N_KV_HEADS, GROUP // 2, 2).transpose(0, 2, 1)
    return jnp.repeat(s.reshape(N_KV_HEADS, GROUP), rows_per_head, axis=1)[..., None]


def _dup_heads(t):
    t = t.transpose(0, 2, 1, 3)
    return jnp.concatenate([t, t], axis=-1).astype(BF16)


def _forward(x_prompt, x_sample, state_conv, cache_k, cache_v, mix_norm_g, mlp_norm_g, w_up, w_down,
             conv_w_in, conv_w, conv_w_out, kv_norm_g, w_kv, k_norm_g, w_q, q_norm_g, sinks, w_o,
             *, tm_mlp, tf, tf_cast, tm_conv, tn, tm_attn, tm_attn_s, tm_kv):
    _, s, d = x_prompt.shape
    b, l, _ = x_sample.shape
    n_a = conv_w_in.shape[0]
    depth = w_up.shape[0]
    xp = x_prompt.reshape(s, d)
    xs = x_sample.reshape(b * l, d)

    w_in_b, w_out_b = conv_w_in.astype(BF16), conv_w_out.astype(BF16)
    w_kv_b, w_q_b, w_o_b = w_kv.astype(BF16), w_q.astype(BF16), w_o.astype(BF16)
    w_qt_b, w_ot_b = w_q_b.transpose(0, 2, 1), w_o_b.transpose(0, 2, 1)
    mix_g = mix_norm_g.reshape(depth, 1, d)
    mlp_g = mlp_norm_g.reshape(depth, 1, d)

    half = ROT_DIM // 2
    ang_t = (ROPE_THETA ** (-jnp.arange(half, dtype=F32) / half))[:, None] * jnp.arange(s).astype(F32)[None, :]
    cos_t, sin_t = jnp.cos(ang_t), jnp.sin(ang_t)
    rope_p = _rope_tables(jnp.arange(s))
    rope_s = _rope_tables(jnp.tile(PAST_LEN + jnp.arange(l), b))
    kg = jnp.tile(k_norm_g.astype(F32), LANES // HEAD_DIM).reshape(1, LANES)

    conv_p, conv_s = [], []
    for i in range(depth):
        if i < n_a:
            xp, cp = _conv_mixer(xp, mix_g, w_in_b, conv_w, w_out_b, None, i, tm=tm_conv, tn=tn, seg=None)
            xs, cs = _conv_mixer(xs, mix_g, w_in_b, conv_w, w_out_b, state_conv, i, tm=tm_conv, tn=tn, seg=l)
            conv_p.append(cp)
            conv_s.append(cs)
        else:
            if i == n_a:
                kp, vp, kbp, vtp = _shared_kv(xp, kv_norm_g.reshape(1, d), w_kv_b, kg, rope_p, tm=tm_kv, dup=False)
                ks, vs, k2s, v2s = _shared_kv(xs, kv_norm_g.reshape(1, d), w_kv_b, kg, rope_s, tm=tm_kv, dup=True)
                cache2 = (_dup_heads(cache_k), _dup_heads(cache_v))
            j = i - n_a
            qg = jnp.tile(q_norm_g[j].astype(F32), LANES // HEAD_DIM).reshape(1, LANES)
            qg_t = jnp.broadcast_to((q_norm_g[j].astype(F32) * (SCALE * LOG2E))[:, None], (HEAD_DIM, LANES))
            sink_rows = jnp.broadcast_to((sinks[j].astype(F32) * LOG2E)[:, None], (sinks.shape[1], LANES))
            xp = _attn_mixer_prompt(xp, mix_g, w_qt_b, qg_t, cos_t, sin_t, kbp, vtp, sink_rows, w_o_b, i, j, tm=tm_attn)
            xs = _attn_mixer_sample(xs, mix_g, w_q_b, qg, rope_s, k2s, v2s, cache2, _sink_column(sinks[j], l),
                                    w_o_b, i, j, tm=tm_attn_s, seg=l)
        xs, w_up_b, w_down_b = _mlp_cast(xs, mlp_g, w_up, w_down, i, tf=tf_cast)
        xp = _mlp(xp, mlp_g, w_up_b, w_down_b, i, tm=tm_mlp, tf=tf)

    hd = (N_KV_HEADS, HEAD_DIM)
    ks_new = ks.reshape(b, l, *hd)
    vs_new = vs.reshape(b, l, *hd)
    return (xp.reshape(1, s, d), xs.reshape(b, l, d), jnp.stack(conv_p), jnp.stack(conv_s),
            kp[s - WINDOW:].reshape(1, WINDOW, *hd), vp[s - WINDOW:].reshape(1, WINDOW, *hd),
            jnp.concatenate([cache_k[:, l:], ks_new], axis=1), jnp.concatenate([cache_v[:, l:], vs_new], axis=1))


def kernel(x_prompt, x_sample, state_conv, cache_k, cache_v, mix_norm_g, mlp_norm_g, w_up, w_down, conv_w_in, conv_w, conv_w_out, kv_norm_g, w_kv, k_norm_g, w_q, q_norm_g, sinks, w_o):
    return _forward(x_prompt, x_sample, state_conv, cache_k, cache_v, mix_norm_g, mlp_norm_g, w_up, w_down,
                    conv_w_in, conv_w, conv_w_out, kv_norm_g, w_kv, k_norm_g, w_q, q_norm_g, sinks, w_o,
                    tm_mlp=512, tf=2048, tf_cast=512, tm_conv=512, tn=512, tm_attn=512, tm_attn_s=256, tm_kv=512)
```

```python
import functools

import jax
import jax.numpy as jnp
from jax import lax
from jax.experimental import pallas as pl
from jax.experimental.pallas import tpu as pltpu

EPS = 1e-6
CHUNK = 64
WINDOW = 128
HEAD_DIM = 64
N_KV_HEADS = 4
GROUP = 8
ROT_DIM = 16
ROPE_THETA = 500000.0
PAST_LEN = 2048
SCALE = HEAD_DIM ** -0.5
LOG2E = 1.4426950408889634

LANES = 128
KEYS = 2 * WINDOW
VMEM_LIMIT_CAP = 56 * 2 ** 20

F32 = jnp.float32
BF16 = jnp.bfloat16


def _vmem_limit(nbytes):
    return int(min(VMEM_LIMIT_CAP, max(32 * 2 ** 20, nbytes * 5 // 4 + 4 * 2 ** 20)))


def _rms(x, g):
    return x * lax.rsqrt(jnp.mean(x * x, axis=-1, keepdims=True) + EPS) * g


def _half_mask(shape):
    return lax.broadcasted_iota(jnp.int32, shape, len(shape) - 1) < HEAD_DIM


def _head_norm_rope(t, gain, cos, sneg, spos):
    lo = _half_mask(t.shape)
    sq = t * t
    s_lo = jnp.sum(jnp.where(lo, sq, 0.0), axis=-1, keepdims=True)
    s_hi = jnp.sum(jnp.where(lo, 0.0, sq), axis=-1, keepdims=True)
    inv = jnp.where(lo, lax.rsqrt(s_lo / HEAD_DIM + EPS), lax.rsqrt(s_hi / HEAD_DIM + EPS))
    tn = t * inv * gain
    half = ROT_DIM // 2
    return tn * cos + pltpu.roll(tn, LANES - half, 1) * sneg + pltpu.roll(tn, half, 1) * spos


def _mlp_body(x_ref, g_ref, wu_ref, wd_ref, o_ref, xn_ref):
    def contribution():
        h = jnp.dot(xn_ref[...], wu_ref[...], preferred_element_type=F32)
        h = jnp.square(jnp.maximum(h, 0.0)).astype(BF16)
        return jnp.dot(h, wd_ref[...], preferred_element_type=F32)

    @pl.when(pl.program_id(1) == 0)
    def _():
        xn_ref[...] = _rms(x_ref[...], g_ref[...]).astype(BF16)
        o_ref[...] = x_ref[...] + contribution()

    @pl.when(pl.program_id(1) > 0)
    def _():
        o_ref[...] += contribution()


def _mlp_cast_body(x_ref, g_ref, wu_ref, wd_ref, o_ref, wub_ref, wdb_ref, xn_ref):
    def contribution():
        wu = wu_ref[...].astype(BF16)
        wd = wd_ref[...].astype(BF16)
        wub_ref[...] = wu
        wdb_ref[...] = wd
        h = jnp.dot(xn_ref[...], wu, preferred_element_type=F32)
        h = jnp.square(jnp.maximum(h, 0.0)).astype(BF16)
        return jnp.dot(h, wd, preferred_element_type=F32)

    @pl.when(pl.program_id(0) == 0)
    def _():
        xn_ref[...] = _rms(x_ref[...], g_ref[...]).astype(BF16)
        o_ref[...] = x_ref[...] + contribution()

    @pl.when(pl.program_id(0) > 0)
    def _():
        o_ref[...] += contribution()


def _mlp_cast(x, g, w_up, w_down, layer, *, tf):
    t, d = x.shape
    f = w_up.shape[2]
    tf = min(tf, f)
    nbytes = 2 * t * d * 4 + t * d * 2 + 2 * 2 * d * tf * (4 + 2) + t * tf * 6
    return pl.pallas_call(
        _mlp_cast_body,
        grid=(f // tf,),
        in_specs=[
            pl.BlockSpec((t, d), lambda j: (0, 0), pipeline_mode=pl.Buffered(1)),
            pl.BlockSpec((None, 1, d), lambda j: (layer, 0, 0)),
            pl.BlockSpec((None, d, tf), lambda j: (layer, 0, j)),
            pl.BlockSpec((None, tf, d), lambda j: (layer, j, 0)),
        ],
        out_specs=[
            pl.BlockSpec((t, d), lambda j: (0, 0), pipeline_mode=pl.Buffered(1)),
            pl.BlockSpec((d, tf), lambda j: (0, j)),
            pl.BlockSpec((tf, d), lambda j: (j, 0)),
        ],
        out_shape=[
            jax.ShapeDtypeStruct((t, d), F32),
            jax.ShapeDtypeStruct((d, f), BF16),
            jax.ShapeDtypeStruct((f, d), BF16),
        ],
        scratch_shapes=[pltpu.VMEM((t, d), BF16)],
        compiler_params=pltpu.CompilerParams(
            dimension_semantics=("arbitrary",), vmem_limit_bytes=_vmem_limit(nbytes)),
        name="mlp_cast",
    )(x, g, w_up, w_down)


def _mlp(x, g, w_up, w_down, layer, *, tm, tf):
    t, d = x.shape
    f = w_up.shape[1]
    tm, tf = min(tm, t), min(tf, f)
    nbytes = 2 * (2 * tm * d * 4 + 2 * d * tf * 2) + tm * d * 2 + tm * tf * 6
    return pl.pallas_call(
        _mlp_body,
        grid=(t // tm, f // tf),
        in_specs=[
            pl.BlockSpec((tm, d), lambda i, j: (i, 0)),
            pl.BlockSpec((None, 1, d), lambda i, j: (layer, 0, 0)),
            pl.BlockSpec((d, tf), lambda i, j: (0, j)),
            pl.BlockSpec((tf, d), lambda i, j: (j, 0)),
        ],
        out_specs=pl.BlockSpec((tm, d), lambda i, j: (i, 0)),
        out_shape=jax.ShapeDtypeStruct((t, d), F32),
        scratch_shapes=[pltpu.VMEM((tm, d), BF16)],
        compiler_params=pltpu.CompilerParams(
            dimension_semantics=("arbitrary", "arbitrary"), vmem_limit_bytes=_vmem_limit(nbytes)),
        name="mlp",
    )(x, g, w_up, w_down)


def _conv_body(*refs, seg, tn):
    if seg is None:
        x_ref, g_ref, wb_ref, wc_ref, wu_ref, cw_ref, wout_ref, o_ref, zl_ref, xn_ref = refs
    else:
        x_ref, g_ref, wb_ref, wc_ref, wu_ref, cw_ref, wout_ref, st_ref, o_ref, zl_ref, xn_ref = refs
    i, j = pl.program_id(0), pl.program_id(1)

    if seg is None:
        @pl.when(i == 0)
        def _():
            zl_ref[j] = jnp.zeros((2, tn), F32)

    def contribution():
        xn = xn_ref[...]
        gate_b = jnp.dot(xn, wb_ref[...], preferred_element_type=F32)
        z = jnp.dot(xn, wc_ref[...], preferred_element_type=F32) * jnp.dot(xn, wu_ref[...], preferred_element_type=F32)
        tm = z.shape[0]
        row = lax.broadcasted_iota(jnp.int32, z.shape, 0)
        r1 = pltpu.roll(z, 1, 0)
        r2 = pltpu.roll(z, 2, 0)
        if seg is None:
            prev = zl_ref[j]
            p0, p1 = prev[0:1, :], prev[1:2, :]
            zl_ref[j] = z[tm - 2:, :]
        else:
            nb = tm // seg
            st = st_ref[...]
            p0 = jnp.broadcast_to(st[:, 0:1, :], (nb, seg, tn)).reshape(tm, tn)
            p1 = jnp.broadcast_to(st[:, 1:2, :], (nb, seg, tn)).reshape(tm, tn)
            row = row % seg
            zl_ref[...] = z.reshape(nb, seg, tn)[:, seg - 2:, :]
        zp1 = jnp.where(row == 0, p1, r1)
        zp2 = jnp.where(row == 0, p0, jnp.where(row == 1, p1, r2))
        cw = cw_ref[...]
        conv = zp2 * cw[0:1, :] + zp1 * cw[1:2, :] + z * cw[2:3, :]
        y = (gate_b * conv).astype(BF16)
        return jnp.dot(y, wout_ref[...], preferred_element_type=F32)

    @pl.when(j == 0)
    def _():
        xn_ref[...] = _rms(x_ref[...], g_ref[...]).astype(BF16)
        o_ref[...] = x_ref[...] + contribution()

    @pl.when(j > 0)
    def _():
        o_ref[...] += contribution()


def _conv_mixer(x, g, w_in, cw, w_out, state, layer, *, tm, tn, seg):
    t, d = x.shape
    tm = min(tm, t)
    nj = d // tn
    in_specs = [
        pl.BlockSpec((tm, d), lambda i, j: (i, 0)),
        pl.BlockSpec((None, 1, d), lambda i, j: (layer, 0, 0)),
        pl.BlockSpec((None, d, tn), lambda i, j: (layer, 0, j)),
        pl.BlockSpec((None, d, tn), lambda i, j: (layer, 0, nj + j)),
        pl.BlockSpec((None, d, tn), lambda i, j: (layer, 0, 2 * nj + j)),
        pl.BlockSpec((None, 3, tn), lambda i, j: (layer, 0, j)),
        pl.BlockSpec((None, tn, d), lambda i, j: (layer, j, 0)),
    ]
    args = [x, g, w_in, w_in, w_in, cw, w_out]
    if seg is None:
        zl_shape = (nj, 2, tn)
        zl_spec = pl.BlockSpec((nj, 2, tn), lambda i, j: (0, 0, 0))
    else:
        nb = tm // seg
        zl_shape = (t // seg, 2, d)
        zl_spec = pl.BlockSpec((nb, 2, tn), lambda i, j: (i, 0, j))
        in_specs.append(pl.BlockSpec((None, nb, 2, tn), lambda i, j: (layer, i, 0, j)))
        args.append(state)
    nbytes = 2 * (2 * tm * d * 4 + d * 3 * tn * 2 + tn * d * 2) + tm * d * 2 + tm * tn * 4 * 8
    y, zl = pl.pallas_call(
        functools.partial(_conv_body, seg=seg, tn=tn),
        grid=(t // tm, nj),
        in_specs=in_specs,
        out_specs=[pl.BlockSpec((tm, d), lambda i, j: (i, 0)), zl_spec],
        out_shape=[jax.ShapeDtypeStruct((t, d), F32), jax.ShapeDtypeStruct(zl_shape, F32)],
        scratch_shapes=[pltpu.VMEM((tm, d), BF16)],
        compiler_params=pltpu.CompilerParams(
            dimension_semantics=("arbitrary", "arbitrary"), vmem_limit_bytes=_vmem_limit(nbytes)),
        name="conv_mixer",
    )(*args)
    if seg is None:
        zl = zl.transpose(1, 0, 2).reshape(1, 2, d)
    return y, zl


def _kv_body(x_ref, g_ref, wkv_ref, kg_ref, cos_ref, sneg_ref, spos_ref, k_ref, v_ref, ka_ref, va_ref, *, dup):
    xn = _rms(x_ref[...], g_ref[...]).astype(BF16)
    kv = jnp.dot(xn, wkv_ref[...], preferred_element_type=F32)
    nkv = N_KV_HEADS * HEAD_DIM
    lo = _half_mask((xn.shape[0], LANES))
    for p in range(nkv // LANES):
        sl = slice(p * LANES, (p + 1) * LANES)
        kr = _head_norm_rope(kv[:, sl], kg_ref[...], cos_ref[...], sneg_ref[...], spos_ref[...])
        vr = kv[:, nkv + p * LANES: nkv + (p + 1) * LANES]
        k_ref[:, sl] = kr
        v_ref[:, sl] = vr
        if dup:
            for src, dst in ((kr, ka_ref), (vr, va_ref)):
                sw = pltpu.roll(src, HEAD_DIM, 1)
                dst[2 * p] = jnp.where(lo, src, sw).astype(BF16)
                dst[2 * p + 1] = jnp.where(lo, sw, src).astype(BF16)
        else:
            ka_ref[:, sl] = kr.astype(BF16)
            va_ref[sl, :] = vr.T.astype(BF16)


def _shared_kv(x, g, w_kv, kg, rope, *, tm, dup):
    t, d = x.shape
    tm = min(tm, t)
    nkv = N_KV_HEADS * HEAD_DIM
    row = lambda i: (i, 0)
    const = lambda i: (0, 0)
    if dup:
        f32_spec, f32_shape = pl.BlockSpec((tm, nkv), row), jax.ShapeDtypeStruct((t, nkv), F32)
        aux_specs = [pl.BlockSpec((N_KV_HEADS, tm, LANES), lambda i: (0, i, 0))] * 2
        aux_shapes = [jax.ShapeDtypeStruct((N_KV_HEADS, t, LANES), BF16)] * 2
    else:
        f32_spec, f32_shape = pl.BlockSpec((tm, nkv), const), jax.ShapeDtypeStruct((tm, nkv), F32)
        aux_specs = [pl.BlockSpec((tm, nkv), row), pl.BlockSpec((nkv, tm), lambda i: (0, i))]
        aux_shapes = [jax.ShapeDtypeStruct((t, nkv), BF16), jax.ShapeDtypeStruct((nkv, t), BF16)]
    nbytes = 2 * (tm * d * 4 + d * 2 * nkv * 2 + 5 * tm * nkv * 4) + tm * d * 8
    return pl.pallas_call(
        functools.partial(_kv_body, dup=dup),
        grid=(t // tm,),
        in_specs=[
            pl.BlockSpec((tm, d), row),
            pl.BlockSpec((1, d), const),
            pl.BlockSpec((d, 2 * nkv), const),
            pl.BlockSpec((1, LANES), const),
            pl.BlockSpec((tm, LANES), row),
            pl.BlockSpec((tm, LANES), row),
            pl.BlockSpec((tm, LANES), row),
        ],
        out_specs=[f32_spec, f32_spec] + aux_specs,
        out_shape=[f32_shape, f32_shape] + aux_shapes,
        compiler_params=pltpu.CompilerParams(
            dimension_semantics=("arbitrary",), vmem_limit_bytes=_vmem_limit(nbytes)),
        name="shared_kv",
    )(x, g, w_kv, kg, *rope)


def _attend(qcat, kwin, vwin, valid, sink):
    s = lax.dot_general(qcat, kwin, (((1,), (1,)), ((), ())), preferred_element_type=F32)
    s = jnp.where(valid, s, -jnp.inf)
    m = jnp.maximum(jnp.max(s, axis=-1, keepdims=True), sink)
    e = jnp.exp2(s - m)
    den = jnp.sum(e, axis=-1, keepdims=True) + jnp.exp2(sink - m)
    o = jnp.dot(e.astype(BF16), vwin, preferred_element_type=F32)
    return o / den


def _project_q(x_ref, g_ref, wq_ref, qg_ref, cos_ref, sneg_ref, spos_ref, qe_ref, qo_ref):
    xn = _rms(x_ref[...], g_ref[...]).astype(BF16)
    q = jnp.dot(xn, wq_ref[...], preferred_element_type=F32)
    lo = _half_mask((q.shape[0], LANES))
    for p in range(q.shape[1] // LANES):
        sl = slice(p * LANES, (p + 1) * LANES)
        qr = _head_norm_rope(q[:, sl], qg_ref[...], cos_ref[...], sneg_ref[...], spos_ref[...]) * (SCALE * LOG2E)
        qe_ref[:, sl] = jnp.where(lo, qr, 0.0).astype(BF16)
        qo_ref[:, sl] = jnp.where(lo, 0.0, qr).astype(BF16)


def _attend_rows(qe_ref, qo_ref, att_ref, r0, nr, kh, kwin, vwin, valid, sink):
    pairs = GROUP // 2
    cols = [slice((pairs * kh + j) * LANES, (pairs * kh + j + 1) * LANES) for j in range(pairs)]
    qcat = jnp.concatenate([qe_ref[r0:r0 + nr, c] for c in cols] + [qo_ref[r0:r0 + nr, c] for c in cols], axis=0)
    o = _attend(qcat, kwin, vwin, valid, sink)
    lo = _half_mask((nr, LANES))
    for j, c in enumerate(cols):
        att_ref[r0:r0 + nr, c] = jnp.where(lo, o[j * nr:(j + 1) * nr], o[(pairs + j) * nr:(pairs + j + 1) * nr]).astype(BF16)


def _attn_prompt_body(x_ref, g_ref, wqt_ref, qg_ref, cos_ref, sin_ref,
                      kprev_ref, kcur_ref, vprev_ref, vcur_ref, sink_ref, wot_ref,
                      o_ref, qt_ref, att_ref, pt_ref, kw_ref, vw_ref):
    i = pl.program_id(0)
    tm, d = x_ref.shape
    half = ROT_DIM // 2
    x = x_ref[...]
    xn = _rms(x, g_ref[...]).astype(BF16)
    cos, sin = cos_ref[...], sin_ref[...]
    gain = jnp.concatenate([qg_ref[...]] * (tm // LANES), axis=1)
    rows = GROUP * HEAD_DIM

    nblk = tm // LANES
    dk = d // nblk

    def project_q(kh, c):
        return lax.dot_general(wqt_ref[kh * rows:(kh + 1) * rows, c * dk:(c + 1) * dk], xn[:, c * dk:(c + 1) * dk],
                               (((1,), (1,)), ((), ())), preferred_element_type=F32)

    def norm_rope_q(qt, kh, c):
        for j in range(c * (GROUP // nblk), (c + 1) * (GROUP // nblk)):
            t = qt[j * HEAD_DIM:(j + 1) * HEAD_DIM, :]
            tn = t * lax.rsqrt(jnp.sum(t * t, axis=0, keepdims=True) / HEAD_DIM + EPS) * gain
            x1, x2 = tn[:half], tn[half:ROT_DIM]
            h = GROUP * kh + j
            qt_ref[h * HEAD_DIM:(h + 1) * HEAD_DIM, :] = jnp.concatenate(
                [x1 * cos - x2 * sin, x2 * cos + x1 * sin, tn[ROT_DIM:]], axis=0).astype(BF16)


    kw_ref[:WINDOW, :] = kprev_ref[...]
    kw_ref[WINDOW:, :] = kcur_ref[...]
    vw_ref[:, :WINDOW] = vprev_ref[...]
    vw_ref[:, WINDOW:] = vcur_ref[...]

    first_query_chunk = lax.broadcasted_iota(jnp.int32, (CHUNK, LANES), 1) < CHUNK
    has_past = jnp.broadcast_to(i > 0, (CHUNK, LANES))
    zeros = jnp.zeros((HEAD_DIM, LANES), BF16)

    def scores_t(p, kh):
        qs = slice(p * LANES, (p + 1) * LANES)
        kwin = kw_ref[p * LANES:p * LANES + KEYS, (kh // 2) * LANES:(kh // 2 + 1) * LANES]
        rhs = jnp.concatenate(
            [jnp.concatenate([qt_ref[h * HEAD_DIM:(h + 1) * HEAD_DIM, qs], zeros] if kh % 2 == 0 else
                             [zeros, qt_ref[h * HEAD_DIM:(h + 1) * HEAD_DIM, qs]], axis=0)
             for h in range(GROUP * kh, GROUP * (kh + 1))], axis=1)
        return jnp.dot(kwin, rhs, preferred_element_type=F32)

    def softmax_t(st, p, kh, slot):
        masks = [first_query_chunk & has_past if p == 0 else first_query_chunk, has_past if p == 0 else None,
                 None, ~first_query_chunk]
        sink_terms = []
        for j in range(GROUP):
            s = jnp.concatenate(
                [st[c * CHUNK:(c + 1) * CHUNK, j * LANES:(j + 1) * LANES] if mask is None else
                 jnp.where(mask, st[c * CHUNK:(c + 1) * CHUNK, j * LANES:(j + 1) * LANES], -jnp.inf)
                 for c, mask in enumerate(masks)], axis=0)
            sink = sink_ref[GROUP * kh + j:GROUP * kh + j + 1, :]
            m = jnp.maximum(jnp.max(s, axis=0, keepdims=True), sink)
            sink_terms.append(jnp.exp2(sink - m))
            pt_ref[slot, :, j * LANES:(j + 1) * LANES] = jnp.exp2(s - m).astype(BF16)
        return jnp.concatenate(sink_terms, axis=1)

    ones_rows = jnp.ones((16, KEYS), BF16)

    def weighted_values_t(p, kh, slot, sink_term):
        qs = slice(p * LANES, (p + 1) * LANES)
        v_ones = jnp.concatenate([vw_ref[kh * HEAD_DIM:(kh + 1) * HEAD_DIM, p * LANES:p * LANES + KEYS], ones_rows], axis=0)
        ot = jnp.dot(v_ones, pt_ref[slot], preferred_element_type=F32)
        ot = ot[:HEAD_DIM] * (1.0 / (ot[HEAD_DIM:HEAD_DIM + 1] + sink_term))
        for j in range(GROUP):
            h = GROUP * kh + j
            att_ref[h * HEAD_DIM:(h + 1) * HEAD_DIM, qs] = ot[:, j * LANES:(j + 1) * LANES].astype(BF16)

    def full_q(kh):
        qt = project_q(kh, 0)
        for c in range(1, nblk):
            qt = qt + project_q(kh, c)
        return qt

    qts = {0: full_q(0)}
    for c in range(nblk):
        norm_rope_q(qts[0], 0, c)
    qts[1] = full_q(1)
    for kh in range(N_KV_HEADS):
        st = scores_t(0, kh)
        pending = None
        for n in range(nblk):
            st_next = scores_t(n + 1, kh) if n + 1 < nblk else None
            if kh + 2 < N_KV_HEADS:
                piece = project_q(kh + 2, n)
                qts[kh + 2] = piece if n == 0 else qts[kh + 2] + piece
            den = softmax_t(st, n, kh, n % 2)
            if kh + 1 < N_KV_HEADS:
                norm_rope_q(qts[kh + 1], kh + 1, n)
            if pending is not None:
                weighted_values_t(*pending)
            pending = (n, kh, n % 2, den)
            st = st_next
        weighted_values_t(*pending)
    o_ref[...] = x_ref[...] + lax.dot_general(att_ref[...], wot_ref[...], (((0,), (0,)), ((), ())),
                                              preferred_element_type=F32)


def _attn_sample_body(x_ref, g_ref, wq_ref, qg_ref, cos_ref, sneg_ref, spos_ref,
                      kc_ref, knew_ref, vc_ref, vnew_ref, sink_ref, wo_ref,
                      o_ref, qe_ref, qo_ref, att_ref, kw_ref, vw_ref, *, seg):
    tm = x_ref.shape[0]
    _project_q(x_ref, g_ref, wq_ref, qg_ref, cos_ref, sneg_ref, spos_ref, qe_ref, qo_ref)
    nkeys = WINDOW + seg
    valid = lax.broadcasted_iota(jnp.int32, (1, KEYS), 1) < nkeys
    kw_ref[nkeys:, :] = jnp.zeros((KEYS - nkeys, LANES), BF16)
    vw_ref[nkeys:, :] = jnp.zeros((KEYS - nkeys, LANES), BF16)
    for b in range(tm // seg):
        for kh in range(N_KV_HEADS):
            kw_ref[:WINDOW, :] = kc_ref[b, kh]
            kw_ref[WINDOW:nkeys, :] = knew_ref[kh, b * seg:(b + 1) * seg, :]
            vw_ref[:WINDOW, :] = vc_ref[b, kh]
            vw_ref[WINDOW:nkeys, :] = vnew_ref[kh, b * seg:(b + 1) * seg, :]
            _attend_rows(qe_ref, qo_ref, att_ref, b * seg, seg, kh, kw_ref[...], vw_ref[...], valid, sink_ref[kh])
    o_ref[...] = x_ref[...] + jnp.dot(att_ref[...], wo_ref[...], preferred_element_type=F32)


def _attn_mixer_sample(x, g, w_q, qg, rope, k2, v2, cache, sink_col, w_o, layer, blayer, *, tm, seg):
    t, d = x.shape
    tm = min(tm, t)
    nb = tm // seg
    row = lambda i: (i, 0)
    const = lambda i: (0, 0)
    resident = dict(pipeline_mode=pl.Buffered(1))
    cur = pl.BlockSpec((N_KV_HEADS, tm, LANES), lambda i: (0, i, 0))
    cspec = pl.BlockSpec((nb, N_KV_HEADS, WINDOW, LANES), lambda i: (i, 0, 0, 0))
    in_specs = [
        pl.BlockSpec((tm, d), row),
        pl.BlockSpec((None, 1, d), lambda i: (layer, 0, 0)),
        pl.BlockSpec((None, d, d), lambda i: (blayer, 0, 0), **resident),
        pl.BlockSpec((1, LANES), const),
        pl.BlockSpec((tm, LANES), row),
        pl.BlockSpec((tm, LANES), row),
        pl.BlockSpec((tm, LANES), row),
        cspec, cur, cspec, cur,
        pl.BlockSpec((N_KV_HEADS, GROUP * seg, 1), lambda i: (0, 0, 0)),
        pl.BlockSpec((None, d, d), lambda i: (blayer, 0, 0), **resident),
    ]
    nbytes = 2 * d * d * 2 + 4 * tm * d * 4 + tm * d * (4 + 3 * 2) + 8 * tm * LANES * 4 * 2 + 4 * 2 ** 20
    return pl.pallas_call(
        functools.partial(_attn_sample_body, seg=seg),
        grid=(t // tm,),
        in_specs=in_specs,
        out_specs=pl.BlockSpec((tm, d), row),
        out_shape=jax.ShapeDtypeStruct((t, d), F32),
        scratch_shapes=[pltpu.VMEM((tm, d), BF16), pltpu.VMEM((tm, d), BF16), pltpu.VMEM((tm, d), BF16),
                        pltpu.VMEM((KEYS, LANES), BF16), pltpu.VMEM((KEYS, LANES), BF16)],
        compiler_params=pltpu.CompilerParams(
            dimension_semantics=("arbitrary",), vmem_limit_bytes=_vmem_limit(nbytes)),
        name="attn_mixer_sample",
    )(x, g, w_q, qg, *rope, cache[0], k2, cache[1], v2, sink_col, w_o)


def _attn_mixer_prompt(x, g, w_qt, qg, cos_t, sin_t, kb, vt, sink_rows, w_ot, layer, blayer, *, tm):
    t, d = x.shape
    tm = min(tm, t)
    nkv = N_KV_HEADS * HEAD_DIM
    nh = d // HEAD_DIM
    half = ROT_DIM // 2
    prev_blk = lambda i: jnp.maximum(i * (tm // WINDOW) - 1, 0)
    resident = dict(pipeline_mode=pl.Buffered(1))
    in_specs = [
        pl.BlockSpec((tm, d), lambda i: (i, 0)),
        pl.BlockSpec((None, 1, d), lambda i: (layer, 0, 0)),
        pl.BlockSpec((None, d, d), lambda i: (blayer, 0, 0), **resident),
        pl.BlockSpec((HEAD_DIM, LANES), lambda i: (0, 0)),
        pl.BlockSpec((half, tm), lambda i: (0, i)),
        pl.BlockSpec((half, tm), lambda i: (0, i)),
        pl.BlockSpec((WINDOW, nkv), lambda i: (prev_blk(i), 0)),
        pl.BlockSpec((tm, nkv), lambda i: (i, 0)),
        pl.BlockSpec((nkv, WINDOW), lambda i: (0, prev_blk(i))),
        pl.BlockSpec((nkv, tm), lambda i: (0, i)),
        pl.BlockSpec((nh, LANES), lambda i: (0, 0)),
        pl.BlockSpec((None, d, d), lambda i: (blayer, 0, 0), **resident),
    ]
    nbytes = (2 * d * d * 2 + 4 * tm * d * 4 + 3 * tm * d * 4 + 2 * tm * d * 2 + 2 * KEYS * GROUP * LANES * 2
              + KEYS * GROUP * LANES * 4 * 2 + 4 * (WINDOW + tm) * nkv * 2)
    return pl.pallas_call(
        _attn_prompt_body,
        grid=(t // tm,),
        in_specs=in_specs,
        out_specs=pl.BlockSpec((tm, d), lambda i: (i, 0)),
        out_shape=jax.ShapeDtypeStruct((t, d), F32),
        scratch_shapes=[pltpu.VMEM((d, tm), BF16), pltpu.VMEM((d, tm), BF16),
                        pltpu.VMEM((2, KEYS, GROUP * LANES), BF16),
                        pltpu.VMEM((WINDOW + tm, nkv), BF16), pltpu.VMEM((nkv, WINDOW + tm), BF16)],
        compiler_params=pltpu.CompilerParams(
            dimension_semantics=("arbitrary",), vmem_limit_bytes=_vmem_limit(nbytes)),
        name="attn_mixer_prompt",
    )(x, g, w_qt, qg, cos_t, sin_t, kb, kb, vt, vt, sink_rows, w_ot)


def _rope_tables(pos):
    half = ROT_DIM // 2
    inv = ROPE_THETA ** (-jnp.arange(half, dtype=F32) / half)
    ang = pos.astype(F32)[:, None] * inv[None, :]
    cos, sin = jnp.cos(ang), jnp.sin(ang)
    n = pos.shape[0]
    one = jnp.ones((n, HEAD_DIM - ROT_DIM), F32)
    zero = jnp.zeros((n, HEAD_DIM - ROT_DIM), F32)
    zh = jnp.zeros((n, half), F32)
    c = jnp.concatenate([cos, cos, one], axis=1)
    sneg = jnp.concatenate([-sin, zh, zero], axis=1)
    spos = jnp.concatenate([zh, sin, zero], axis=1)
    return tuple(jnp.tile(a, (1, LANES // HEAD_DIM)) for a in (c, sneg, spos))


def _sink_column(sinks_l, rows_per_head):
    s = (sinks_l.astype(F32) * LOG2E).reshape(N_KV_HEADS, GROUP // 2, 2).transpose(0, 2, 1)
    return jnp.repeat(s.reshape(N_KV_HEADS, GROUP), rows_per_head, axis=1)[..., None]


def _dup_heads(t):
    t = t.transpose(0, 2, 1, 3)
    return jnp.concatenate([t, t], axis=-1).astype(BF16)


def _forward(x_prompt, x_sample, state_conv, cache_k, cache_v, mix_norm_g, mlp_norm_g, w_up, w_down,
             conv_w_in, conv_w, conv_w_out, kv_norm_g, w_kv, k_norm_g, w_q, q_norm_g, sinks, w_o,
             *, tm_mlp, tf, tf_cast, tm_conv, tn, tm_attn, tm_attn_s, tm_kv):
    _, s, d = x_prompt.shape
    b, l, _ = x_sample.shape
    n_a = conv_w_in.shape[0]
    depth = w_up.shape[0]
    xp = x_prompt.reshape(s, d)
    xs = x_sample.reshape(b * l, d)

    w_in_b, w_out_b = conv_w_in.astype(BF16), conv_w_out.astype(BF16)
    w_kv_b, w_q_b, w_o_b = w_kv.astype(BF16), w_q.astype(BF16), w_o.astype(BF16)
    w_qt_b, w_ot_b = w_q_b.transpose(0, 2, 1), w_o_b.transpose(0, 2, 1)
    mix_g = mix_norm_g.reshape(depth, 1, d)
    mlp_g = mlp_norm_g.reshape(depth, 1, d)

    half = ROT_DIM // 2
    ang_t = (ROPE_THETA ** (-jnp.arange(half, dtype=F32) / half))[:, None] * jnp.arange(s).astype(F32)[None, :]
    cos_t, sin_t = jnp.cos(ang_t), jnp.sin(ang_t)
    rope_p = _rope_tables(jnp.arange(s))
    rope_s = _rope_tables(jnp.tile(PAST_LEN + jnp.arange(l), b))
    kg = jnp.tile(k_norm_g.astype(F32), LANES // HEAD_DIM).reshape(1, LANES)

    conv_p, conv_s = [], []
    for i in range(depth):
        if i < n_a:
            xp, cp = _conv_mixer(xp, mix_g, w_in_b, conv_w, w_out_b, None, i, tm=tm_conv, tn=tn, seg=None)
            xs, cs = _conv_mixer(xs, mix_g, w_in_b, conv_w, w_out_b, state_conv, i, tm=tm_conv, tn=tn, seg=l)
            conv_p.append(cp)
            conv_s.append(cs)
        else:
            if i == n_a:
                kp, vp, kbp, vtp = _shared_kv(xp, kv_norm_g.reshape(1, d), w_kv_b, kg, rope_p, tm=tm_kv, dup=False)
                ks, vs, k2s, v2s = _shared_kv(xs, kv_norm_g.reshape(1, d), w_kv_b, kg, rope_s, tm=tm_kv, dup=True)
                cache2 = (_dup_heads(cache_k), _dup_heads(cache_v))
            j = i - n_a
            qg = jnp.tile(q_norm_g[j].astype(F32), LANES // HEAD_DIM).reshape(1, LANES)
            qg_t = jnp.broadcast_to((q_norm_g[j].astype(F32) * (SCALE * LOG2E))[:, None], (HEAD_DIM, LANES))
            sink_rows = jnp.broadcast_to((sinks[j].astype(F32) * LOG2E)[:, None], (sinks.shape[1], LANES))
            xp = _attn_mixer_prompt(xp, mix_g, w_qt_b, qg_t, cos_t, sin_t, kbp, vtp, sink_rows, w_o_b, i, j, tm=tm_attn)
            xs = _attn_mixer_sample(xs, mix_g, w_q_b, qg, rope_s, k2s, v2s, cache2, _sink_column(sinks[j], l),
                                    w_o_b, i, j, tm=tm_attn_s, seg=l)
        xs, w_up_b, w_down_b = _mlp_cast(xs, mlp_g, w_up, w_down, i, tf=tf_cast)
        xp = _mlp(xp, mlp_g, w_up_b, w_down_b, i, tm=tm_mlp, tf=tf)

    hd = (N_KV_HEADS, HEAD_DIM)
    ks_new = ks.reshape(b, l, *hd)
    vs_new = vs.reshape(b, l, *hd)
    return (xp.reshape(1, s, d), xs.reshape(b, l, d), jnp.stack(conv_p), jnp.stack(conv_s),
            kp[-WINDOW:].reshape(1, WINDOW, *hd), vp[-WINDOW:].reshape(1, WINDOW, *hd),
            jnp.concatenate([cache_k[:, l:], ks_new], axis=1), jnp.concatenate([cache_v[:, l:], vs_new], axis=1))


def kernel(x_prompt, x_sample, state_conv, cache_k, cache_v, mix_norm_g, mlp_norm_g, w_up, w_down, conv_w_in, conv_w, conv_w_out, kv_norm_g, w_kv, k_norm_g, w_q, q_norm_g, sinks, w_o):
    return _forward(x_prompt, x_sample, state_conv, cache_k, cache_v, mix_norm_g, mlp_norm_g, w_up, w_down,
                    conv_w_in, conv_w, conv_w_out, kv_norm_g, w_kv, k_norm_g, w_q, q_norm_g, sinks, w_o,
                    tm_mlp=512, tf=2048, tf_cast=512, tm_conv=512, tn=512, tm_attn=512, tm_attn_s=256, tm_kv=1024)
```

```python
import functools

import jax
import jax.numpy as jnp
from jax import lax
from jax.experimental import pallas as pl
from jax.experimental.pallas import tpu as pltpu

EPS = 1e-6
CHUNK = 64
WINDOW = 128
HEAD_DIM = 64
N_KV_HEADS = 4
GROUP = 8
ROT_DIM = 16
ROPE_THETA = 500000.0
PAST_LEN = 2048
SCALE = HEAD_DIM ** -0.5
LOG2E = 1.4426950408889634

LANES = 128
KEYS = 2 * WINDOW
KV_ROW_GROUP = 256
CONV_SUBCHUNK = 256
VMEM_LIMIT_CAP = 56 * 2 ** 20

F32 = jnp.float32
BF16 = jnp.bfloat16


def _vmem_limit(nbytes):
    return int(min(VMEM_LIMIT_CAP, max(32 * 2 ** 20, nbytes * 5 // 4 + 4 * 2 ** 20)))


def _rms(x, g):
    return x * lax.rsqrt(jnp.mean(x * x, axis=-1, keepdims=True) + EPS) * g


def _half_mask(shape):
    return lax.broadcasted_iota(jnp.int32, shape, len(shape) - 1) < HEAD_DIM


def _head_norm_rope(t, gain, cos, sneg, spos):
    lo = _half_mask(t.shape)
    sq = t * t
    s_lo = jnp.sum(jnp.where(lo, sq, 0.0), axis=-1, keepdims=True)
    s_hi = jnp.sum(jnp.where(lo, 0.0, sq), axis=-1, keepdims=True)
    inv = jnp.where(lo, lax.rsqrt(s_lo / HEAD_DIM + EPS), lax.rsqrt(s_hi / HEAD_DIM + EPS))
    tn = t * inv * gain
    half = ROT_DIM // 2
    return tn * cos + pltpu.roll(tn, LANES - half, 1) * sneg + pltpu.roll(tn, half, 1) * spos


def _mlp_body(x_ref, g_ref, wu_ref, wd_ref, o_ref, xn_ref):
    def contribution():
        h = jnp.dot(xn_ref[...], wu_ref[...], preferred_element_type=F32)
        h = jnp.square(jnp.maximum(h, 0.0)).astype(BF16)
        return jnp.dot(h, wd_ref[...], preferred_element_type=F32)

    @pl.when(pl.program_id(1) == 0)
    def _():
        xn_ref[...] = _rms(x_ref[...], g_ref[...]).astype(BF16)
        o_ref[...] = x_ref[...] + contribution()

    @pl.when(pl.program_id(1) > 0)
    def _():
        o_ref[...] += contribution()


def _mlp_cast_body(x_ref, g_ref, wu_ref, wd_ref, o_ref, wub_ref, wdb_ref, xn_ref):
    def contribution():
        wu = wu_ref[...].astype(BF16)
        wd = wd_ref[...].astype(BF16)
        wub_ref[...] = wu
        wdb_ref[...] = wd
        h = jnp.dot(xn_ref[...], wu, preferred_element_type=F32)
        h = jnp.square(jnp.maximum(h, 0.0)).astype(BF16)
        return jnp.dot(h, wd, preferred_element_type=F32)

    @pl.when(pl.program_id(0) == 0)
    def _():
        xn_ref[...] = _rms(x_ref[...], g_ref[...]).astype(BF16)
        o_ref[...] = x_ref[...] + contribution()

    @pl.when(pl.program_id(0) > 0)
    def _():
        o_ref[...] += contribution()


def _mlp_cast(x, g, w_up, w_down, layer, *, tf):
    t, d = x.shape
    f = w_up.shape[2]
    tf = min(tf, f)
    nbytes = 2 * t * d * 4 + t * d * 2 + 2 * 2 * d * tf * (4 + 2) + t * tf * 6
    return pl.pallas_call(
        _mlp_cast_body,
        grid=(f // tf,),
        in_specs=[
            pl.BlockSpec((t, d), lambda j: (0, 0), pipeline_mode=pl.Buffered(1)),
            pl.BlockSpec((None, 1, d), lambda j: (layer, 0, 0)),
            pl.BlockSpec((None, d, tf), lambda j: (layer, 0, j)),
            pl.BlockSpec((None, tf, d), lambda j: (layer, j, 0)),
        ],
        out_specs=[
            pl.BlockSpec((t, d), lambda j: (0, 0), pipeline_mode=pl.Buffered(1)),
            pl.BlockSpec((d, tf), lambda j: (0, j)),
            pl.BlockSpec((tf, d), lambda j: (j, 0)),
        ],
        out_shape=[
            jax.ShapeDtypeStruct((t, d), F32),
            jax.ShapeDtypeStruct((d, f), BF16),
            jax.ShapeDtypeStruct((f, d), BF16),
        ],
        scratch_shapes=[pltpu.VMEM((t, d), BF16)],
        compiler_params=pltpu.CompilerParams(
            dimension_semantics=("arbitrary",), vmem_limit_bytes=_vmem_limit(nbytes)),
        name="mlp_cast",
    )(x, g, w_up, w_down)


def _mlp(x, g, w_up, w_down, layer, *, tm, tf):
    t, d = x.shape
    f = w_up.shape[1]
    tm, tf = min(tm, t), min(tf, f)
    nbytes = 2 * (2 * tm * d * 4 + 2 * d * tf * 2) + tm * d * 2 + tm * tf * 6
    return pl.pallas_call(
        _mlp_body,
        grid=(t // tm, f // tf),
        in_specs=[
            pl.BlockSpec((tm, d), lambda i, j: (i, 0)),
            pl.BlockSpec((None, 1, d), lambda i, j: (layer, 0, 0)),
            pl.BlockSpec((d, tf), lambda i, j: (0, j)),
            pl.BlockSpec((tf, d), lambda i, j: (j, 0)),
        ],
        out_specs=pl.BlockSpec((tm, d), lambda i, j: (i, 0)),
        out_shape=jax.ShapeDtypeStruct((t, d), F32),
        scratch_shapes=[pltpu.VMEM((tm, d), BF16)],
        compiler_params=pltpu.CompilerParams(
            dimension_semantics=("arbitrary", "arbitrary"), vmem_limit_bytes=_vmem_limit(nbytes)),
        name="mlp",
    )(x, g, w_up, w_down)


def _conv_body(*refs, seg, tn):
    if seg is None:
        x_ref, g_ref, wb_ref, wc_ref, wu_ref, cw_ref, wout_ref, o_ref, zl_ref, xn_ref = refs
    else:
        x_ref, g_ref, wb_ref, wc_ref, wu_ref, cw_ref, wout_ref, st_ref, o_ref, zl_ref, xn_ref = refs
    i, j = pl.program_id(0), pl.program_id(1)

    if seg is None:
        @pl.when(i == 0)
        def _():
            zl_ref[j] = jnp.zeros((2, tn), F32)

    tm = x_ref.shape[0]
    tc = min(tn, CONV_SUBCHUNK)

    def project(c):
        cs = slice(c * tc, (c + 1) * tc)
        xn = xn_ref[...]
        z = jnp.dot(xn, wc_ref[:, cs], preferred_element_type=F32) * jnp.dot(xn, wu_ref[:, cs], preferred_element_type=F32)
        return jnp.dot(xn, wb_ref[:, cs], preferred_element_type=F32), z

    def gated_conv(gate_b, z, c):
        cs = slice(c * tc, (c + 1) * tc)
        row = lax.broadcasted_iota(jnp.int32, z.shape, 0)
        r1 = pltpu.roll(z, 1, 0)
        r2 = pltpu.roll(z, 2, 0)
        if seg is None:
            prev = zl_ref[j, :, cs]
            p0, p1 = prev[0:1, :], prev[1:2, :]
            zl_ref[j, :, cs] = z[tm - 2:, :]
        else:
            nb = tm // seg
            st = st_ref[:, :, cs]
            p0 = jnp.broadcast_to(st[:, 0:1, :], (nb, seg, tc)).reshape(tm, tc)
            p1 = jnp.broadcast_to(st[:, 1:2, :], (nb, seg, tc)).reshape(tm, tc)
            row = row % seg
            zl_ref[:, :, cs] = z.reshape(nb, seg, tc)[:, seg - 2:, :]
        zp1 = jnp.where(row == 0, p1, r1)
        zp2 = jnp.where(row == 0, p0, jnp.where(row == 1, p1, r2))
        cw = cw_ref[:, cs]
        conv = zp2 * cw[0:1, :] + zp1 * cw[1:2, :] + z * cw[2:3, :]
        return (gate_b * conv).astype(BF16)

    def contribution():
        nc = tn // tc
        out = None
        pending = project(0)
        for c in range(nc):
            nxt = project(c + 1) if c + 1 < nc else None
            y = gated_conv(*pending, c)
            part = jnp.dot(y, wout_ref[c * tc:(c + 1) * tc, :], preferred_element_type=F32)
            out = part if out is None else out + part
            pending = nxt
        return out

    @pl.when(j == 0)
    def _():
        xn_ref[...] = _rms(x_ref[...], g_ref[...]).astype(BF16)
        o_ref[...] = x_ref[...] + contribution()

    @pl.when(j > 0)
    def _():
        o_ref[...] += contribution()


def _conv_mixer(x, g, w_in, cw, w_out, state, layer, *, tm, tn, seg):
    t, d = x.shape
    tm = min(tm, t)
    nj = d // tn
    in_specs = [
        pl.BlockSpec((tm, d), lambda i, j: (i, 0)),
        pl.BlockSpec((None, 1, d), lambda i, j: (layer, 0, 0)),
        pl.BlockSpec((None, d, tn), lambda i, j: (layer, 0, j)),
        pl.BlockSpec((None, d, tn), lambda i, j: (layer, 0, nj + j)),
        pl.BlockSpec((None, d, tn), lambda i, j: (layer, 0, 2 * nj + j)),
        pl.BlockSpec((None, 3, tn), lambda i, j: (layer, 0, j)),
        pl.BlockSpec((None, tn, d), lambda i, j: (layer, j, 0)),
    ]
    args = [x, g, w_in, w_in, w_in, cw, w_out]
    if seg is None:
        zl_shape = (nj, 2, tn)
        zl_spec = pl.BlockSpec((nj, 2, tn), lambda i, j: (0, 0, 0))
    else:
        nb = tm // seg
        zl_shape = (t // seg, 2, d)
        zl_spec = pl.BlockSpec((nb, 2, tn), lambda i, j: (i, 0, j))
        in_specs.append(pl.BlockSpec((None, nb, 2, tn), lambda i, j: (layer, i, 0, j)))
        args.append(state)
    nbytes = 2 * (2 * tm * d * 4 + d * 3 * tn * 2 + tn * d * 2) + tm * d * 2 + tm * tn * 4 * 8
    y, zl = pl.pallas_call(
        functools.partial(_conv_body, seg=seg, tn=tn),
        grid=(t // tm, nj),
        in_specs=in_specs,
        out_specs=[pl.BlockSpec((tm, d), lambda i, j: (i, 0)), zl_spec],
        out_shape=[jax.ShapeDtypeStruct((t, d), F32), jax.ShapeDtypeStruct(zl_shape, F32)],
        scratch_shapes=[pltpu.VMEM((tm, d), BF16)],
        compiler_params=pltpu.CompilerParams(
            dimension_semantics=("arbitrary", "arbitrary"), vmem_limit_bytes=_vmem_limit(nbytes)),
        name="conv_mixer",
    )(*args)
    if seg is None:
        zl = zl.transpose(1, 0, 2).reshape(1, 2, d)
    return y, zl


def _kv_body(x_ref, g_ref, wkv_ref, kg_ref, cos_ref, sneg_ref, spos_ref, k_ref, v_ref, ka_ref, va_ref, *, dup):
    tm = x_ref.shape[0]
    nkv = N_KV_HEADS * HEAD_DIM
    gr = min(tm, KV_ROW_GROUP)
    lo = _half_mask((gr, LANES))

    def normed(r):
        return _rms(x_ref[r * gr:(r + 1) * gr, :], g_ref[...]).astype(BF16)

    xn = normed(0)
    for r in range(tm // gr):
        rs = slice(r * gr, (r + 1) * gr)
        xn_next = normed(r + 1) if (r + 1) * gr < tm else None
        kv = jnp.dot(xn, wkv_ref[...], preferred_element_type=F32)
        for p in range(nkv // LANES):
            sl = slice(p * LANES, (p + 1) * LANES)
            kr = _head_norm_rope(kv[:, sl], kg_ref[...], cos_ref[rs, :], sneg_ref[rs, :], spos_ref[rs, :])
            vr = kv[:, nkv + p * LANES: nkv + (p + 1) * LANES]
            k_ref[rs, sl] = kr
            v_ref[rs, sl] = vr
            if dup:
                for src, dst in ((kr, ka_ref), (vr, va_ref)):
                    sw = pltpu.roll(src, HEAD_DIM, 1)
                    dst[2 * p, rs, :] = jnp.where(lo, src, sw).astype(BF16)
                    dst[2 * p + 1, rs, :] = jnp.where(lo, sw, src).astype(BF16)
            else:
                ka_ref[rs, sl] = kr.astype(BF16)
                va_ref[sl, rs] = vr.T.astype(BF16)
        xn = xn_next


def _shared_kv(x, g, w_kv, kg, rope, *, tm, dup):
    t, d = x.shape
    tm = min(tm, t)
    nkv = N_KV_HEADS * HEAD_DIM
    row = lambda i: (i, 0)
    const = lambda i: (0, 0)
    if dup:
        f32_spec, f32_shape = pl.BlockSpec((tm, nkv), row), jax.ShapeDtypeStruct((t, nkv), F32)
        aux_specs = [pl.BlockSpec((N_KV_HEADS, tm, LANES), lambda i: (0, i, 0))] * 2
        aux_shapes = [jax.ShapeDtypeStruct((N_KV_HEADS, t, LANES), BF16)] * 2
    else:
        f32_spec, f32_shape = pl.BlockSpec((tm, nkv), const), jax.ShapeDtypeStruct((tm, nkv), F32)
        aux_specs = [pl.BlockSpec((tm, nkv), row), pl.BlockSpec((nkv, tm), lambda i: (0, i))]
        aux_shapes = [jax.ShapeDtypeStruct((t, nkv), BF16), jax.ShapeDtypeStruct((nkv, t), BF16)]
    nbytes = 2 * (tm * d * 4 + d * 2 * nkv * 2 + 5 * tm * nkv * 4) + tm * d * 8
    return pl.pallas_call(
        functools.partial(_kv_body, dup=dup),
        grid=(t // tm,),
        in_specs=[
            pl.BlockSpec((tm, d), row),
            pl.BlockSpec((1, d), const),
            pl.BlockSpec((d, 2 * nkv), const),
            pl.BlockSpec((1, LANES), const),
            pl.BlockSpec((tm, LANES), row),
            pl.BlockSpec((tm, LANES), row),
            pl.BlockSpec((tm, LANES), row),
        ],
        out_specs=[f32_spec, f32_spec] + aux_specs,
        out_shape=[f32_shape, f32_shape] + aux_shapes,
        compiler_params=pltpu.CompilerParams(
            dimension_semantics=("arbitrary",), vmem_limit_bytes=_vmem_limit(nbytes)),
        name="shared_kv",
    )(x, g, w_kv, kg, *rope)


def _attend(qcat, kwin, vwin, valid, sink):
    s = lax.dot_general(qcat, kwin, (((1,), (1,)), ((), ())), preferred_element_type=F32)
    s = jnp.where(valid, s, -jnp.inf)
    m = jnp.maximum(jnp.max(s, axis=-1, keepdims=True), sink)
    e = jnp.exp2(s - m)
    den = jnp.sum(e, axis=-1, keepdims=True) + jnp.exp2(sink - m)
    o = jnp.dot(e.astype(BF16), vwin, preferred_element_type=F32)
    return o / den


def _project_q(x_ref, g_ref, wq_ref, qg_ref, cos_ref, sneg_ref, spos_ref, qe_ref, qo_ref):
    xn = _rms(x_ref[...], g_ref[...]).astype(BF16)
    q = jnp.dot(xn, wq_ref[...], preferred_element_type=F32)
    lo = _half_mask((q.shape[0], LANES))
    for p in range(q.shape[1] // LANES):
        sl = slice(p * LANES, (p + 1) * LANES)
        qr = _head_norm_rope(q[:, sl], qg_ref[...], cos_ref[...], sneg_ref[...], spos_ref[...]) * (SCALE * LOG2E)
        qe_ref[:, sl] = jnp.where(lo, qr, 0.0).astype(BF16)
        qo_ref[:, sl] = jnp.where(lo, 0.0, qr).astype(BF16)


def _attend_rows(qe_ref, qo_ref, att_ref, r0, nr, kh, kwin, vwin, valid, sink):
    pairs = GROUP // 2
    cols = [slice((pairs * kh + j) * LANES, (pairs * kh + j + 1) * LANES) for j in range(pairs)]
    qcat = jnp.concatenate([qe_ref[r0:r0 + nr, c] for c in cols] + [qo_ref[r0:r0 + nr, c] for c in cols], axis=0)
    o = _attend(qcat, kwin, vwin, valid, sink)
    lo = _half_mask((nr, LANES))
    for j, c in enumerate(cols):
        att_ref[r0:r0 + nr, c] = jnp.where(lo, o[j * nr:(j + 1) * nr], o[(pairs + j) * nr:(pairs + j + 1) * nr]).astype(BF16)


def _attn_prompt_body(x_ref, g_ref, wqt_ref, qg_ref, cos_ref, sin_ref,
                      kprev_ref, kcur_ref, vprev_ref, vcur_ref, sink_ref, wot_ref,
                      o_ref, qt_ref, att_ref, pt_ref, kw_ref, vw_ref):
    i = pl.program_id(0)
    tm, d = x_ref.shape
    half = ROT_DIM // 2
    x = x_ref[...]
    xn = _rms(x, g_ref[...]).astype(BF16)
    cos, sin = cos_ref[...], sin_ref[...]
    gain = jnp.concatenate([qg_ref[...]] * (tm // LANES), axis=1)
    rows = GROUP * HEAD_DIM

    nblk = tm // LANES
    dk = d // nblk

    def project_q(kh, c):
        return lax.dot_general(wqt_ref[kh * rows:(kh + 1) * rows, c * dk:(c + 1) * dk], xn[:, c * dk:(c + 1) * dk],
                               (((1,), (1,)), ((), ())), preferred_element_type=F32)

    def norm_rope_q(qt, kh, c):
        for j in range(c * (GROUP // nblk), (c + 1) * (GROUP // nblk)):
            t = qt[j * HEAD_DIM:(j + 1) * HEAD_DIM, :]
            tn = t * lax.rsqrt(jnp.sum(t * t, axis=0, keepdims=True) / HEAD_DIM + EPS) * gain
            x1, x2 = tn[:half], tn[half:ROT_DIM]
            h = GROUP * kh + j
            qt_ref[h * HEAD_DIM:(h + 1) * HEAD_DIM, :] = jnp.concatenate(
                [x1 * cos - x2 * sin, x2 * cos + x1 * sin, tn[ROT_DIM:]], axis=0).astype(BF16)


    kw_ref[:WINDOW, :] = kprev_ref[...]
    kw_ref[WINDOW:, :] = kcur_ref[...]
    vw_ref[:, :WINDOW] = vprev_ref[...]
    vw_ref[:, WINDOW:] = vcur_ref[...]

    first_query_chunk = lax.broadcasted_iota(jnp.int32, (CHUNK, LANES), 1) < CHUNK
    has_past = jnp.broadcast_to(i > 0, (CHUNK, LANES))
    zeros = jnp.zeros((HEAD_DIM, LANES), BF16)

    def scores_t(p, kh):
        qs = slice(p * LANES, (p + 1) * LANES)
        kwin = kw_ref[p * LANES:p * LANES + KEYS, (kh // 2) * LANES:(kh // 2 + 1) * LANES]
        rhs = jnp.concatenate(
            [jnp.concatenate([qt_ref[h * HEAD_DIM:(h + 1) * HEAD_DIM, qs], zeros] if kh % 2 == 0 else
                             [zeros, qt_ref[h * HEAD_DIM:(h + 1) * HEAD_DIM, qs]], axis=0)
             for h in range(GROUP * kh, GROUP * (kh + 1))], axis=1)
        return jnp.dot(kwin, rhs, preferred_element_type=F32)

    def softmax_t(st, p, kh, slot):
        masks = [first_query_chunk & has_past if p == 0 else first_query_chunk, has_past if p == 0 else None,
                 None, ~first_query_chunk]
        sink_terms = []
        for j in range(GROUP):
            s = jnp.concatenate(
                [st[c * CHUNK:(c + 1) * CHUNK, j * LANES:(j + 1) * LANES] if mask is None else
                 jnp.where(mask, st[c * CHUNK:(c + 1) * CHUNK, j * LANES:(j + 1) * LANES], -jnp.inf)
                 for c, mask in enumerate(masks)], axis=0)
            sink = sink_ref[GROUP * kh + j:GROUP * kh + j + 1, :]
            m = jnp.maximum(jnp.max(s, axis=0, keepdims=True), sink)
            sink_terms.append(jnp.exp2(sink - m))
            pt_ref[slot, :, j * LANES:(j + 1) * LANES] = jnp.exp2(s - m).astype(BF16)
        return jnp.concatenate(sink_terms, axis=1)

    ones_rows = jnp.ones((16, KEYS), BF16)

    def weighted_values_t(p, kh, slot, sink_term):
        qs = slice(p * LANES, (p + 1) * LANES)
        v_ones = jnp.concatenate([vw_ref[kh * HEAD_DIM:(kh + 1) * HEAD_DIM, p * LANES:p * LANES + KEYS], ones_rows], axis=0)
        ot = jnp.dot(v_ones, pt_ref[slot], preferred_element_type=F32)
        ot = ot[:HEAD_DIM] * (1.0 / (ot[HEAD_DIM:HEAD_DIM + 1] + sink_term))
        for j in range(GROUP):
            h = GROUP * kh + j
            att_ref[h * HEAD_DIM:(h + 1) * HEAD_DIM, qs] = ot[:, j * LANES:(j + 1) * LANES].astype(BF16)

    def full_q(kh):
        qt = project_q(kh, 0)
        for c in range(1, nblk):
            qt = qt + project_q(kh, c)
        return qt

    qts = {0: full_q(0)}
    for c in range(nblk):
        norm_rope_q(qts[0], 0, c)
    qts[1] = full_q(1)
    for kh in range(N_KV_HEADS):
        st = scores_t(0, kh)
        pending = None
        for n in range(nblk):
            st_next = scores_t(n + 1, kh) if n + 1 < nblk else None
            if kh + 2 < N_KV_HEADS:
                piece = project_q(kh + 2, n)
                qts[kh + 2] = piece if n == 0 else qts[kh + 2] + piece
            den = softmax_t(st, n, kh, n % 2)
            if kh + 1 < N_KV_HEADS:
                norm_rope_q(qts[kh + 1], kh + 1, n)
            if pending is not None:
                weighted_values_t(*pending)
            pending = (n, kh, n % 2, den)
            st = st_next
        weighted_values_t(*pending)
    o_ref[...] = x_ref[...] + lax.dot_general(att_ref[...], wot_ref[...], (((0,), (0,)), ((), ())),
                                              preferred_element_type=F32)


def _attn_sample_body(x_ref, g_ref, wq_ref, qg_ref, cos_ref, sneg_ref, spos_ref,
                      kc_ref, knew_ref, vc_ref, vnew_ref, sink_ref, wo_ref,
                      o_ref, qe_ref, qo_ref, att_ref, kw_ref, vw_ref, *, seg):
    tm = x_ref.shape[0]
    _project_q(x_ref, g_ref, wq_ref, qg_ref, cos_ref, sneg_ref, spos_ref, qe_ref, qo_ref)
    nkeys = WINDOW + seg
    valid = lax.broadcasted_iota(jnp.int32, (1, KEYS), 1) < nkeys
    kw_ref[nkeys:, :] = jnp.zeros((KEYS - nkeys, LANES), BF16)
    vw_ref[nkeys:, :] = jnp.zeros((KEYS - nkeys, LANES), BF16)
    for b in range(tm // seg):
        for kh in range(N_KV_HEADS):
            kw_ref[:WINDOW, :] = kc_ref[b, kh]
            kw_ref[WINDOW:nkeys, :] = knew_ref[kh, b * seg:(b + 1) * seg, :]
            vw_ref[:WINDOW, :] = vc_ref[b, kh]
            vw_ref[WINDOW:nkeys, :] = vnew_ref[kh, b * seg:(b + 1) * seg, :]
            _attend_rows(qe_ref, qo_ref, att_ref, b * seg, seg, kh, kw_ref[...], vw_ref[...], valid, sink_ref[kh])
    o_ref[...] = x_ref[...] + jnp.dot(att_ref[...], wo_ref[...], preferred_element_type=F32)


def _attn_mixer_sample(x, g, w_q, qg, rope, k2, v2, cache, sink_col, w_o, layer, blayer, *, tm, seg):
    t, d = x.shape
    tm = min(tm, t)
    nb = tm // seg
    row = lambda i: (i, 0)
    const = lambda i: (0, 0)
    resident = dict(pipeline_mode=pl.Buffered(1))
    cur = pl.BlockSpec((N_KV_HEADS, tm, LANES), lambda i: (0, i, 0))
    cspec = pl.BlockSpec((nb, N_KV_HEADS, WINDOW, LANES), lambda i: (i, 0, 0, 0))
    in_specs = [
        pl.BlockSpec((tm, d), row),
        pl.BlockSpec((None, 1, d), lambda i: (layer, 0, 0)),
        pl.BlockSpec((None, d, d), lambda i: (blayer, 0, 0), **resident),
        pl.BlockSpec((1, LANES), const),
        pl.BlockSpec((tm, LANES), row),
        pl.BlockSpec((tm, LANES), row),
        pl.BlockSpec((tm, LANES), row),
        cspec, cur, cspec, cur,
        pl.BlockSpec((N_KV_HEADS, GROUP * seg, 1), lambda i: (0, 0, 0)),
        pl.BlockSpec((None, d, d), lambda i: (blayer, 0, 0), **resident),
    ]
    nbytes = 2 * d * d * 2 + 4 * tm * d * 4 + tm * d * (4 + 3 * 2) + 8 * tm * LANES * 4 * 2 + 4 * 2 ** 20
    return pl.pallas_call(
        functools.partial(_attn_sample_body, seg=seg),
        grid=(t // tm,),
        in_specs=in_specs,
        out_specs=pl.BlockSpec((tm, d), row),
        out_shape=jax.ShapeDtypeStruct((t, d), F32),
        scratch_shapes=[pltpu.VMEM((tm, d), BF16), pltpu.VMEM((tm, d), BF16), pltpu.VMEM((tm, d), BF16),
                        pltpu.VMEM((KEYS, LANES), BF16), pltpu.VMEM((KEYS, LANES), BF16)],
        compiler_params=pltpu.CompilerParams(
            dimension_semantics=("arbitrary",), vmem_limit_bytes=_vmem_limit(nbytes)),
        name="attn_mixer_sample",
    )(x, g, w_q, qg, *rope, cache[0], k2, cache[1], v2, sink_col, w_o)


def _attn_mixer_prompt(x, g, w_qt, qg, cos_t, sin_t, kb, vt, sink_rows, w_ot, layer, blayer, *, tm):
    t, d = x.shape
    tm = min(tm, t)
    nkv = N_KV_HEADS * HEAD_DIM
    nh = d // HEAD_DIM
    half = ROT_DIM // 2
    prev_blk = lambda i: jnp.maximum(i * (tm // WINDOW) - 1, 0)
    resident = dict(pipeline_mode=pl.Buffered(1))
    in_specs = [
        pl.BlockSpec((tm, d), lambda i: (i, 0)),
        pl.BlockSpec((None, 1, d), lambda i: (layer, 0, 0)),
        pl.BlockSpec((None, d, d), lambda i: (blayer, 0, 0), **resident),
        pl.BlockSpec((HEAD_DIM, LANES), lambda i: (0, 0)),
        pl.BlockSpec((half, tm), lambda i: (0, i)),
        pl.BlockSpec((half, tm), lambda i: (0, i)),
        pl.BlockSpec((WINDOW, nkv), lambda i: (prev_blk(i), 0)),
        pl.BlockSpec((tm, nkv), lambda i: (i, 0)),
        pl.BlockSpec((nkv, WINDOW), lambda i: (0, prev_blk(i))),
        pl.BlockSpec((nkv, tm), lambda i: (0, i)),
        pl.BlockSpec((nh, LANES), lambda i: (0, 0)),
        pl.BlockSpec((None, d, d), lambda i: (blayer, 0, 0), **resident),
    ]
    nbytes = (2 * d * d * 2 + 4 * tm * d * 4 + 3 * tm * d * 4 + 2 * tm * d * 2 + 2 * KEYS * GROUP * LANES * 2
              + KEYS * GROUP * LANES * 4 * 2 + 4 * (WINDOW + tm) * nkv * 2)
    return pl.pallas_call(
        _attn_prompt_body,
        grid=(t // tm,),
        in_specs=in_specs,
        out_specs=pl.BlockSpec((tm, d), lambda i: (i, 0)),
        out_shape=jax.ShapeDtypeStruct((t, d), F32),
        scratch_shapes=[pltpu.VMEM((d, tm), BF16), pltpu.VMEM((d, tm), BF16),
                        pltpu.VMEM((2, KEYS, GROUP * LANES), BF16),
                        pltpu.VMEM((WINDOW + tm, nkv), BF16), pltpu.VMEM((nkv, WINDOW + tm), BF16)],
        compiler_params=pltpu.CompilerParams(
            dimension_semantics=("arbitrary",), vmem_limit_bytes=_vmem_limit(nbytes)),
        name="attn_mixer_prompt",
    )(x, g, w_qt, qg, cos_t, sin_t, kb, kb, vt, vt, sink_rows, w_ot)


def _rope_tables(pos):
    half = ROT_DIM // 2
    inv = ROPE_THETA ** (-jnp.arange(half, dtype=F32) / half)
    ang = pos.astype(F32)[:, None] * inv[None, :]
    cos, sin = jnp.cos(ang), jnp.sin(ang)
    n = pos.shape[0]
    one = jnp.ones((n, HEAD_DIM - ROT_DIM), F32)
    zero = jnp.zeros((n, HEAD_DIM - ROT_DIM), F32)
    zh = jnp.zeros((n, half), F32)
    c = jnp.concatenate([cos, cos, one], axis=1)
    sneg = jnp.concatenate([-sin, zh, zero], axis=1)
    spos = jnp.concatenate([zh, sin, zero], axis=1)
    return tuple(jnp.tile(a, (1, LANES // HEAD_DIM)) for a in (c, sneg, spos))


def _sink_column(sinks_l, rows_per_head):
    s = (sinks_l.astype(F32) * LOG2E).reshape(N_KV_HEADS, GROUP // 2, 2).transpose(0, 2, 1)
    return jnp.repeat(s.reshape(N_KV_HEADS, GROUP), rows_per_head, axis=1)[..., None]


def _dup_heads(t):
    t = t.transpose(0, 2, 1, 3)
    return jnp.concatenate([t, t], axis=-1).astype(BF16)


def _forward(x_prompt, x_sample, state_conv, cache_k, cache_v, mix_norm_g, mlp_norm_g, w_up, w_down,
             conv_w_in, conv_w, conv_w_out, kv_norm_g, w_kv, k_norm_g, w_q, q_norm_g, sinks, w_o,
             *, tm_mlp, tf, tf_cast, tm_conv, tn, tm_attn, tm_attn_s, tm_kv):
    _, s, d = x_prompt.shape
    b, l, _ = x_sample.shape
    n_a = conv_w_in.shape[0]
    depth = w_up.shape[0]
    xp = x_prompt.reshape(s, d)
    xs = x_sample.reshape(b * l, d)

    w_in_b, w_out_b = conv_w_in.astype(BF16), conv_w_out.astype(BF16)
    w_kv_b, w_q_b, w_o_b = w_kv.astype(BF16), w_q.astype(BF16), w_o.astype(BF16)
    w_qt_b, w_ot_b = w_q_b.transpose(0, 2, 1), w_o_b.transpose(0, 2, 1)
    mix_g = mix_norm_g.reshape(depth, 1, d)
    mlp_g = mlp_norm_g.reshape(depth, 1, d)

    half = ROT_DIM // 2
    ang_t = (ROPE_THETA ** (-jnp.arange(half, dtype=F32) / half))[:, None] * jnp.arange(s).astype(F32)[None, :]
    cos_t, sin_t = jnp.cos(ang_t), jnp.sin(ang_t)
    rope_p = _rope_tables(jnp.arange(s))
    rope_s = _rope_tables(jnp.tile(PAST_LEN + jnp.arange(l), b))
    kg = jnp.tile(k_norm_g.astype(F32), LANES // HEAD_DIM).reshape(1, LANES)

    conv_p, conv_s = [], []
    for i in range(depth):
        if i < n_a:
            xp, cp = _conv_mixer(xp, mix_g, w_in_b, conv_w, w_out_b, None, i, tm=tm_conv, tn=tn, seg=None)
            xs, cs = _conv_mixer(xs, mix_g, w_in_b, conv_w, w_out_b, state_conv, i, tm=tm_conv, tn=tn, seg=l)
            conv_p.append(cp)
            conv_s.append(cs)
        else:
            if i == n_a:
                kp, vp, kbp, vtp = _shared_kv(xp, kv_norm_g.reshape(1, d), w_kv_b, kg, rope_p, tm=tm_kv, dup=False)
                ks, vs, k2s, v2s = _shared_kv(xs, kv_norm_g.reshape(1, d), w_kv_b, kg, rope_s, tm=tm_kv, dup=True)
                cache2 = (_dup_heads(cache_k), _dup_heads(cache_v))
            j = i - n_a
            qg = jnp.tile(q_norm_g[j].astype(F32), LANES // HEAD_DIM).reshape(1, LANES)
            qg_t = jnp.broadcast_to((q_norm_g[j].astype(F32) * (SCALE * LOG2E))[:, None], (HEAD_DIM, LANES))
            sink_rows = jnp.broadcast_to((sinks[j].astype(F32) * LOG2E)[:, None], (sinks.shape[1], LANES))
            xp = _attn_mixer_prompt(xp, mix_g, w_qt_b, qg_t, cos_t, sin_t, kbp, vtp, sink_rows, w_o_b, i, j, tm=tm_attn)
            xs = _attn_mixer_sample(xs, mix_g, w_q_b, qg, rope_s, k2s, v2s, cache2, _sink_column(sinks[j], l),
                                    w_o_b, i, j, tm=tm_attn_s, seg=l)
        xs, w_up_b, w_down_b = _mlp_cast(xs, mlp_g, w_up, w_down, i, tf=tf_cast)
        xp = _mlp(xp, mlp_g, w_up_b, w_down_b, i, tm=tm_mlp, tf=tf)

    hd = (N_KV_HEADS, HEAD_DIM)
    ks_new = ks.reshape(b, l, *hd)
    vs_new = vs.reshape(b, l, *hd)
    return (xp.reshape(1, s, d), xs.reshape(b, l, d), jnp.stack(conv_p), jnp.stack(conv_s),
            kp[-WINDOW:].reshape(1, WINDOW, *hd), vp[-WINDOW:].reshape(1, WINDOW, *hd),
            jnp.concatenate([cache_k[:, l:], ks_new], axis=1), jnp.concatenate([cache_v[:, l:], vs_new], axis=1))


def kernel(x_prompt, x_sample, state_conv, cache_k, cache_v, mix_norm_g, mlp_norm_g, w_up, w_down, conv_w_in, conv_w, conv_w_out, kv_norm_g, w_kv, k_norm_g, w_q, q_norm_g, sinks, w_o):
    return _forward(x_prompt, x_sample, state_conv, cache_k, cache_v, mix_norm_g, mlp_norm_g, w_up, w_down,
                    conv_w_in, conv_w, conv_w_out, kv_norm_g, w_kv, k_norm_g, w_q, q_norm_g, sinks, w_o,
                    tm_mlp=512, tf=2048, tf_cast=512, tm_conv=512, tn=512, tm_attn=512, tm_attn_s=256, tm_kv=1024)
```

```python
import functools

import jax
import jax.numpy as jnp
from jax import lax
from jax.experimental import pallas as pl
from jax.experimental.pallas import tpu as pltpu

EPS = 1e-6
CHUNK = 64
WINDOW = 128
HEAD_DIM = 64
N_KV_HEADS = 4
GROUP = 8
ROT_DIM = 16
ROPE_THETA = 500000.0
PAST_LEN = 2048
SCALE = HEAD_DIM ** -0.5
LOG2E = 1.4426950408889634

LANES = 128
KEYS = 2 * WINDOW
KV_ROW_GROUP = 256
CONV_SUBCHUNK = 256
VMEM_LIMIT_CAP = 56 * 2 ** 20

F32 = jnp.float32
BF16 = jnp.bfloat16


def _vmem_limit(nbytes):
    return int(min(VMEM_LIMIT_CAP, max(32 * 2 ** 20, nbytes * 5 // 4 + 4 * 2 ** 20)))


def _rms(x, g):
    return x * lax.rsqrt(jnp.mean(x * x, axis=-1, keepdims=True) + EPS) * g


def _half_mask(shape):
    return lax.broadcasted_iota(jnp.int32, shape, len(shape) - 1) < HEAD_DIM


def _head_norm_rope(t, gain, cos, sneg, spos):
    lo = _half_mask(t.shape)
    sq = t * t
    s_lo = jnp.sum(jnp.where(lo, sq, 0.0), axis=-1, keepdims=True)
    s_hi = jnp.sum(jnp.where(lo, 0.0, sq), axis=-1, keepdims=True)
    inv = jnp.where(lo, lax.rsqrt(s_lo / HEAD_DIM + EPS), lax.rsqrt(s_hi / HEAD_DIM + EPS))
    tn = t * inv * gain
    half = ROT_DIM // 2
    return tn * cos + pltpu.roll(tn, LANES - half, 1) * sneg + pltpu.roll(tn, half, 1) * spos


def _mlp_body(x_ref, g_ref, wu_ref, wd_ref, o_ref, xn_ref):
    def contribution():
        h = jnp.dot(xn_ref[...], wu_ref[...], preferred_element_type=F32)
        h = jnp.square(jnp.maximum(h, 0.0)).astype(BF16)
        return jnp.dot(h, wd_ref[...], preferred_element_type=F32)

    @pl.when(pl.program_id(1) == 0)
    def _():
        xn_ref[...] = _rms(x_ref[...], g_ref[...]).astype(BF16)
        o_ref[...] = x_ref[...] + contribution()

    @pl.when(pl.program_id(1) > 0)
    def _():
        o_ref[...] += contribution()


def _mlp_cast_body(x_ref, g_ref, wu_ref, wd_ref, o_ref, wub_ref, wdb_ref, xn_ref):
    def contribution():
        wu = wu_ref[...].astype(BF16)
        wd = wd_ref[...].astype(BF16)
        wub_ref[...] = wu
        wdb_ref[...] = wd
        h = jnp.dot(xn_ref[...], wu, preferred_element_type=F32)
        h = jnp.square(jnp.maximum(h, 0.0)).astype(BF16)
        return jnp.dot(h, wd, preferred_element_type=F32)

    @pl.when(pl.program_id(0) == 0)
    def _():
        xn_ref[...] = _rms(x_ref[...], g_ref[...]).astype(BF16)
        o_ref[...] = x_ref[...] + contribution()

    @pl.when(pl.program_id(0) > 0)
    def _():
        o_ref[...] += contribution()


def _mlp_cast(x, g, w_up, w_down, layer, *, tf):
    t, d = x.shape
    f = w_up.shape[2]
    tf = min(tf, f)
    nbytes = 2 * t * d * 4 + t * d * 2 + 2 * 2 * d * tf * (4 + 2) + t * tf * 6
    return pl.pallas_call(
        _mlp_cast_body,
        grid=(f // tf,),
        in_specs=[
            pl.BlockSpec((t, d), lambda j: (0, 0), pipeline_mode=pl.Buffered(1)),
            pl.BlockSpec((None, 1, d), lambda j: (layer, 0, 0)),
            pl.BlockSpec((None, d, tf), lambda j: (layer, 0, j)),
            pl.BlockSpec((None, tf, d), lambda j: (layer, j, 0)),
        ],
        out_specs=[
            pl.BlockSpec((t, d), lambda j: (0, 0), pipeline_mode=pl.Buffered(1)),
            pl.BlockSpec((d, tf), lambda j: (0, j)),
            pl.BlockSpec((tf, d), lambda j: (j, 0)),
        ],
        out_shape=[
            jax.ShapeDtypeStruct((t, d), F32),
            jax.ShapeDtypeStruct((d, f), BF16),
            jax.ShapeDtypeStruct((f, d), BF16),
        ],
        scratch_shapes=[pltpu.VMEM((t, d), BF16)],
        compiler_params=pltpu.CompilerParams(
            dimension_semantics=("arbitrary",), vmem_limit_bytes=_vmem_limit(nbytes)),
        name="mlp_cast",
    )(x, g, w_up, w_down)


def _mlp(x, g, w_up, w_down, layer, *, tm, tf):
    t, d = x.shape
    f = w_up.shape[1]
    tm, tf = min(tm, t), min(tf, f)
    nbytes = 2 * (2 * tm * d * 4 + 2 * d * tf * 2) + tm * d * 2 + tm * tf * 6
    return pl.pallas_call(
        _mlp_body,
        grid=(t // tm, f // tf),
        in_specs=[
            pl.BlockSpec((tm, d), lambda i, j: (i, 0)),
            pl.BlockSpec((None, 1, d), lambda i, j: (layer, 0, 0)),
            pl.BlockSpec((d, tf), lambda i, j: (0, j)),
            pl.BlockSpec((tf, d), lambda i, j: (j, 0)),
        ],
        out_specs=pl.BlockSpec((tm, d), lambda i, j: (i, 0)),
        out_shape=jax.ShapeDtypeStruct((t, d), F32),
        scratch_shapes=[pltpu.VMEM((tm, d), BF16)],
        compiler_params=pltpu.CompilerParams(
            dimension_semantics=("arbitrary", "arbitrary"), vmem_limit_bytes=_vmem_limit(nbytes)),
        name="mlp",
    )(x, g, w_up, w_down)


def _conv_body(*refs, seg, tn, cast):
    if seg is None:
        x_ref, g_ref, wb_ref, wc_ref, wu_ref, cw_ref, wout_ref, o_ref, zl_ref, *rest = refs
    else:
        x_ref, g_ref, wb_ref, wc_ref, wu_ref, cw_ref, wout_ref, st_ref, o_ref, zl_ref, *rest = refs
    xn_ref = rest[-1]
    i, j = pl.program_id(0), pl.program_id(1)

    if seg is None:
        @pl.when(i == 0)
        def _():
            zl_ref[j] = jnp.zeros((2, tn), F32)

    tm = x_ref.shape[0]
    tc = min(tn, CONV_SUBCHUNK)

    def weights():
        if not cast:
            return wb_ref, wc_ref, wu_ref, wout_ref
        ws = [r[...].astype(BF16) for r in (wb_ref, wc_ref, wu_ref, wout_ref)]
        for dst, w in zip(rest[:4], ws):
            dst[...] = w
        return ws

    def project(c, wb, wc, wu):
        cs = slice(c * tc, (c + 1) * tc)
        xn = xn_ref[...]
        z = jnp.dot(xn, wc[:, cs], preferred_element_type=F32) * jnp.dot(xn, wu[:, cs], preferred_element_type=F32)
        return jnp.dot(xn, wb[:, cs], preferred_element_type=F32), z

    def gated_conv(gate_b, z, c):
        cs = slice(c * tc, (c + 1) * tc)
        row = lax.broadcasted_iota(jnp.int32, z.shape, 0)
        r1 = pltpu.roll(z, 1, 0)
        r2 = pltpu.roll(z, 2, 0)
        if seg is None:
            prev = zl_ref[j, :, cs]
            p0, p1 = prev[0:1, :], prev[1:2, :]
            zl_ref[j, :, cs] = z[tm - 2:, :]
        else:
            nb = tm // seg
            st = st_ref[:, :, cs]
            p0 = jnp.broadcast_to(st[:, 0:1, :], (nb, seg, tc)).reshape(tm, tc)
            p1 = jnp.broadcast_to(st[:, 1:2, :], (nb, seg, tc)).reshape(tm, tc)
            row = row % seg
            zl_ref[:, :, cs] = z.reshape(nb, seg, tc)[:, seg - 2:, :]
        zp1 = jnp.where(row == 0, p1, r1)
        zp2 = jnp.where(row == 0, p0, jnp.where(row == 1, p1, r2))
        cw = cw_ref[:, cs]
        conv = zp2 * cw[0:1, :] + zp1 * cw[1:2, :] + z * cw[2:3, :]
        return (gate_b * conv).astype(BF16)

    def contribution():
        nc = tn // tc
        wb, wc, wu, wout = weights()
        out = None
        pending = project(0, wb, wc, wu)
        for c in range(nc):
            nxt = project(c + 1, wb, wc, wu) if c + 1 < nc else None
            y = gated_conv(*pending, c)
            part = jnp.dot(y, wout[c * tc:(c + 1) * tc, :], preferred_element_type=F32)
            out = part if out is None else out + part
            pending = nxt
        return out

    @pl.when(j == 0)
    def _():
        xn_ref[...] = _rms(x_ref[...], g_ref[...]).astype(BF16)
        o_ref[...] = x_ref[...] + contribution()

    @pl.when(j > 0)
    def _():
        o_ref[...] += contribution()


def _conv_mixer(x, g, w_in, cw, w_out, state, layer, *, tm, tn, seg):
    t, d = x.shape
    cast = w_in is not None
    tm = t if cast else min(tm, t)
    nj = d // tn
    once = dict(pipeline_mode=pl.Buffered(1)) if cast else {}
    in_specs = [
        pl.BlockSpec((tm, d), lambda i, j: (i, 0), **once),
        pl.BlockSpec((None, 1, d), lambda i, j: (layer, 0, 0)),
    ]
    if cast:
        in_specs += [
            pl.BlockSpec((None, d, tn), lambda i, j: (layer, 0, j)),
            pl.BlockSpec((None, d, tn), lambda i, j: (layer, 0, nj + j)),
            pl.BlockSpec((None, d, tn), lambda i, j: (layer, 0, 2 * nj + j)),
            pl.BlockSpec((None, 3, tn), lambda i, j: (layer, 0, j)),
            pl.BlockSpec((None, tn, d), lambda i, j: (layer, j, 0)),
        ]
        args = [x, g, w_in, w_in, w_in, cw, w_out]
    else:
        in_specs += [
            pl.BlockSpec((d, tn), lambda i, j: (0, j)),
            pl.BlockSpec((d, tn), lambda i, j: (0, j)),
            pl.BlockSpec((d, tn), lambda i, j: (0, j)),
            pl.BlockSpec((None, 3, tn), lambda i, j: (layer, 0, j)),
            pl.BlockSpec((tn, d), lambda i, j: (j, 0)),
        ]
        args = [x, g, *w_out[:3], cw, w_out[3]]
    if seg is None:
        zl_shape = (nj, 2, tn)
        zl_spec = pl.BlockSpec((nj, 2, tn), lambda i, j: (0, 0, 0))
    else:
        nb = tm // seg
        zl_shape = (t // seg, 2, d)
        zl_spec = pl.BlockSpec((nb, 2, tn), lambda i, j: (i, 0, j))
        in_specs.append(pl.BlockSpec((None, nb, 2, tn), lambda i, j: (layer, i, 0, j)))
        args.append(state)
    out_specs = [pl.BlockSpec((tm, d), lambda i, j: (i, 0), **once), zl_spec]
    out_shape = [jax.ShapeDtypeStruct((t, d), F32), jax.ShapeDtypeStruct(zl_shape, F32)]
    if cast:
        out_specs += [pl.BlockSpec((d, tn), lambda i, j: (0, j))] * 3 + [pl.BlockSpec((tn, d), lambda i, j: (j, 0))]
        out_shape += [jax.ShapeDtypeStruct((d, d), BF16)] * 4
        nbytes = 2 * tm * d * 4 + 2 * 4 * d * tn * (4 + 2) + tm * d * 2 + tm * tn * 4 * 8
    else:
        nbytes = 2 * (2 * tm * d * 4 + d * 3 * tn * 2 + tn * d * 2) + tm * d * 2 + tm * tn * 4 * 8
    y, zl, *w_b = pl.pallas_call(
        functools.partial(_conv_body, seg=seg, tn=tn, cast=cast),
        grid=(t // tm, nj),
        in_specs=in_specs,
        out_specs=out_specs,
        out_shape=out_shape,
        scratch_shapes=[pltpu.VMEM((tm, d), BF16)],
        compiler_params=pltpu.CompilerParams(
            dimension_semantics=("arbitrary", "arbitrary"), vmem_limit_bytes=_vmem_limit(nbytes)),
        name="conv_mixer_cast" if cast else "conv_mixer",
    )(*args)
    if seg is None:
        zl = zl.transpose(1, 0, 2).reshape(1, 2, d)
    return (y, zl, tuple(w_b)) if cast else (y, zl)


def _kv_body(x_ref, g_ref, wkv_ref, kg_ref, cos_ref, sneg_ref, spos_ref, k_ref, v_ref, ka_ref, va_ref, *, dup):
    tm = x_ref.shape[0]
    nkv = N_KV_HEADS * HEAD_DIM
    gr = min(tm, KV_ROW_GROUP)
    lo = _half_mask((gr, LANES))

    def normed(r):
        return _rms(x_ref[r * gr:(r + 1) * gr, :], g_ref[...]).astype(BF16)

    xn = normed(0)
    for r in range(tm // gr):
        rs = slice(r * gr, (r + 1) * gr)
        xn_next = normed(r + 1) if (r + 1) * gr < tm else None
        kv = jnp.dot(xn, wkv_ref[...], preferred_element_type=F32)
        for p in range(nkv // LANES):
            sl = slice(p * LANES, (p + 1) * LANES)
            kr = _head_norm_rope(kv[:, sl], kg_ref[...], cos_ref[rs, :], sneg_ref[rs, :], spos_ref[rs, :])
            vr = kv[:, nkv + p * LANES: nkv + (p + 1) * LANES]
            k_ref[rs, sl] = kr
            v_ref[rs, sl] = vr
            if dup:
                for src, dst in ((kr, ka_ref), (vr, va_ref)):
                    sw = pltpu.roll(src, HEAD_DIM, 1)
                    dst[2 * p, rs, :] = jnp.where(lo, src, sw).astype(BF16)
                    dst[2 * p + 1, rs, :] = jnp.where(lo, sw, src).astype(BF16)
            else:
                ka_ref[rs, sl] = kr.astype(BF16)
                va_ref[sl, rs] = vr.T.astype(BF16)
        xn = xn_next


def _shared_kv(x, g, w_kv, kg, rope, *, tm, dup):
    t, d = x.shape
    tm = min(tm, t)
    nkv = N_KV_HEADS * HEAD_DIM
    row = lambda i: (i, 0)
    const = lambda i: (0, 0)
    if dup:
        f32_spec, f32_shape = pl.BlockSpec((tm, nkv), row), jax.ShapeDtypeStruct((t, nkv), F32)
        aux_specs = [pl.BlockSpec((N_KV_HEADS, tm, LANES), lambda i: (0, i, 0))] * 2
        aux_shapes = [jax.ShapeDtypeStruct((N_KV_HEADS, t, LANES), BF16)] * 2
    else:
        f32_spec, f32_shape = pl.BlockSpec((tm, nkv), const), jax.ShapeDtypeStruct((tm, nkv), F32)
        aux_specs = [pl.BlockSpec((tm, nkv), row), pl.BlockSpec((nkv, tm), lambda i: (0, i))]
        aux_shapes = [jax.ShapeDtypeStruct((t, nkv), BF16), jax.ShapeDtypeStruct((nkv, t), BF16)]
    nbytes = 2 * (tm * d * 4 + d * 2 * nkv * 2 + 5 * tm * nkv * 4) + tm * d * 8
    return pl.pallas_call(
        functools.partial(_kv_body, dup=dup),
        grid=(t // tm,),
        in_specs=[
            pl.BlockSpec((tm, d), row),
            pl.BlockSpec((1, d), const),
            pl.BlockSpec((d, 2 * nkv), const),
            pl.BlockSpec((1, LANES), const),
            pl.BlockSpec((tm, LANES), row),
            pl.BlockSpec((tm, LANES), row),
            pl.BlockSpec((tm, LANES), row),
        ],
        out_specs=[f32_spec, f32_spec] + aux_specs,
        out_shape=[f32_shape, f32_shape] + aux_shapes,
        compiler_params=pltpu.CompilerParams(
            dimension_semantics=("arbitrary",), vmem_limit_bytes=_vmem_limit(nbytes)),
        name="shared_kv",
    )(x, g, w_kv, kg, *rope)


def _attend(qcat, kwin, vwin, valid, sink):
    s = lax.dot_general(qcat, kwin, (((1,), (1,)), ((), ())), preferred_element_type=F32)
    s = jnp.where(valid, s, -jnp.inf)
    m = jnp.maximum(jnp.max(s, axis=-1, keepdims=True), sink)
    e = jnp.exp2(s - m)
    den = jnp.sum(e, axis=-1, keepdims=True) + jnp.exp2(sink - m)
    o = jnp.dot(e.astype(BF16), vwin, preferred_element_type=F32)
    return o / den


def _project_q(x_ref, g_ref, wq_ref, qg_ref, cos_ref, sneg_ref, spos_ref, qe_ref, qo_ref):
    xn = _rms(x_ref[...], g_ref[...]).astype(BF16)
    q = jnp.dot(xn, wq_ref[...], preferred_element_type=F32)
    lo = _half_mask((q.shape[0], LANES))
    for p in range(q.shape[1] // LANES):
        sl = slice(p * LANES, (p + 1) * LANES)
        qr = _head_norm_rope(q[:, sl], qg_ref[...], cos_ref[...], sneg_ref[...], spos_ref[...]) * (SCALE * LOG2E)
        qe_ref[:, sl] = jnp.where(lo, qr, 0.0).astype(BF16)
        qo_ref[:, sl] = jnp.where(lo, 0.0, qr).astype(BF16)


def _attend_rows(qe_ref, qo_ref, att_ref, r0, nr, kh, kwin, vwin, valid, sink):
    pairs = GROUP // 2
    cols = [slice((pairs * kh + j) * LANES, (pairs * kh + j + 1) * LANES) for j in range(pairs)]
    qcat = jnp.concatenate([qe_ref[r0:r0 + nr, c] for c in cols] + [qo_ref[r0:r0 + nr, c] for c in cols], axis=0)
    o = _attend(qcat, kwin, vwin, valid, sink)
    lo = _half_mask((nr, LANES))
    for j, c in enumerate(cols):
        att_ref[r0:r0 + nr, c] = jnp.where(lo, o[j * nr:(j + 1) * nr], o[(pairs + j) * nr:(pairs + j + 1) * nr]).astype(BF16)


def _attn_prompt_body(x_ref, g_ref, wqt_ref, qg_ref, cos_ref, sin_ref,
                      kprev_ref, kcur_ref, vprev_ref, vcur_ref, sink_ref, wot_ref,
                      o_ref, qt_ref, att_ref, kw_ref, vw_ref):
    i = pl.program_id(0)
    tm, d = x_ref.shape
    half = ROT_DIM // 2
    x = x_ref[...]
    xn = _rms(x, g_ref[...]).astype(BF16)
    cos, sin = cos_ref[...], sin_ref[...]
    gain = jnp.concatenate([qg_ref[...]] * (tm // LANES), axis=1)
    rows = GROUP * HEAD_DIM

    nblk = tm // LANES
    dk = d // nblk

    def project_q(kh, c):
        return lax.dot_general(wqt_ref[kh * rows:(kh + 1) * rows, c * dk:(c + 1) * dk], xn[:, c * dk:(c + 1) * dk],
                               (((1,), (1,)), ((), ())), preferred_element_type=F32)

    def norm_rope_q(qt, kh, c):
        for j in range(c * (GROUP // nblk), (c + 1) * (GROUP // nblk)):
            t = qt[j * HEAD_DIM:(j + 1) * HEAD_DIM, :]
            tn = t * lax.rsqrt(jnp.sum(t * t, axis=0, keepdims=True) / HEAD_DIM + EPS) * gain
            x1, x2 = tn[:half], tn[half:ROT_DIM]
            h = GROUP * kh + j
            qt_ref[h * HEAD_DIM:(h + 1) * HEAD_DIM, :] = jnp.concatenate(
                [x1 * cos - x2 * sin, x2 * cos + x1 * sin, tn[ROT_DIM:]], axis=0).astype(BF16)


    kw_ref[:WINDOW, :] = kprev_ref[...]
    kw_ref[WINDOW:, :] = kcur_ref[...]
    vw_ref[:, :WINDOW] = vprev_ref[...]
    vw_ref[:, WINDOW:] = vcur_ref[...]

    first_query_chunk = lax.broadcasted_iota(jnp.int32, (CHUNK, LANES), 1) < CHUNK
    has_past = jnp.broadcast_to(i > 0, (CHUNK, LANES))
    zeros = jnp.zeros((HEAD_DIM, LANES), BF16)

    def scores_t(p, kh):
        qs = slice(p * LANES, (p + 1) * LANES)
        kwin = kw_ref[p * LANES:p * LANES + KEYS, (kh // 2) * LANES:(kh // 2 + 1) * LANES]
        rhs = jnp.concatenate(
            [jnp.concatenate([qt_ref[h * HEAD_DIM:(h + 1) * HEAD_DIM, qs], zeros] if kh % 2 == 0 else
                             [zeros, qt_ref[h * HEAD_DIM:(h + 1) * HEAD_DIM, qs]], axis=0)
             for h in range(GROUP * kh, GROUP * (kh + 1))], axis=1)
        return jnp.dot(kwin, rhs, preferred_element_type=F32)

    def softmax_t(st, p, kh):
        masks = [first_query_chunk & has_past if p == 0 else first_query_chunk, has_past if p == 0 else None,
                 None, ~first_query_chunk]
        sink_terms, pts = [], []
        for j in range(GROUP):
            s = jnp.concatenate(
                [st[c * CHUNK:(c + 1) * CHUNK, j * LANES:(j + 1) * LANES] if mask is None else
                 jnp.where(mask, st[c * CHUNK:(c + 1) * CHUNK, j * LANES:(j + 1) * LANES], -jnp.inf)
                 for c, mask in enumerate(masks)], axis=0)
            sink = sink_ref[GROUP * kh + j:GROUP * kh + j + 1, :]
            m = jnp.maximum(jnp.max(s, axis=0, keepdims=True), sink)
            sink_terms.append(jnp.exp2(sink - m))
            pts.append(jnp.exp2(s - m).astype(BF16))
        return jnp.concatenate(pts, axis=1), jnp.concatenate(sink_terms, axis=1)

    ones_rows = jnp.ones((16, KEYS), BF16)

    def weighted_values_t(p, kh, pt, sink_term):
        qs = slice(p * LANES, (p + 1) * LANES)
        v_ones = jnp.concatenate([vw_ref[kh * HEAD_DIM:(kh + 1) * HEAD_DIM, p * LANES:p * LANES + KEYS], ones_rows], axis=0)
        ot = jnp.dot(v_ones, pt, preferred_element_type=F32)
        ot = ot[:HEAD_DIM] * (1.0 / (ot[HEAD_DIM:HEAD_DIM + 1] + sink_term))
        for j in range(GROUP):
            h = GROUP * kh + j
            att_ref[h * HEAD_DIM:(h + 1) * HEAD_DIM, qs] = ot[:, j * LANES:(j + 1) * LANES].astype(BF16)

    def full_q(kh):
        qt = project_q(kh, 0)
        for c in range(1, nblk):
            qt = qt + project_q(kh, c)
        return qt

    qts = {0: full_q(0)}
    for c in range(nblk):
        norm_rope_q(qts[0], 0, c)
    qts[1] = full_q(1)
    for kh in range(N_KV_HEADS):
        st = scores_t(0, kh)
        pending = None
        for n in range(nblk):
            st_next = scores_t(n + 1, kh) if n + 1 < nblk else None
            if kh + 2 < N_KV_HEADS:
                piece = project_q(kh + 2, n)
                qts[kh + 2] = piece if n == 0 else qts[kh + 2] + piece
            pt, sink_term = softmax_t(st, n, kh)
            if kh + 1 < N_KV_HEADS:
                norm_rope_q(qts[kh + 1], kh + 1, n)
            if pending is not None:
                weighted_values_t(*pending)
            pending = (n, kh, pt, sink_term)
            st = st_next
        weighted_values_t(*pending)
    o_ref[...] = x_ref[...] + lax.dot_general(att_ref[...], wot_ref[...], (((0,), (0,)), ((), ())),
                                              preferred_element_type=F32)


def _attn_sample_body(x_ref, g_ref, wq_ref, qg_ref, cos_ref, sneg_ref, spos_ref,
                      kc_ref, knew_ref, vc_ref, vnew_ref, sink_ref, wo_ref,
                      o_ref, qe_ref, qo_ref, att_ref, kw_ref, vw_ref, *, seg):
    tm = x_ref.shape[0]
    _project_q(x_ref, g_ref, wq_ref, qg_ref, cos_ref, sneg_ref, spos_ref, qe_ref, qo_ref)
    nkeys = WINDOW + seg
    valid = lax.broadcasted_iota(jnp.int32, (1, KEYS), 1) < nkeys
    kw_ref[nkeys:, :] = jnp.zeros((KEYS - nkeys, LANES), BF16)
    vw_ref[nkeys:, :] = jnp.zeros((KEYS - nkeys, LANES), BF16)
    for b in range(tm // seg):
        for kh in range(N_KV_HEADS):
            kw_ref[:WINDOW, :] = kc_ref[b, kh]
            kw_ref[WINDOW:nkeys, :] = knew_ref[kh, b * seg:(b + 1) * seg, :]
            vw_ref[:WINDOW, :] = vc_ref[b, kh]
            vw_ref[WINDOW:nkeys, :] = vnew_ref[kh, b * seg:(b + 1) * seg, :]
            _attend_rows(qe_ref, qo_ref, att_ref, b * seg, seg, kh, kw_ref[...], vw_ref[...], valid, sink_ref[kh])
    o_ref[...] = x_ref[...] + jnp.dot(att_ref[...], wo_ref[...], preferred_element_type=F32)


def _attn_mixer_sample(x, g, w_q, qg, rope, k2, v2, cache, sink_col, w_o, layer, blayer, *, tm, seg):
    t, d = x.shape
    tm = min(tm, t)
    nb = tm // seg
    row = lambda i: (i, 0)
    const = lambda i: (0, 0)
    resident = dict(pipeline_mode=pl.Buffered(1))
    cur = pl.BlockSpec((N_KV_HEADS, tm, LANES), lambda i: (0, i, 0))
    cspec = pl.BlockSpec((nb, N_KV_HEADS, WINDOW, LANES), lambda i: (i, 0, 0, 0))
    in_specs = [
        pl.BlockSpec((tm, d), row),
        pl.BlockSpec((None, 1, d), lambda i: (layer, 0, 0)),
        pl.BlockSpec((None, d, d), lambda i: (blayer, 0, 0), **resident),
        pl.BlockSpec((1, LANES), const),
        pl.BlockSpec((tm, LANES), row),
        pl.BlockSpec((tm, LANES), row),
        pl.BlockSpec((tm, LANES), row),
        cspec, cur, cspec, cur,
        pl.BlockSpec((N_KV_HEADS, GROUP * seg, 1), lambda i: (0, 0, 0)),
        pl.BlockSpec((None, d, d), lambda i: (blayer, 0, 0), **resident),
    ]
    nbytes = 2 * d * d * 2 + 4 * tm * d * 4 + tm * d * (4 + 3 * 2) + 8 * tm * LANES * 4 * 2 + 4 * 2 ** 20
    return pl.pallas_call(
        functools.partial(_attn_sample_body, seg=seg),
        grid=(t // tm,),
        in_specs=in_specs,
        out_specs=pl.BlockSpec((tm, d), row),
        out_shape=jax.ShapeDtypeStruct((t, d), F32),
        scratch_shapes=[pltpu.VMEM((tm, d), BF16), pltpu.VMEM((tm, d), BF16), pltpu.VMEM((tm, d), BF16),
                        pltpu.VMEM((KEYS, LANES), BF16), pltpu.VMEM((KEYS, LANES), BF16)],
        compiler_params=pltpu.CompilerParams(
            dimension_semantics=("arbitrary",), vmem_limit_bytes=_vmem_limit(nbytes)),
        name="attn_mixer_sample",
    )(x, g, w_q, qg, *rope, cache[0], k2, cache[1], v2, sink_col, w_o)


def _attn_mixer_prompt(x, g, w_qt, qg, cos_t, sin_t, kb, vt, sink_rows, w_ot, layer, blayer, *, tm):
    t, d = x.shape
    tm = min(tm, t)
    nkv = N_KV_HEADS * HEAD_DIM
    nh = d // HEAD_DIM
    half = ROT_DIM // 2
    prev_blk = lambda i: jnp.maximum(i * (tm // WINDOW) - 1, 0)
    resident = dict(pipeline_mode=pl.Buffered(1))
    in_specs = [
        pl.BlockSpec((tm, d), lambda i: (i, 0)),
        pl.BlockSpec((None, 1, d), lambda i: (layer, 0, 0)),
        pl.BlockSpec((None, d, d), lambda i: (blayer, 0, 0), **resident),
        pl.BlockSpec((HEAD_DIM, LANES), lambda i: (0, 0)),
        pl.BlockSpec((half, tm), lambda i: (0, i)),
        pl.BlockSpec((half, tm), lambda i: (0, i)),
        pl.BlockSpec((WINDOW, nkv), lambda i: (prev_blk(i), 0)),
        pl.BlockSpec((tm, nkv), lambda i: (i, 0)),
        pl.BlockSpec((nkv, WINDOW), lambda i: (0, prev_blk(i))),
        pl.BlockSpec((nkv, tm), lambda i: (0, i)),
        pl.BlockSpec((nh, LANES), lambda i: (0, 0)),
        pl.BlockSpec((None, d, d), lambda i: (blayer, 0, 0), **resident),
    ]
    nbytes = (2 * d * d * 2 + 4 * tm * d * 4 + 3 * tm * d * 4 + 2 * tm * d * 2 + 2 * KEYS * GROUP * LANES * 2
              + KEYS * GROUP * LANES * 4 * 2 + 4 * (WINDOW + tm) * nkv * 2)
    return pl.pallas_call(
        _attn_prompt_body,
        grid=(t // tm,),
        in_specs=in_specs,
        out_specs=pl.BlockSpec((tm, d), lambda i: (i, 0)),
        out_shape=jax.ShapeDtypeStruct((t, d), F32),
        scratch_shapes=[pltpu.VMEM((d, tm), BF16), pltpu.VMEM((d, tm), BF16),
                        pltpu.VMEM((WINDOW + tm, nkv), BF16), pltpu.VMEM((nkv, WINDOW + tm), BF16)],
        compiler_params=pltpu.CompilerParams(
            dimension_semantics=("arbitrary",), vmem_limit_bytes=_vmem_limit(nbytes)),
        name="attn_mixer_prompt",
    )(x, g, w_qt, qg, cos_t, sin_t, kb, kb, vt, vt, sink_rows, w_ot)


def _rope_tables(pos):
    half = ROT_DIM // 2
    inv = ROPE_THETA ** (-jnp.arange(half, dtype=F32) / half)
    ang = pos.astype(F32)[:, None] * inv[None, :]
    cos, sin = jnp.cos(ang), jnp.sin(ang)
    n = pos.shape[0]
    one = jnp.ones((n, HEAD_DIM - ROT_DIM), F32)
    zero = jnp.zeros((n, HEAD_DIM - ROT_DIM), F32)
    zh = jnp.zeros((n, half), F32)
    c = jnp.concatenate([cos, cos, one], axis=1)
    sneg = jnp.concatenate([-sin, zh, zero], axis=1)
    spos = jnp.concatenate([zh, sin, zero], axis=1)
    return tuple(jnp.tile(a, (1, LANES // HEAD_DIM)) for a in (c, sneg, spos))


def _sink_column(sinks_l, rows_per_head):
    s = (sinks_l.astype(F32) * LOG2E).reshape(N_KV_HEADS, GROUP // 2, 2).transpose(0, 2, 1)
    return jnp.repeat(s.reshape(N_KV_HEADS, GROUP), rows_per_head, axis=1)[..., None]


def _dup_heads(t):
    t = t.transpose(0, 2, 1, 3)
    return jnp.concatenate([t, t], axis=-1).astype(BF16)


def _forward(x_prompt, x_sample, state_conv, cache_k, cache_v, mix_norm_g, mlp_norm_g, w_up, w_down,
             conv_w_in, conv_w, conv_w_out, kv_norm_g, w_kv, k_norm_g, w_q, q_norm_g, sinks, w_o,
             *, tm_mlp, tf, tf_cast, tm_conv, tn, tn_cast, tm_attn, tm_attn_s, tm_kv):
    _, s, d = x_prompt.shape
    b, l, _ = x_sample.shape
    n_a = conv_w_in.shape[0]
    depth = w_up.shape[0]
    xp = x_prompt.reshape(s, d)
    xs = x_sample.reshape(b * l, d)

    w_kv_b, w_q_b, w_o_b = w_kv.astype(BF16), w_q.astype(BF16), w_o.astype(BF16)
    w_qt_b, w_ot_b = w_q_b.transpose(0, 2, 1), w_o_b.transpose(0, 2, 1)
    mix_g = mix_norm_g.reshape(depth, 1, d)
    mlp_g = mlp_norm_g.reshape(depth, 1, d)

    half = ROT_DIM // 2
    ang_t = (ROPE_THETA ** (-jnp.arange(half, dtype=F32) / half))[:, None] * jnp.arange(s).astype(F32)[None, :]
    cos_t, sin_t = jnp.cos(ang_t), jnp.sin(ang_t)
    rope_p = _rope_tables(jnp.arange(s))
    rope_s = _rope_tables(jnp.tile(PAST_LEN + jnp.arange(l), b))
    kg = jnp.tile(k_norm_g.astype(F32), LANES // HEAD_DIM).reshape(1, LANES)

    conv_p, conv_s = [], []
    for i in range(depth):
        if i < n_a:
            xs, cs, conv_wb = _conv_mixer(xs, mix_g, conv_w_in, conv_w, conv_w_out, state_conv, i, tm=None, tn=tn_cast, seg=l)
            xp, cp = _conv_mixer(xp, mix_g, None, conv_w, conv_wb, None, i, tm=tm_conv, tn=tn, seg=None)
            conv_p.append(cp)
            conv_s.append(cs)
        else:
            if i == n_a:
                kp, vp, kbp, vtp = _shared_kv(xp, kv_norm_g.reshape(1, d), w_kv_b, kg, rope_p, tm=tm_kv, dup=False)
                ks, vs, k2s, v2s = _shared_kv(xs, kv_norm_g.reshape(1, d), w_kv_b, kg, rope_s, tm=tm_kv, dup=True)
                cache2 = (_dup_heads(cache_k), _dup_heads(cache_v))
            j = i - n_a
            qg = jnp.tile(q_norm_g[j].astype(F32), LANES // HEAD_DIM).reshape(1, LANES)
            qg_t = jnp.broadcast_to((q_norm_g[j].astype(F32) * (SCALE * LOG2E))[:, None], (HEAD_DIM, LANES))
            sink_rows = jnp.broadcast_to((sinks[j].astype(F32) * LOG2E)[:, None], (sinks.shape[1], LANES))
            xp = _attn_mixer_prompt(xp, mix_g, w_qt_b, qg_t, cos_t, sin_t, kbp, vtp, sink_rows, w_o_b, i, j, tm=tm_attn)
            xs = _attn_mixer_sample(xs, mix_g, w_q_b, qg, rope_s, k2s, v2s, cache2, _sink_column(sinks[j], l),
                                    w_o_b, i, j, tm=tm_attn_s, seg=l)
        xs, w_up_b, w_down_b = _mlp_cast(xs, mlp_g, w_up, w_down, i, tf=tf_cast)
        xp = _mlp(xp, mlp_g, w_up_b, w_down_b, i, tm=tm_mlp, tf=tf)

    hd = (N_KV_HEADS, HEAD_DIM)
    ks_new = ks.reshape(b, l, *hd)
    vs_new = vs.reshape(b, l, *hd)
    return (xp.reshape(1, s, d), xs.reshape(b, l, d), jnp.stack(conv_p), jnp.stack(conv_s),
            kp[-WINDOW:].reshape(1, WINDOW, *hd), vp[-WINDOW:].reshape(1, WINDOW, *hd),
            jnp.concatenate([cache_k[:, l:], ks_new], axis=1), jnp.concatenate([cache_v[:, l:], vs_new], axis=1))


def kernel(x_prompt, x_sample, state_conv, cache_k, cache_v, mix_norm_g, mlp_norm_g, w_up, w_down, conv_w_in, conv_w, conv_w_out, kv_norm_g, w_kv, k_norm_g, w_q, q_norm_g, sinks, w_o):
    return _forward(x_prompt, x_sample, state_conv, cache_k, cache_v, mix_norm_g, mlp_norm_g, w_up, w_down,
                    conv_w_in, conv_w, conv_w_out, kv_norm_g, w_kv, k_norm_g, w_q, q_norm_g, sinks, w_o,
                    tm_mlp=512, tf=2048, tf_cast=512, tm_conv=512, tn=512, tn_cast=256, tm_attn=512, tm_attn_s=256, tm_kv=1024)
```

```python
import functools

import jax
import jax.numpy as jnp
from jax import lax
from jax.experimental import pallas as pl
from jax.experimental.pallas import tpu as pltpu

EPS = 1e-6
CHUNK = 64
WINDOW = 128
HEAD_DIM = 64
N_KV_HEADS = 4
GROUP = 8
ROT_DIM = 16
ROPE_THETA = 500000.0
PAST_LEN = 2048
SCALE = HEAD_DIM ** -0.5
LOG2E = 1.4426950408889634

LANES = 128
KEYS = 2 * WINDOW
KV_ROW_GROUP = 256
CONV_SUBCHUNK = 256
VMEM_LIMIT_CAP = 56 * 2 ** 20

F32 = jnp.float32
BF16 = jnp.bfloat16


def _vmem_limit(nbytes):
    return int(min(VMEM_LIMIT_CAP, max(32 * 2 ** 20, nbytes * 5 // 4 + 4 * 2 ** 20)))


def _rms(x, g):
    return x * lax.rsqrt(jnp.mean(x * x, axis=-1, keepdims=True) + EPS) * g


def _half_mask(shape):
    return lax.broadcasted_iota(jnp.int32, shape, len(shape) - 1) < HEAD_DIM


def _head_norm_rope(t, gain, cos, sneg, spos):
    lo = _half_mask(t.shape)
    sq = t * t
    s_lo = jnp.sum(jnp.where(lo, sq, 0.0), axis=-1, keepdims=True)
    s_hi = jnp.sum(jnp.where(lo, 0.0, sq), axis=-1, keepdims=True)
    inv = jnp.where(lo, lax.rsqrt(s_lo / HEAD_DIM + EPS), lax.rsqrt(s_hi / HEAD_DIM + EPS))
    tn = t * inv * gain
    half = ROT_DIM // 2
    return tn * cos + pltpu.roll(tn, LANES - half, 1) * sneg + pltpu.roll(tn, half, 1) * spos


def _mlp_body(x_ref, g_ref, wu_ref, wd_ref, o_ref, xn_ref):
    def contribution():
        h = jnp.dot(xn_ref[...], wu_ref[...], preferred_element_type=F32)
        h = jnp.square(jnp.maximum(h, 0.0)).astype(BF16)
        return jnp.dot(h, wd_ref[...], preferred_element_type=F32)

    @pl.when(pl.program_id(1) == 0)
    def _():
        xn_ref[...] = _rms(x_ref[...], g_ref[...]).astype(BF16)
        o_ref[...] = x_ref[...] + contribution()

    @pl.when(pl.program_id(1) > 0)
    def _():
        o_ref[...] += contribution()


def _mlp_cast_body(x_ref, g_ref, wu_ref, wd_ref, o_ref, wub_ref, wdb_ref, xn_ref):
    def contribution():
        wu = wu_ref[...].astype(BF16)
        wd = wd_ref[...].astype(BF16)
        wub_ref[...] = wu
        wdb_ref[...] = wd
        h = jnp.dot(xn_ref[...], wu, preferred_element_type=F32)
        h = jnp.square(jnp.maximum(h, 0.0)).astype(BF16)
        return jnp.dot(h, wd, preferred_element_type=F32)

    @pl.when(pl.program_id(0) == 0)
    def _():
        xn_ref[...] = _rms(x_ref[...], g_ref[...]).astype(BF16)
        o_ref[...] = x_ref[...] + contribution()

    @pl.when(pl.program_id(0) > 0)
    def _():
        o_ref[...] += contribution()


def _mlp_cast(x, g, w_up, w_down, layer, *, tf):
    t, d = x.shape
    f = w_up.shape[2]
    tf = min(tf, f)
    nbytes = 2 * t * d * 4 + t * d * 2 + 2 * 2 * d * tf * (4 + 2) + t * tf * 6
    return pl.pallas_call(
        _mlp_cast_body,
        grid=(f // tf,),
        in_specs=[
            pl.BlockSpec((t, d), lambda j: (0, 0), pipeline_mode=pl.Buffered(1)),
            pl.BlockSpec((None, 1, d), lambda j: (layer, 0, 0)),
            pl.BlockSpec((None, d, tf), lambda j: (layer, 0, j)),
            pl.BlockSpec((None, tf, d), lambda j: (layer, j, 0)),
        ],
        out_specs=[
            pl.BlockSpec((t, d), lambda j: (0, 0), pipeline_mode=pl.Buffered(1)),
            pl.BlockSpec((d, tf), lambda j: (0, j)),
            pl.BlockSpec((tf, d), lambda j: (j, 0)),
        ],
        out_shape=[
            jax.ShapeDtypeStruct((t, d), F32),
            jax.ShapeDtypeStruct((d, f), BF16),
            jax.ShapeDtypeStruct((f, d), BF16),
        ],
        scratch_shapes=[pltpu.VMEM((t, d), BF16)],
        compiler_params=pltpu.CompilerParams(
            dimension_semantics=("arbitrary",), vmem_limit_bytes=_vmem_limit(nbytes)),
        name="mlp_cast",
    )(x, g, w_up, w_down)


def _mlp(x, g, w_up, w_down, layer, *, tm, tf):
    t, d = x.shape
    f = w_up.shape[1]
    tm, tf = min(tm, t), min(tf, f)
    nbytes = 2 * (2 * tm * d * 4 + 2 * d * tf * 2) + tm * d * 2 + tm * tf * 6
    return pl.pallas_call(
        _mlp_body,
        grid=(t // tm, f // tf),
        in_specs=[
            pl.BlockSpec((tm, d), lambda i, j: (i, 0)),
            pl.BlockSpec((None, 1, d), lambda i, j: (layer, 0, 0)),
            pl.BlockSpec((d, tf), lambda i, j: (0, j)),
            pl.BlockSpec((tf, d), lambda i, j: (j, 0)),
        ],
        out_specs=pl.BlockSpec((tm, d), lambda i, j: (i, 0)),
        out_shape=jax.ShapeDtypeStruct((t, d), F32),
        scratch_shapes=[pltpu.VMEM((tm, d), BF16)],
        compiler_params=pltpu.CompilerParams(
            dimension_semantics=("arbitrary", "arbitrary"), vmem_limit_bytes=_vmem_limit(nbytes)),
        name="mlp",
    )(x, g, w_up, w_down)


def _conv_body(*refs, seg, tn, cast):
    if seg is None:
        x_ref, g_ref, wb_ref, wc_ref, wu_ref, cw_ref, wout_ref, o_ref, zl_ref, *rest = refs
    else:
        x_ref, g_ref, wb_ref, wc_ref, wu_ref, cw_ref, wout_ref, st_ref, o_ref, zl_ref, *rest = refs
    xn_ref = rest[-1]
    i, j = pl.program_id(0), pl.program_id(1)

    if seg is None:
        @pl.when(i == 0)
        def _():
            zl_ref[j] = jnp.zeros((2, tn), F32)

    tm = x_ref.shape[0]
    tc = min(tn, CONV_SUBCHUNK)

    def weights():
        if not cast:
            return wb_ref, wc_ref, wu_ref, wout_ref
        ws = [r[...].astype(BF16) for r in (wb_ref, wc_ref, wu_ref, wout_ref)]
        for dst, w in zip(rest[:4], ws):
            dst[...] = w
        return ws

    def project(c, wb, wc, wu):
        cs = slice(c * tc, (c + 1) * tc)
        xn = xn_ref[...]
        z = jnp.dot(xn, wc[:, cs], preferred_element_type=F32) * jnp.dot(xn, wu[:, cs], preferred_element_type=F32)
        return jnp.dot(xn, wb[:, cs], preferred_element_type=F32), z

    def gated_conv(gate_b, z, c):
        cs = slice(c * tc, (c + 1) * tc)
        row = lax.broadcasted_iota(jnp.int32, z.shape, 0)
        r1 = pltpu.roll(z, 1, 0)
        r2 = pltpu.roll(z, 2, 0)
        if seg is None:
            prev = zl_ref[j, :, cs]
            p0, p1 = prev[0:1, :], prev[1:2, :]
            zl_ref[j, :, cs] = z[tm - 2:, :]
        else:
            nb = tm // seg
            st = st_ref[:, :, cs]
            p0 = jnp.broadcast_to(st[:, 0:1, :], (nb, seg, tc)).reshape(tm, tc)
            p1 = jnp.broadcast_to(st[:, 1:2, :], (nb, seg, tc)).reshape(tm, tc)
            row = row % seg
            zl_ref[:, :, cs] = z.reshape(nb, seg, tc)[:, seg - 2:, :]
        zp1 = jnp.where(row == 0, p1, r1)
        zp2 = jnp.where(row == 0, p0, jnp.where(row == 1, p1, r2))
        cw = cw_ref[:, cs]
        conv = zp2 * cw[0:1, :] + zp1 * cw[1:2, :] + z * cw[2:3, :]
        return (gate_b * conv).astype(BF16)

    def contribution():
        nc = tn // tc
        wb, wc, wu, wout = weights()
        out = None
        pending = project(0, wb, wc, wu)
        for c in range(nc):
            nxt = project(c + 1, wb, wc, wu) if c + 1 < nc else None
            y = gated_conv(*pending, c)
            part = jnp.dot(y, wout[c * tc:(c + 1) * tc, :], preferred_element_type=F32)
            out = part if out is None else out + part
            pending = nxt
        return out

    @pl.when(j == 0)
    def _():
        xn_ref[...] = _rms(x_ref[...], g_ref[...]).astype(BF16)
        o_ref[...] = x_ref[...] + contribution()

    @pl.when(j > 0)
    def _():
        o_ref[...] += contribution()


def _conv_mixer(x, g, w_in, cw, w_out, state, layer, *, tm, tn, seg):
    t, d = x.shape
    cast = w_in is not None
    tm = t if cast else min(tm, t)
    nj = d // tn
    once = dict(pipeline_mode=pl.Buffered(1)) if cast else {}
    in_specs = [
        pl.BlockSpec((tm, d), lambda i, j: (i, 0), **once),
        pl.BlockSpec((None, 1, d), lambda i, j: (layer, 0, 0)),
    ]
    if cast:
        in_specs += [
            pl.BlockSpec((None, d, tn), lambda i, j: (layer, 0, j)),
            pl.BlockSpec((None, d, tn), lambda i, j: (layer, 0, nj + j)),
            pl.BlockSpec((None, d, tn), lambda i, j: (layer, 0, 2 * nj + j)),
            pl.BlockSpec((None, 3, tn), lambda i, j: (layer, 0, j)),
            pl.BlockSpec((None, tn, d), lambda i, j: (layer, j, 0)),
        ]
        args = [x, g, w_in, w_in, w_in, cw, w_out]
    else:
        in_specs += [
            pl.BlockSpec((d, tn), lambda i, j: (0, j)),
            pl.BlockSpec((d, tn), lambda i, j: (0, j)),
            pl.BlockSpec((d, tn), lambda i, j: (0, j)),
            pl.BlockSpec((None, 3, tn), lambda i, j: (layer, 0, j)),
            pl.BlockSpec((tn, d), lambda i, j: (j, 0)),
        ]
        args = [x, g, *w_out[:3], cw, w_out[3]]
    if seg is None:
        zl_shape = (nj, 2, tn)
        zl_spec = pl.BlockSpec((nj, 2, tn), lambda i, j: (0, 0, 0))
    else:
        nb = tm // seg
        zl_shape = (t // seg, 2, d)
        zl_spec = pl.BlockSpec((nb, 2, tn), lambda i, j: (i, 0, j))
        in_specs.append(pl.BlockSpec((None, nb, 2, tn), lambda i, j: (layer, i, 0, j)))
        args.append(state)
    out_specs = [pl.BlockSpec((tm, d), lambda i, j: (i, 0), **once), zl_spec]
    out_shape = [jax.ShapeDtypeStruct((t, d), F32), jax.ShapeDtypeStruct(zl_shape, F32)]
    if cast:
        out_specs += [pl.BlockSpec((d, tn), lambda i, j: (0, j))] * 3 + [pl.BlockSpec((tn, d), lambda i, j: (j, 0))]
        out_shape += [jax.ShapeDtypeStruct((d, d), BF16)] * 4
        nbytes = 2 * tm * d * 4 + 2 * 4 * d * tn * (4 + 2) + tm * d * 2 + tm * tn * 4 * 8
    else:
        nbytes = 2 * (2 * tm * d * 4 + d * 3 * tn * 2 + tn * d * 2) + tm * d * 2 + tm * tn * 4 * 8
    y, zl, *w_b = pl.pallas_call(
        functools.partial(_conv_body, seg=seg, tn=tn, cast=cast),
        grid=(t // tm, nj),
        in_specs=in_specs,
        out_specs=out_specs,
        out_shape=out_shape,
        scratch_shapes=[pltpu.VMEM((tm, d), BF16)],
        compiler_params=pltpu.CompilerParams(
            dimension_semantics=("arbitrary", "arbitrary"), vmem_limit_bytes=_vmem_limit(nbytes)),
        name="conv_mixer_cast" if cast else "conv_mixer",
    )(*args)
    if seg is None:
        zl = zl.transpose(1, 0, 2).reshape(1, 2, d)
    return (y, zl, tuple(w_b)) if cast else (y, zl)


def _kv_body(x_ref, g_ref, wkv_ref, kg_ref, cos_ref, sneg_ref, spos_ref, k_ref, v_ref, ka_ref, va_ref, *, dup):
    tm = x_ref.shape[0]
    nkv = N_KV_HEADS * HEAD_DIM
    gr = min(tm, KV_ROW_GROUP)
    lo = _half_mask((gr, LANES))

    def normed(r):
        return _rms(x_ref[r * gr:(r + 1) * gr, :], g_ref[...]).astype(BF16)

    xn = normed(0)
    for r in range(tm // gr):
        rs = slice(r * gr, (r + 1) * gr)
        xn_next = normed(r + 1) if (r + 1) * gr < tm else None
        kv = jnp.dot(xn, wkv_ref[...], preferred_element_type=F32)
        for p in range(nkv // LANES):
            sl = slice(p * LANES, (p + 1) * LANES)
            kr = _head_norm_rope(kv[:, sl], kg_ref[...], cos_ref[rs, :], sneg_ref[rs, :], spos_ref[rs, :])
            vr = kv[:, nkv + p * LANES: nkv + (p + 1) * LANES]
            k_ref[rs, sl] = kr
            v_ref[rs, sl] = vr
            if dup:
                for src, dst in ((kr, ka_ref), (vr, va_ref)):
                    sw = pltpu.roll(src, HEAD_DIM, 1)
                    dst[2 * p, rs, :] = jnp.where(lo, src, sw).astype(BF16)
                    dst[2 * p + 1, rs, :] = jnp.where(lo, sw, src).astype(BF16)
            else:
                ka_ref[rs, sl] = kr.astype(BF16)
                va_ref[sl, rs] = vr.T.astype(BF16)
        xn = xn_next


def _shared_kv(x, g, w_kv, kg, rope, *, tm, dup):
    t, d = x.shape
    tm = min(tm, t)
    nkv = N_KV_HEADS * HEAD_DIM
    row = lambda i: (i, 0)
    const = lambda i: (0, 0)
    if dup:
        f32_spec, f32_shape = pl.BlockSpec((tm, nkv), row), jax.ShapeDtypeStruct((t, nkv), F32)
        aux_specs = [pl.BlockSpec((N_KV_HEADS, tm, LANES), lambda i: (0, i, 0))] * 2
        aux_shapes = [jax.ShapeDtypeStruct((N_KV_HEADS, t, LANES), BF16)] * 2
    else:
        f32_spec, f32_shape = pl.BlockSpec((tm, nkv), const), jax.ShapeDtypeStruct((tm, nkv), F32)
        aux_specs = [pl.BlockSpec((tm, nkv), row), pl.BlockSpec((nkv, tm), lambda i: (0, i))]
        aux_shapes = [jax.ShapeDtypeStruct((t, nkv), BF16), jax.ShapeDtypeStruct((nkv, t), BF16)]
    nbytes = 2 * (tm * d * 4 + d * 2 * nkv * 2 + 5 * tm * nkv * 4) + tm * d * 8
    return pl.pallas_call(
        functools.partial(_kv_body, dup=dup),
        grid=(t // tm,),
        in_specs=[
            pl.BlockSpec((tm, d), row),
            pl.BlockSpec((1, d), const),
            pl.BlockSpec((d, 2 * nkv), const),
            pl.BlockSpec((1, LANES), const),
            pl.BlockSpec((tm, LANES), row),
            pl.BlockSpec((tm, LANES), row),
            pl.BlockSpec((tm, LANES), row),
        ],
        out_specs=[f32_spec, f32_spec] + aux_specs,
        out_shape=[f32_shape, f32_shape] + aux_shapes,
        compiler_params=pltpu.CompilerParams(
            dimension_semantics=("arbitrary",), vmem_limit_bytes=_vmem_limit(nbytes)),
        name="shared_kv",
    )(x, g, w_kv, kg, *rope)


def _project_q(x_ref, g_ref, wq_ref, qg_ref, cos_ref, sneg_ref, spos_ref, qe_ref, qo_ref):
    xn = _rms(x_ref[...], g_ref[...]).astype(BF16)
    q = jnp.dot(xn, wq_ref[...], preferred_element_type=F32)
    lo = _half_mask((q.shape[0], LANES))
    for p in range(q.shape[1] // LANES):
        sl = slice(p * LANES, (p + 1) * LANES)
        qr = _head_norm_rope(q[:, sl], qg_ref[...], cos_ref[...], sneg_ref[...], spos_ref[...]) * (SCALE * LOG2E)
        qe_ref[:, sl] = jnp.where(lo, qr, 0.0).astype(BF16)
        qo_ref[:, sl] = jnp.where(lo, 0.0, qr).astype(BF16)


def _attn_prompt_body(x_ref, g_ref, wqt_ref, qg_ref, cos_ref, sin_ref,
                      kprev_ref, kcur_ref, vprev_ref, vcur_ref, sink_ref, wo_ref,
                      o_ref, qt_ref, att_ref, kw_ref, vw_ref):
    i = pl.program_id(0)
    tm, d = x_ref.shape
    half = ROT_DIM // 2
    cos, sin = cos_ref[...], sin_ref[...]
    gain = jnp.concatenate([qg_ref[...]] * (tm // LANES), axis=1)
    rows = GROUP * HEAD_DIM

    nblk = tm // LANES
    dk = d // nblk

    x = x_ref[...]
    inv_rms = lax.rsqrt(jnp.mean(x * x, axis=-1, keepdims=True) + EPS)
    xn = {}

    def normed(c):
        if c not in xn:
            xn[c] = (x_ref[:, c * dk:(c + 1) * dk] * inv_rms * g_ref[:, c * dk:(c + 1) * dk]).astype(BF16)
        return xn[c]

    def project_q(kh, c):
        return lax.dot_general(wqt_ref[kh * rows:(kh + 1) * rows, c * dk:(c + 1) * dk], normed(c),
                               (((1,), (1,)), ((), ())), preferred_element_type=F32)

    def norm_rope_q(qt, kh, c):
        for j in range(c * (GROUP // nblk), (c + 1) * (GROUP // nblk)):
            t = qt[j * HEAD_DIM:(j + 1) * HEAD_DIM, :]
            tn = t * lax.rsqrt(jnp.sum(t * t, axis=0, keepdims=True) / HEAD_DIM + EPS) * gain
            x1, x2 = tn[:half], tn[half:ROT_DIM]
            h = GROUP * kh + j
            qt_ref[h * HEAD_DIM:(h + 1) * HEAD_DIM, :] = jnp.concatenate(
                [x1 * cos - x2 * sin, x2 * cos + x1 * sin, tn[ROT_DIM:]], axis=0).astype(BF16)


    kw_ref[:WINDOW, :] = kprev_ref[...]
    kw_ref[WINDOW:, :] = kcur_ref[...]
    vw_ref[:, :WINDOW] = vprev_ref[...]
    vw_ref[:, WINDOW:] = vcur_ref[...]

    first_query_chunk = lax.broadcasted_iota(jnp.int32, (CHUNK, LANES), 1) < CHUNK
    has_past = jnp.broadcast_to(i > 0, (CHUNK, LANES))
    zeros = jnp.zeros((HEAD_DIM, LANES), BF16)

    def scores_t(p, kh):
        qs = slice(p * LANES, (p + 1) * LANES)
        kwin = kw_ref[p * LANES:p * LANES + KEYS, (kh // 2) * LANES:(kh // 2 + 1) * LANES]
        rhs = jnp.concatenate(
            [jnp.concatenate([qt_ref[h * HEAD_DIM:(h + 1) * HEAD_DIM, qs], zeros] if kh % 2 == 0 else
                             [zeros, qt_ref[h * HEAD_DIM:(h + 1) * HEAD_DIM, qs]], axis=0)
             for h in range(GROUP * kh, GROUP * (kh + 1))], axis=1)
        return jnp.dot(kwin, rhs, preferred_element_type=F32)

    def softmax_t(st, p, kh):
        masks = [first_query_chunk & has_past if p == 0 else first_query_chunk, has_past if p == 0 else None,
                 None, ~first_query_chunk]
        sink_terms, pts = [], []
        for j in range(GROUP):
            s = jnp.concatenate(
                [st[c * CHUNK:(c + 1) * CHUNK, j * LANES:(j + 1) * LANES] if mask is None else
                 jnp.where(mask, st[c * CHUNK:(c + 1) * CHUNK, j * LANES:(j + 1) * LANES], -jnp.inf)
                 for c, mask in enumerate(masks)], axis=0)
            sink = sink_ref[GROUP * kh + j:GROUP * kh + j + 1, :]
            m = jnp.maximum(jnp.max(s, axis=0, keepdims=True), sink)
            sink_terms.append(jnp.exp2(sink - m))
            pts.append(jnp.exp2(s - m).astype(BF16))
        return jnp.concatenate(pts, axis=1), jnp.concatenate(sink_terms, axis=1)

    ones_rows = jnp.ones((16, KEYS), BF16)

    def weighted_values_t(p, kh, pt, sink_term):
        qs = slice(p * LANES, (p + 1) * LANES)
        v_ones = jnp.concatenate([vw_ref[kh * HEAD_DIM:(kh + 1) * HEAD_DIM, p * LANES:p * LANES + KEYS], ones_rows], axis=0)
        ot = jnp.dot(v_ones, pt, preferred_element_type=F32)
        ot = ot[:HEAD_DIM] * (1.0 / (ot[HEAD_DIM:HEAD_DIM + 1] + sink_term))
        for j in range(GROUP):
            h = GROUP * kh + j
            att_ref[h * HEAD_DIM:(h + 1) * HEAD_DIM, qs] = ot[:, j * LANES:(j + 1) * LANES].astype(BF16)

    def full_q(kh):
        qt = project_q(kh, 0)
        for c in range(1, nblk):
            qt = qt + project_q(kh, c)
        return qt

    qts = {0: full_q(0)}
    for c in range(nblk):
        piece = project_q(1, c)
        qts[1] = piece if c == 0 else qts[1] + piece
        norm_rope_q(qts[0], 0, c)
    for kh in range(N_KV_HEADS):
        st = scores_t(0, kh)
        pending = None
        for n in range(nblk):
            st_next = scores_t(n + 1, kh) if n + 1 < nblk else None
            if kh + 2 < N_KV_HEADS:
                piece = project_q(kh + 2, n)
                qts[kh + 2] = piece if n == 0 else qts[kh + 2] + piece
            pt, sink_term = softmax_t(st, n, kh)
            if kh + 1 < N_KV_HEADS:
                norm_rope_q(qts[kh + 1], kh + 1, n)
            if pending is not None:
                weighted_values_t(*pending)
            pending = (n, kh, pt, sink_term)
            st = st_next
        weighted_values_t(*pending)
    o_ref[...] = x_ref[...] + lax.dot_general(att_ref[...], wo_ref[...], (((0,), (0,)), ((), ())),
                                              preferred_element_type=F32)


def _attn_sample_body(x_ref, g_ref, wq_ref, qg_ref, cos_ref, sneg_ref, spos_ref,
                      kc_ref, knew_ref, vc_ref, vnew_ref, sink_ref, wo_ref,
                      o_ref, qe_ref, qo_ref, att_ref, *, seg):
    tm = x_ref.shape[0]
    _project_q(x_ref, g_ref, wq_ref, qg_ref, cos_ref, sneg_ref, spos_ref, qe_ref, qo_ref)
    nkeys = WINDOW + seg
    valid = lax.broadcasted_iota(jnp.int32, (1, KEYS), 1) < nkeys
    pad = jnp.zeros((KEYS - nkeys, LANES), BF16)
    pairs = GROUP // 2
    lo = _half_mask((seg, LANES))

    def window(cache_ref, new_ref, b, kh):
        return jnp.concatenate([cache_ref[b, kh], new_ref[kh, b * seg:(b + 1) * seg, :], pad], axis=0)

    def head_cols(kh):
        return [slice((pairs * kh + j) * LANES, (pairs * kh + j + 1) * LANES) for j in range(pairs)]

    def scores(b, kh):
        rows = slice(b * seg, (b + 1) * seg)
        qcat = jnp.concatenate([qe_ref[rows, c] for c in head_cols(kh)] + [qo_ref[rows, c] for c in head_cols(kh)], axis=0)
        return lax.dot_general(qcat, window(kc_ref, knew_ref, b, kh), (((1,), (1,)), ((), ())),
                               preferred_element_type=F32)

    def softmax(s, kh):
        sink = sink_ref[kh]
        s = jnp.where(valid, s, -jnp.inf)
        m = jnp.maximum(jnp.max(s, axis=-1, keepdims=True), sink)
        e = jnp.exp2(s - m)
        return e.astype(BF16), jnp.sum(e, axis=-1, keepdims=True) + jnp.exp2(sink - m)

    def weighted_values(b, kh, p, den):
        o = jnp.dot(p, window(vc_ref, vnew_ref, b, kh), preferred_element_type=F32) / den
        for j, c in enumerate(head_cols(kh)):
            att_ref[b * seg:(b + 1) * seg, c] = jnp.where(
                lo, o[j * seg:(j + 1) * seg], o[(pairs + j) * seg:(pairs + j + 1) * seg]).astype(BF16)

    blocks = [(b, kh) for b in range(tm // seg) for kh in range(N_KV_HEADS)]
    s = scores(*blocks[0])
    pending = None
    for n, blk in enumerate(blocks):
        s_next = scores(*blocks[n + 1]) if n + 1 < len(blocks) else None
        p, den = softmax(s, blk[1])
        if pending is not None:
            weighted_values(*pending)
        pending = (*blk, p, den)
        s = s_next
    weighted_values(*pending)
    o_ref[...] = x_ref[...] + jnp.dot(att_ref[...], wo_ref[...], preferred_element_type=F32)


def _attn_mixer_sample(x, g, w_q, qg, rope, k2, v2, cache, sink_col, w_o, layer, blayer, *, tm, seg):
    t, d = x.shape
    tm = min(tm, t)
    nb = tm // seg
    row = lambda i: (i, 0)
    const = lambda i: (0, 0)
    resident = dict(pipeline_mode=pl.Buffered(1))
    cur = pl.BlockSpec((N_KV_HEADS, tm, LANES), lambda i: (0, i, 0))
    cspec = pl.BlockSpec((nb, N_KV_HEADS, WINDOW, LANES), lambda i: (i, 0, 0, 0))
    in_specs = [
        pl.BlockSpec((tm, d), row),
        pl.BlockSpec((None, 1, d), lambda i: (layer, 0, 0)),
        pl.BlockSpec((None, d, d), lambda i: (blayer, 0, 0), **resident),
        pl.BlockSpec((1, LANES), const),
        pl.BlockSpec((tm, LANES), row),
        pl.BlockSpec((tm, LANES), row),
        pl.BlockSpec((tm, LANES), row),
        cspec, cur, cspec, cur,
        pl.BlockSpec((N_KV_HEADS, GROUP * seg, 1), lambda i: (0, 0, 0)),
        pl.BlockSpec((None, d, d), lambda i: (blayer, 0, 0), **resident),
    ]
    nbytes = 2 * d * d * 2 + 4 * tm * d * 4 + tm * d * (4 + 3 * 2) + 8 * tm * LANES * 4 * 2 + 4 * 2 ** 20
    return pl.pallas_call(
        functools.partial(_attn_sample_body, seg=seg),
        grid=(t // tm,),
        in_specs=in_specs,
        out_specs=pl.BlockSpec((tm, d), row),
        out_shape=jax.ShapeDtypeStruct((t, d), F32),
        scratch_shapes=[pltpu.VMEM((tm, d), BF16), pltpu.VMEM((tm, d), BF16), pltpu.VMEM((tm, d), BF16)],
        compiler_params=pltpu.CompilerParams(
            dimension_semantics=("arbitrary",), vmem_limit_bytes=_vmem_limit(nbytes)),
        name="attn_mixer_sample",
    )(x, g, w_q, qg, *rope, cache[0], k2, cache[1], v2, sink_col, w_o)


def _attn_mixer_prompt(x, g, w_qt, qg, cos_t, sin_t, kb, vt, sink_rows, w_o, layer, blayer, *, tm):
    t, d = x.shape
    tm = min(tm, t)
    nkv = N_KV_HEADS * HEAD_DIM
    nh = d // HEAD_DIM
    half = ROT_DIM // 2
    prev_blk = lambda i: jnp.maximum(i * (tm // WINDOW) - 1, 0)
    resident = dict(pipeline_mode=pl.Buffered(1))
    in_specs = [
        pl.BlockSpec((tm, d), lambda i: (i, 0)),
        pl.BlockSpec((None, 1, d), lambda i: (layer, 0, 0)),
        pl.BlockSpec((None, d, d), lambda i: (blayer, 0, 0), **resident),
        pl.BlockSpec((HEAD_DIM, LANES), lambda i: (0, 0)),
        pl.BlockSpec((half, tm), lambda i: (0, i)),
        pl.BlockSpec((half, tm), lambda i: (0, i)),
        pl.BlockSpec((WINDOW, nkv), lambda i: (prev_blk(i), 0)),
        pl.BlockSpec((tm, nkv), lambda i: (i, 0)),
        pl.BlockSpec((nkv, WINDOW), lambda i: (0, prev_blk(i))),
        pl.BlockSpec((nkv, tm), lambda i: (0, i)),
        pl.BlockSpec((nh, LANES), lambda i: (0, 0)),
        pl.BlockSpec((None, d, d), lambda i: (blayer, 0, 0), **resident),
    ]
    nbytes = (2 * d * d * 2 + 4 * tm * d * 4 + 3 * tm * d * 4 + 2 * tm * d * 2 + 2 * KEYS * GROUP * LANES * 2
              + KEYS * GROUP * LANES * 4 * 2 + 4 * (WINDOW + tm) * nkv * 2)
    return pl.pallas_call(
        _attn_prompt_body,
        grid=(t // tm,),
        in_specs=in_specs,
        out_specs=pl.BlockSpec((tm, d), lambda i: (i, 0)),
        out_shape=jax.ShapeDtypeStruct((t, d), F32),
        scratch_shapes=[pltpu.VMEM((d, tm), BF16), pltpu.VMEM((d, tm), BF16),
                        pltpu.VMEM((WINDOW + tm, nkv), BF16), pltpu.VMEM((nkv, WINDOW + tm), BF16)],
        compiler_params=pltpu.CompilerParams(
            dimension_semantics=("arbitrary",), vmem_limit_bytes=_vmem_limit(nbytes)),
        name="attn_mixer_prompt",
    )(x, g, w_qt, qg, cos_t, sin_t, kb, kb, vt, vt, sink_rows, w_o)


def _rope_tables(pos):
    half = ROT_DIM // 2
    inv = ROPE_THETA ** (-jnp.arange(half, dtype=F32) / half)
    ang = pos.astype(F32)[:, None] * inv[None, :]
    cos = jnp.tile(jnp.cos(ang), (1, LANES // half))
    sin = jnp.tile(jnp.sin(ang), (1, LANES // half))
    dim = jnp.arange(LANES) % HEAD_DIM
    c = jnp.where(dim < ROT_DIM, cos, 1.0)
    sneg = jnp.where(dim < half, -sin, 0.0)
    spos = jnp.where((dim >= half) & (dim < ROT_DIM), sin, 0.0)
    return c, sneg, spos


def _sink_column(sinks_l, rows_per_head):
    s = (sinks_l.astype(F32) * LOG2E).reshape(N_KV_HEADS, GROUP // 2, 2).transpose(0, 2, 1)
    return jnp.repeat(s.reshape(N_KV_HEADS, GROUP), rows_per_head, axis=1)[..., None]


def _dup_heads(t):
    t = t.transpose(0, 2, 1, 3)
    return jnp.concatenate([t, t], axis=-1).astype(BF16)


def _forward(x_prompt, x_sample, state_conv, cache_k, cache_v, mix_norm_g, mlp_norm_g, w_up, w_down,
             conv_w_in, conv_w, conv_w_out, kv_norm_g, w_kv, k_norm_g, w_q, q_norm_g, sinks, w_o,
             *, tm_mlp, tf, tf_cast, tm_conv, tn, tn_cast, tm_attn, tm_attn_s, tm_kv):
    _, s, d = x_prompt.shape
    b, l, _ = x_sample.shape
    n_a = conv_w_in.shape[0]
    depth = w_up.shape[0]
    xp = x_prompt.reshape(s, d)
    xs = x_sample.reshape(b * l, d)

    w_kv_b, w_q_b, w_o_b = w_kv.astype(BF16), w_q.astype(BF16), w_o.astype(BF16)
    w_qt_b = w_q_b.transpose(0, 2, 1)
    mix_g = mix_norm_g.reshape(depth, 1, d)
    mlp_g = mlp_norm_g.reshape(depth, 1, d)

    half = ROT_DIM // 2
    ang_t = (ROPE_THETA ** (-jnp.arange(half, dtype=F32) / half))[:, None] * jnp.arange(s).astype(F32)[None, :]
    cos_t, sin_t = jnp.cos(ang_t), jnp.sin(ang_t)
    rope_p = _rope_tables(jnp.arange(s))
    rope_s = _rope_tables(jnp.tile(PAST_LEN + jnp.arange(l), b))
    kg = jnp.tile(k_norm_g.astype(F32), LANES // HEAD_DIM).reshape(1, LANES)

    conv_p, conv_s = [], []
    for i in range(depth):
        if i < n_a:
            xs, cs, conv_wb = _conv_mixer(xs, mix_g, conv_w_in, conv_w, conv_w_out, state_conv, i, tm=None, tn=tn_cast, seg=l)
            xp, cp = _conv_mixer(xp, mix_g, None, conv_w, conv_wb, None, i, tm=tm_conv, tn=tn, seg=None)
            conv_p.append(cp)
            conv_s.append(cs)
        else:
            if i == n_a:
                kp, vp, kbp, vtp = _shared_kv(xp, kv_norm_g.reshape(1, d), w_kv_b, kg, rope_p, tm=tm_kv, dup=False)
                ks, vs, k2s, v2s = _shared_kv(xs, kv_norm_g.reshape(1, d), w_kv_b, kg, rope_s, tm=tm_kv, dup=True)
                cache2 = (_dup_heads(cache_k), _dup_heads(cache_v))
            j = i - n_a
            qg = jnp.tile(q_norm_g[j].astype(F32), LANES // HEAD_DIM).reshape(1, LANES)
            qg_t = jnp.broadcast_to((q_norm_g[j].astype(F32) * (SCALE * LOG2E))[:, None], (HEAD_DIM, LANES))
            sink_rows = jnp.broadcast_to((sinks[j].astype(F32) * LOG2E)[:, None], (sinks.shape[1], LANES))
            xp = _attn_mixer_prompt(xp, mix_g, w_qt_b, qg_t, cos_t, sin_t, kbp, vtp, sink_rows, w_o_b, i, j, tm=tm_attn)
            xs = _attn_mixer_sample(xs, mix_g, w_q_b, qg, rope_s, k2s, v2s, cache2, _sink_column(sinks[j], l),
                                    w_o_b, i, j, tm=tm_attn_s, seg=l)
        xs, w_up_b, w_down_b = _mlp_cast(xs, mlp_g, w_up, w_down, i, tf=tf_cast)
        xp = _mlp(xp, mlp_g, w_up_b, w_down_b, i, tm=tm_mlp, tf=tf)

    hd = (N_KV_HEADS, HEAD_DIM)
    ks_new = ks.reshape(b, l, *hd)
    vs_new = vs.reshape(b, l, *hd)
    return (xp.reshape(1, s, d), xs.reshape(b, l, d), jnp.stack(conv_p), jnp.stack(conv_s),
            kp[-WINDOW:].reshape(1, WINDOW, *hd), vp[-WINDOW:].reshape(1, WINDOW, *hd),
            jnp.concatenate([cache_k[:, l:], ks_new], axis=1), jnp.concatenate([cache_v[:, l:], vs_new], axis=1))


def kernel(x_prompt, x_sample, state_conv, cache_k, cache_v, mix_norm_g, mlp_norm_g, w_up, w_down, conv_w_in, conv_w, conv_w_out, kv_norm_g, w_kv, k_norm_g, w_q, q_norm_g, sinks, w_o):
    return _forward(x_prompt, x_sample, state_conv, cache_k, cache_v, mix_norm_g, mlp_norm_g, w_up, w_down,
                    conv_w_in, conv_w, conv_w_out, kv_norm_g, w_kv, k_norm_g, w_q, q_norm_g, sinks, w_o,
                    tm_mlp=512, tf=2048, tf_cast=512, tm_conv=512, tn=512, tn_cast=256, tm_attn=512, tm_attn_s=256, tm_kv=1024)
```

```python
import functools

import jax
import jax.numpy as jnp
from jax import lax
from jax.experimental import pallas as pl
from jax.experimental.pallas import tpu as pltpu

EPS = 1e-6
CHUNK = 64
WINDOW = 128
HEAD_DIM = 64
N_KV_HEADS = 4
GROUP = 8
ROT_DIM = 16
ROPE_THETA = 500000.0
PAST_LEN = 2048
SCALE = HEAD_DIM ** -0.5
LOG2E = 1.4426950408889634

LANES = 128
KEYS = 2 * WINDOW
KV_ROW_GROUP = 256
CONV_SUBCHUNK = 256
VMEM_LIMIT_CAP = 56 * 2 ** 20

F32 = jnp.float32
BF16 = jnp.bfloat16


def _vmem_limit(nbytes):
    return int(min(VMEM_LIMIT_CAP, max(32 * 2 ** 20, nbytes * 5 // 4 + 4 * 2 ** 20)))


def _rms(x, g):
    return x * lax.rsqrt(jnp.mean(x * x, axis=-1, keepdims=True) + EPS) * g


def _half_mask(shape):
    return lax.broadcasted_iota(jnp.int32, shape, len(shape) - 1) < HEAD_DIM


def _head_norm_rope(t, gain, cos, sneg, spos):
    lo = _half_mask(t.shape)
    sq = t * t
    s_lo = jnp.sum(jnp.where(lo, sq, 0.0), axis=-1, keepdims=True)
    s_hi = jnp.sum(jnp.where(lo, 0.0, sq), axis=-1, keepdims=True)
    inv = jnp.where(lo, lax.rsqrt(s_lo / HEAD_DIM + EPS), lax.rsqrt(s_hi / HEAD_DIM + EPS))
    tn = t * inv * gain
    half = ROT_DIM // 2
    return tn * cos + pltpu.roll(tn, LANES - half, 1) * sneg + pltpu.roll(tn, half, 1) * spos


def _mlp_body(x_ref, g_ref, wu_ref, wd_ref, o_ref, xn_ref):
    def contribution():
        h = jnp.dot(xn_ref[...], wu_ref[...], preferred_element_type=F32)
        h = jnp.square(jnp.maximum(h, 0.0)).astype(BF16)
        return jnp.dot(h, wd_ref[...], preferred_element_type=F32)

    @pl.when(pl.program_id(1) == 0)
    def _():
        xn_ref[...] = _rms(x_ref[...], g_ref[...]).astype(BF16)
        o_ref[...] = x_ref[...] + contribution()

    @pl.when(pl.program_id(1) > 0)
    def _():
        o_ref[...] += contribution()


def _mlp_cast_body(x_ref, g_ref, wu_ref, wd_ref, o_ref, wub_ref, wdb_ref, xn_ref):
    def contribution():
        wu = wu_ref[...].astype(BF16)
        wd = wd_ref[...].astype(BF16)
        wub_ref[...] = wu
        wdb_ref[...] = wd
        h = jnp.dot(xn_ref[...], wu, preferred_element_type=F32)
        h = jnp.square(jnp.maximum(h, 0.0)).astype(BF16)
        return jnp.dot(h, wd, preferred_element_type=F32)

    @pl.when(pl.program_id(0) == 0)
    def _():
        xn_ref[...] = _rms(x_ref[...], g_ref[...]).astype(BF16)
        o_ref[...] = x_ref[...] + contribution()

    @pl.when(pl.program_id(0) > 0)
    def _():
        o_ref[...] += contribution()


def _mlp_cast(x, g, w_up, w_down, layer, *, tf):
    t, d = x.shape
    f = w_up.shape[2]
    tf = min(tf, f)
    nbytes = 2 * t * d * 4 + t * d * 2 + 2 * 2 * d * tf * (4 + 2) + t * tf * 6
    return pl.pallas_call(
        _mlp_cast_body,
        grid=(f // tf,),
        in_specs=[
            pl.BlockSpec((t, d), lambda j: (0, 0), pipeline_mode=pl.Buffered(1)),
            pl.BlockSpec((None, 1, d), lambda j: (layer, 0, 0)),
            pl.BlockSpec((None, d, tf), lambda j: (layer, 0, j)),
            pl.BlockSpec((None, tf, d), lambda j: (layer, j, 0)),
        ],
        out_specs=[
            pl.BlockSpec((t, d), lambda j: (0, 0), pipeline_mode=pl.Buffered(1)),
            pl.BlockSpec((d, tf), lambda j: (0, j)),
            pl.BlockSpec((tf, d), lambda j: (j, 0)),
        ],
        out_shape=[
            jax.ShapeDtypeStruct((t, d), F32),
            jax.ShapeDtypeStruct((d, f), BF16),
            jax.ShapeDtypeStruct((f, d), BF16),
        ],
        scratch_shapes=[pltpu.VMEM((t, d), BF16)],
        compiler_params=pltpu.CompilerParams(
            dimension_semantics=("arbitrary",), vmem_limit_bytes=_vmem_limit(nbytes)),
        name="mlp_cast",
    )(x, g, w_up, w_down)


def _mlp(x, g, w_up, w_down, layer, *, tm, tf):
    t, d = x.shape
    f = w_up.shape[1]
    tm, tf = min(tm, t), min(tf, f)
    nbytes = 2 * (2 * tm * d * 4 + 2 * d * tf * 2) + tm * d * 2 + tm * tf * 6
    return pl.pallas_call(
        _mlp_body,
        grid=(t // tm, f // tf),
        in_specs=[
            pl.BlockSpec((tm, d), lambda i, j: (i, 0)),
            pl.BlockSpec((None, 1, d), lambda i, j: (layer, 0, 0)),
            pl.BlockSpec((d, tf), lambda i, j: (0, j)),
            pl.BlockSpec((tf, d), lambda i, j: (j, 0)),
        ],
        out_specs=pl.BlockSpec((tm, d), lambda i, j: (i, 0)),
        out_shape=jax.ShapeDtypeStruct((t, d), F32),
        scratch_shapes=[pltpu.VMEM((tm, d), BF16)],
        compiler_params=pltpu.CompilerParams(
            dimension_semantics=("arbitrary", "arbitrary"), vmem_limit_bytes=_vmem_limit(nbytes)),
        name="mlp",
    )(x, g, w_up, w_down)


def _conv_body(*refs, seg, tn, cast):
    if seg is None:
        x_ref, g_ref, wb_ref, wc_ref, wu_ref, cw_ref, wout_ref, o_ref, zl_ref, *rest = refs
    else:
        x_ref, g_ref, wb_ref, wc_ref, wu_ref, cw_ref, wout_ref, st_ref, o_ref, zl_ref, *rest = refs
    xn_ref = rest[-1]
    i, j = pl.program_id(0), pl.program_id(1)

    if seg is None:
        @pl.when(i == 0)
        def _():
            zl_ref[j] = jnp.zeros((2, tn), F32)

    tm = x_ref.shape[0]
    tc = min(tn, CONV_SUBCHUNK)

    def weights():
        if not cast:
            return wb_ref, wc_ref, wu_ref, wout_ref
        ws = [r[...].astype(BF16) for r in (wb_ref, wc_ref, wu_ref, wout_ref)]
        for dst, w in zip(rest[:4], ws):
            dst[...] = w
        return ws

    def project(c, wb, wc, wu):
        cs = slice(c * tc, (c + 1) * tc)
        xn = xn_ref[...]
        z = jnp.dot(xn, wc[:, cs], preferred_element_type=F32) * jnp.dot(xn, wu[:, cs], preferred_element_type=F32)
        return jnp.dot(xn, wb[:, cs], preferred_element_type=F32), z

    def gated_conv(gate_b, z, c):
        cs = slice(c * tc, (c + 1) * tc)
        row = lax.broadcasted_iota(jnp.int32, z.shape, 0)
        r1 = pltpu.roll(z, 1, 0)
        r2 = pltpu.roll(z, 2, 0)
        if seg is None:
            prev = zl_ref[j, :, cs]
            p0, p1 = prev[0:1, :], prev[1:2, :]
            zl_ref[j, :, cs] = z[tm - 2:, :]
        else:
            nb = tm // seg
            st = st_ref[:, :, cs]
            p0 = jnp.broadcast_to(st[:, 0:1, :], (nb, seg, tc)).reshape(tm, tc)
            p1 = jnp.broadcast_to(st[:, 1:2, :], (nb, seg, tc)).reshape(tm, tc)
            row = row % seg
            zl_ref[:, :, cs] = z.reshape(nb, seg, tc)[:, seg - 2:, :]
        zp1 = jnp.where(row == 0, p1, r1)
        zp2 = jnp.where(row == 0, p0, jnp.where(row == 1, p1, r2))
        cw = cw_ref[:, cs]
        conv = zp2 * cw[0:1, :] + zp1 * cw[1:2, :] + z * cw[2:3, :]
        return (gate_b * conv).astype(BF16)

    def contribution():
        nc = tn // tc
        wb, wc, wu, wout = weights()
        ys = []
        pending = project(0, wb, wc, wu)
        for c in range(nc):
            nxt = project(c + 1, wb, wc, wu) if c + 1 < nc else None
            ys.append(gated_conv(*pending, c))
            pending = nxt
        return jnp.dot(jnp.concatenate(ys, axis=1), wout[...], preferred_element_type=F32)

    @pl.when(j == 0)
    def _():
        xn_ref[...] = _rms(x_ref[...], g_ref[...]).astype(BF16)
        o_ref[...] = x_ref[...] + contribution()

    @pl.when(j > 0)
    def _():
        o_ref[...] += contribution()


def _conv_mixer(x, g, w_in, cw, w_out, state, layer, *, tm, tn, seg):
    t, d = x.shape
    cast = w_in is not None
    tm = t if cast else min(tm, t)
    nj = d // tn
    once = dict(pipeline_mode=pl.Buffered(1)) if cast else {}
    in_specs = [
        pl.BlockSpec((tm, d), lambda i, j: (i, 0), **once),
        pl.BlockSpec((None, 1, d), lambda i, j: (layer, 0, 0)),
    ]
    if cast:
        in_specs += [
            pl.BlockSpec((None, d, tn), lambda i, j: (layer, 0, j)),
            pl.BlockSpec((None, d, tn), lambda i, j: (layer, 0, nj + j)),
            pl.BlockSpec((None, d, tn), lambda i, j: (layer, 0, 2 * nj + j)),
            pl.BlockSpec((None, 3, tn), lambda i, j: (layer, 0, j)),
            pl.BlockSpec((None, tn, d), lambda i, j: (layer, j, 0)),
        ]
        args = [x, g, w_in, w_in, w_in, cw, w_out]
    else:
        in_specs += [
            pl.BlockSpec((d, tn), lambda i, j: (0, j)),
            pl.BlockSpec((d, tn), lambda i, j: (0, j)),
            pl.BlockSpec((d, tn), lambda i, j: (0, j)),
            pl.BlockSpec((None, 3, tn), lambda i, j: (layer, 0, j)),
            pl.BlockSpec((tn, d), lambda i, j: (j, 0)),
        ]
        args = [x, g, *w_out[:3], cw, w_out[3]]
    if seg is None:
        zl_shape = (nj, 2, tn)
        zl_spec = pl.BlockSpec((nj, 2, tn), lambda i, j: (0, 0, 0))
    else:
        nb = tm // seg
        zl_shape = (t // seg, 2, d)
        zl_spec = pl.BlockSpec((nb, 2, tn), lambda i, j: (i, 0, j))
        in_specs.append(pl.BlockSpec((None, nb, 2, tn), lambda i, j: (layer, i, 0, j)))
        args.append(state)
    out_specs = [pl.BlockSpec((tm, d), lambda i, j: (i, 0), **once), zl_spec]
    out_shape = [jax.ShapeDtypeStruct((t, d), F32), jax.ShapeDtypeStruct(zl_shape, F32)]
    if cast:
        out_specs += [pl.BlockSpec((d, tn), lambda i, j: (0, j))] * 3 + [pl.BlockSpec((tn, d), lambda i, j: (j, 0))]
        out_shape += [jax.ShapeDtypeStruct((d, d), BF16)] * 4
        nbytes = 2 * tm * d * 4 + 2 * 4 * d * tn * (4 + 2) + tm * d * 2 + tm * tn * 4 * 8
    else:
        nbytes = 2 * (2 * tm * d * 4 + d * 3 * tn * 2 + tn * d * 2) + tm * d * 2 + tm * tn * 4 * 8
    y, zl, *w_b = pl.pallas_call(
        functools.partial(_conv_body, seg=seg, tn=tn, cast=cast),
        grid=(t // tm, nj),
        in_specs=in_specs,
        out_specs=out_specs,
        out_shape=out_shape,
        scratch_shapes=[pltpu.VMEM((tm, d), BF16)],
        compiler_params=pltpu.CompilerParams(
            dimension_semantics=("arbitrary", "arbitrary"), vmem_limit_bytes=_vmem_limit(nbytes)),
        name="conv_mixer_cast" if cast else "conv_mixer",
    )(*args)
    if seg is None:
        zl = zl.transpose(1, 0, 2).reshape(1, 2, d)
    return (y, zl, tuple(w_b)) if cast else (y, zl)


def _kv_body(x_ref, g_ref, wkv_ref, kg_ref, cos_ref, sneg_ref, spos_ref, k_ref, v_ref, ka_ref, va_ref, *, dup):
    tm = x_ref.shape[0]
    nkv = N_KV_HEADS * HEAD_DIM
    gr = min(tm, KV_ROW_GROUP)
    lo = _half_mask((gr, LANES))

    def normed(r):
        return _rms(x_ref[r * gr:(r + 1) * gr, :], g_ref[...]).astype(BF16)

    xn = normed(0)
    for r in range(tm // gr):
        rs = slice(r * gr, (r + 1) * gr)
        xn_next = normed(r + 1) if (r + 1) * gr < tm else None
        kv = jnp.dot(xn, wkv_ref[...], preferred_element_type=F32)
        for p in range(nkv // LANES):
            sl = slice(p * LANES, (p + 1) * LANES)
            kr = _head_norm_rope(kv[:, sl], kg_ref[...], cos_ref[rs, :], sneg_ref[rs, :], spos_ref[rs, :])
            vr = kv[:, nkv + p * LANES: nkv + (p + 1) * LANES]
            k_ref[rs, sl] = kr
            v_ref[rs, sl] = vr
            if dup:
                for src, dst in ((kr, ka_ref), (vr, va_ref)):
                    sw = pltpu.roll(src, HEAD_DIM, 1)
                    dst[2 * p, rs, :] = jnp.where(lo, src, sw).astype(BF16)
                    dst[2 * p + 1, rs, :] = jnp.where(lo, sw, src).astype(BF16)
            else:
                ka_ref[rs, sl] = kr.astype(BF16)
                va_ref[sl, rs] = vr.T.astype(BF16)
        xn = xn_next


def _shared_kv(x, g, w_kv, kg, rope, *, tm, dup):
    t, d = x.shape
    tm = min(tm, t)
    nkv = N_KV_HEADS * HEAD_DIM
    row = lambda i: (i, 0)
    const = lambda i: (0, 0)
    if dup:
        f32_spec, f32_shape = pl.BlockSpec((tm, nkv), row), jax.ShapeDtypeStruct((t, nkv), F32)
        aux_specs = [pl.BlockSpec((N_KV_HEADS, tm, LANES), lambda i: (0, i, 0))] * 2
        aux_shapes = [jax.ShapeDtypeStruct((N_KV_HEADS, t, LANES), BF16)] * 2
    else:
        f32_spec, f32_shape = pl.BlockSpec((tm, nkv), const), jax.ShapeDtypeStruct((tm, nkv), F32)
        aux_specs = [pl.BlockSpec((tm, nkv), row), pl.BlockSpec((nkv, tm), lambda i: (0, i))]
        aux_shapes = [jax.ShapeDtypeStruct((t, nkv), BF16), jax.ShapeDtypeStruct((nkv, t), BF16)]
    nbytes = 2 * (tm * d * 4 + d * 2 * nkv * 2 + 5 * tm * nkv * 4) + tm * d * 8
    return pl.pallas_call(
        functools.partial(_kv_body, dup=dup),
        grid=(t // tm,),
        in_specs=[
            pl.BlockSpec((tm, d), row),
            pl.BlockSpec((1, d), const),
            pl.BlockSpec((d, 2 * nkv), const),
            pl.BlockSpec((1, LANES), const),
            pl.BlockSpec((tm, LANES), row),
            pl.BlockSpec((tm, LANES), row),
            pl.BlockSpec((tm, LANES), row),
        ],
        out_specs=[f32_spec, f32_spec] + aux_specs,
        out_shape=[f32_shape, f32_shape] + aux_shapes,
        compiler_params=pltpu.CompilerParams(
            dimension_semantics=("arbitrary",), vmem_limit_bytes=_vmem_limit(nbytes)),
        name="shared_kv",
    )(x, g, w_kv, kg, *rope)


def _project_q(x_ref, g_ref, wq_ref, qg_ref, cos_ref, sneg_ref, spos_ref, qe_ref, qo_ref):
    xn = _rms(x_ref[...], g_ref[...]).astype(BF16)
    q = jnp.dot(xn, wq_ref[...], preferred_element_type=F32)
    lo = _half_mask((q.shape[0], LANES))
    for p in range(q.shape[1] // LANES):
        sl = slice(p * LANES, (p + 1) * LANES)
        qr = _head_norm_rope(q[:, sl], qg_ref[...], cos_ref[...], sneg_ref[...], spos_ref[...]) * (SCALE * LOG2E)
        qe_ref[:, sl] = jnp.where(lo, qr, 0.0).astype(BF16)
        qo_ref[:, sl] = jnp.where(lo, 0.0, qr).astype(BF16)


def _attn_prompt_body(x_ref, g_ref, wqt_ref, qg_ref, cos_ref, sin_ref,
                      kprev_ref, kcur_ref, vprev_ref, vcur_ref, sink_ref, wo_ref,
                      o_ref, qt_ref, att_ref, kw_ref, vw_ref):
    i = pl.program_id(0)
    tm, d = x_ref.shape
    half = ROT_DIM // 2
    cos, sin = cos_ref[...], sin_ref[...]
    gain = jnp.concatenate([qg_ref[...]] * (tm // LANES), axis=1)
    rows = GROUP * HEAD_DIM

    nblk = tm // LANES
    dk = d // nblk

    x = x_ref[...]
    inv_rms = lax.rsqrt(jnp.mean(x * x, axis=-1, keepdims=True) + EPS)
    xn = {}

    def normed(c):
        if c not in xn:
            xn[c] = (x_ref[:, c * dk:(c + 1) * dk] * inv_rms * g_ref[:, c * dk:(c + 1) * dk]).astype(BF16)
        return xn[c]

    def project_q(kh, c):
        return lax.dot_general(wqt_ref[kh * rows:(kh + 1) * rows, c * dk:(c + 1) * dk], normed(c),
                               (((1,), (1,)), ((), ())), preferred_element_type=F32)

    def norm_rope_q(qt, kh, c):
        for j in range(c * (GROUP // nblk), (c + 1) * (GROUP // nblk)):
            t = qt[j * HEAD_DIM:(j + 1) * HEAD_DIM, :]
            tn = t * lax.rsqrt(jnp.sum(t * t, axis=0, keepdims=True) / HEAD_DIM + EPS) * gain
            x1, x2 = tn[:half], tn[half:ROT_DIM]
            h = GROUP * kh + j
            qt_ref[h * HEAD_DIM:(h + 1) * HEAD_DIM, :] = jnp.concatenate(
                [x1 * cos - x2 * sin, x2 * cos + x1 * sin, tn[ROT_DIM:]], axis=0).astype(BF16)


    kw_ref[:WINDOW, :] = kprev_ref[...]
    kw_ref[WINDOW:, :] = kcur_ref[...]
    vw_ref[:, :WINDOW] = vprev_ref[...]
    vw_ref[:, WINDOW:] = vcur_ref[...]

    first_query_chunk = lax.broadcasted_iota(jnp.int32, (CHUNK, LANES), 1) < CHUNK
    has_past = jnp.broadcast_to(i > 0, (CHUNK, LANES))
    zeros = jnp.zeros((HEAD_DIM, LANES), BF16)

    def scores_t(p, kh):
        qs = slice(p * LANES, (p + 1) * LANES)
        kwin = kw_ref[p * LANES:p * LANES + KEYS, (kh // 2) * LANES:(kh // 2 + 1) * LANES]
        rhs = jnp.concatenate(
            [jnp.concatenate([qt_ref[h * HEAD_DIM:(h + 1) * HEAD_DIM, qs], zeros] if kh % 2 == 0 else
                             [zeros, qt_ref[h * HEAD_DIM:(h + 1) * HEAD_DIM, qs]], axis=0)
             for h in range(GROUP * kh, GROUP * (kh + 1))], axis=1)
        return jnp.dot(kwin, rhs, preferred_element_type=F32)

    def softmax_t(st, p, kh):
        masks = [first_query_chunk & has_past if p == 0 else first_query_chunk, has_past if p == 0 else None,
                 None, ~first_query_chunk]
        sink_terms, pts = [], []
        for j in range(GROUP):
            s = jnp.concatenate(
                [st[c * CHUNK:(c + 1) * CHUNK, j * LANES:(j + 1) * LANES] if mask is None else
                 jnp.where(mask, st[c * CHUNK:(c + 1) * CHUNK, j * LANES:(j + 1) * LANES], -jnp.inf)
                 for c, mask in enumerate(masks)], axis=0)
            sink = sink_ref[GROUP * kh + j:GROUP * kh + j + 1, :]
            m = jnp.maximum(jnp.max(s, axis=0, keepdims=True), sink)
            sink_terms.append(jnp.exp2(sink - m))
            pts.append(jnp.exp2(s - m).astype(BF16))
        return jnp.concatenate(pts, axis=1), jnp.concatenate(sink_terms, axis=1)

    ones_rows = jnp.ones((16, KEYS), BF16)

    def weighted_values_t(p, kh, pt, sink_term):
        qs = slice(p * LANES, (p + 1) * LANES)
        v_ones = jnp.concatenate([vw_ref[kh * HEAD_DIM:(kh + 1) * HEAD_DIM, p * LANES:p * LANES + KEYS], ones_rows], axis=0)
        ot = jnp.dot(v_ones, pt, preferred_element_type=F32)
        ot = ot[:HEAD_DIM] * (1.0 / (ot[HEAD_DIM:HEAD_DIM + 1] + sink_term))
        for j in range(GROUP):
            h = GROUP * kh + j
            att_ref[h * HEAD_DIM:(h + 1) * HEAD_DIM, qs] = ot[:, j * LANES:(j + 1) * LANES].astype(BF16)

    def full_q(kh):
        qt = project_q(kh, 0)
        for c in range(1, nblk):
            qt = qt + project_q(kh, c)
        return qt

    qts = {0: full_q(0)}
    for c in range(nblk):
        piece = project_q(1, c)
        qts[1] = piece if c == 0 else qts[1] + piece
        norm_rope_q(qts[0], 0, c)
    for kh in range(N_KV_HEADS):
        st = scores_t(0, kh)
        pending = None
        for n in range(nblk):
            st_next = scores_t(n + 1, kh) if n + 1 < nblk else None
            if kh + 2 < N_KV_HEADS:
                piece = project_q(kh + 2, n)
                qts[kh + 2] = piece if n == 0 else qts[kh + 2] + piece
            pt, sink_term = softmax_t(st, n, kh)
            if kh + 1 < N_KV_HEADS:
                norm_rope_q(qts[kh + 1], kh + 1, n)
            if pending is not None:
                weighted_values_t(*pending)
            pending = (n, kh, pt, sink_term)
            st = st_next
        weighted_values_t(*pending)
    o_ref[...] = x_ref[...] + lax.dot_general(att_ref[...], wo_ref[...], (((0,), (0,)), ((), ())),
                                              preferred_element_type=F32)


def _attn_sample_body(x_ref, g_ref, wq_ref, qg_ref, cos_ref, sneg_ref, spos_ref,
                      kc_ref, knew_ref, vc_ref, vnew_ref, sink_ref, wo_ref,
                      o_ref, qe_ref, qo_ref, att_ref, *, seg):
    tm = x_ref.shape[0]
    _project_q(x_ref, g_ref, wq_ref, qg_ref, cos_ref, sneg_ref, spos_ref, qe_ref, qo_ref)
    nkeys = WINDOW + seg
    valid = lax.broadcasted_iota(jnp.int32, (1, KEYS), 1) < nkeys
    pad = jnp.zeros((KEYS - nkeys, LANES), BF16)
    pairs = GROUP // 2
    lo = _half_mask((seg, LANES))

    def window(cache_ref, new_ref, b, kh):
        return jnp.concatenate([cache_ref[b, kh], new_ref[kh, b * seg:(b + 1) * seg, :], pad], axis=0)

    def head_cols(kh):
        return [slice((pairs * kh + j) * LANES, (pairs * kh + j + 1) * LANES) for j in range(pairs)]

    def scores(b, kh):
        rows = slice(b * seg, (b + 1) * seg)
        qcat = jnp.concatenate([qe_ref[rows, c] for c in head_cols(kh)] + [qo_ref[rows, c] for c in head_cols(kh)], axis=0)
        return lax.dot_general(qcat, window(kc_ref, knew_ref, b, kh), (((1,), (1,)), ((), ())),
                               preferred_element_type=F32)

    def softmax(s, kh):
        sink = sink_ref[kh]
        s = jnp.where(valid, s, -jnp.inf)
        m = jnp.maximum(jnp.max(s, axis=-1, keepdims=True), sink)
        e = jnp.exp2(s - m)
        return e.astype(BF16), jnp.sum(e, axis=-1, keepdims=True) + jnp.exp2(sink - m)

    def weighted_values(b, kh, p, den):
        o = jnp.dot(p, window(vc_ref, vnew_ref, b, kh), preferred_element_type=F32) / den
        for j, c in enumerate(head_cols(kh)):
            att_ref[b * seg:(b + 1) * seg, c] = jnp.where(
                lo, o[j * seg:(j + 1) * seg], o[(pairs + j) * seg:(pairs + j + 1) * seg]).astype(BF16)

    blocks = [(b, kh) for b in range(tm // seg) for kh in range(N_KV_HEADS)]
    s = scores(*blocks[0])
    pending = None
    for n, blk in enumerate(blocks):
        s_next = scores(*blocks[n + 1]) if n + 1 < len(blocks) else None
        p, den = softmax(s, blk[1])
        if pending is not None:
            weighted_values(*pending)
        pending = (*blk, p, den)
        s = s_next
    weighted_values(*pending)
    o_ref[...] = x_ref[...] + jnp.dot(att_ref[...], wo_ref[...], preferred_element_type=F32)


def _attn_mixer_sample(x, g, w_q, qg, rope, k2, v2, cache, sink_col, w_o, layer, blayer, *, tm, seg):
    t, d = x.shape
    tm = min(tm, t)
    nb = tm // seg
    row = lambda i: (i, 0)
    const = lambda i: (0, 0)
    resident = dict(pipeline_mode=pl.Buffered(1))
    cur = pl.BlockSpec((N_KV_HEADS, tm, LANES), lambda i: (0, i, 0))
    cspec = pl.BlockSpec((nb, N_KV_HEADS, WINDOW, LANES), lambda i: (i, 0, 0, 0))
    in_specs = [
        pl.BlockSpec((tm, d), row),
        pl.BlockSpec((None, 1, d), lambda i: (layer, 0, 0)),
        pl.BlockSpec((None, d, d), lambda i: (blayer, 0, 0), **resident),
        pl.BlockSpec((1, LANES), const),
        pl.BlockSpec((tm, LANES), row),
        pl.BlockSpec((tm, LANES), row),
        pl.BlockSpec((tm, LANES), row),
        cspec, cur, cspec, cur,
        pl.BlockSpec((N_KV_HEADS, GROUP * seg, 1), lambda i: (0, 0, 0)),
        pl.BlockSpec((None, d, d), lambda i: (blayer, 0, 0), **resident),
    ]
    nbytes = 2 * d * d * 2 + 4 * tm * d * 4 + tm * d * (4 + 3 * 2) + 8 * tm * LANES * 4 * 2 + 4 * 2 ** 20
    return pl.pallas_call(
        functools.partial(_attn_sample_body, seg=seg),
        grid=(t // tm,),
        in_specs=in_specs,
        out_specs=pl.BlockSpec((tm, d), row),
        out_shape=jax.ShapeDtypeStruct((t, d), F32),
        scratch_shapes=[pltpu.VMEM((tm, d), BF16), pltpu.VMEM((tm, d), BF16), pltpu.VMEM((tm, d), BF16)],
        compiler_params=pltpu.CompilerParams(
            dimension_semantics=("arbitrary",), vmem_limit_bytes=_vmem_limit(nbytes)),
        name="attn_mixer_sample",
    )(x, g, w_q, qg, *rope, cache[0], k2, cache[1], v2, sink_col, w_o)


def _attn_mixer_prompt(x, g, w_qt, qg, cos_t, sin_t, kb, vt, sink_rows, w_o, layer, blayer, *, tm):
    t, d = x.shape
    tm = min(tm, t)
    nkv = N_KV_HEADS * HEAD_DIM
    nh = d // HEAD_DIM
    half = ROT_DIM // 2
    prev_blk = lambda i: jnp.maximum(i * (tm // WINDOW) - 1, 0)
    resident = dict(pipeline_mode=pl.Buffered(1))
    in_specs = [
        pl.BlockSpec((tm, d), lambda i: (i, 0)),
        pl.BlockSpec((None, 1, d), lambda i: (layer, 0, 0)),
        pl.BlockSpec((None, d, d), lambda i: (blayer, 0, 0), **resident),
        pl.BlockSpec((HEAD_DIM, LANES), lambda i: (0, 0)),
        pl.BlockSpec((half, tm), lambda i: (0, i)),
        pl.BlockSpec((half, tm), lambda i: (0, i)),
        pl.BlockSpec((WINDOW, nkv), lambda i: (prev_blk(i), 0)),
        pl.BlockSpec((tm, nkv), lambda i: (i, 0)),
        pl.BlockSpec((nkv, WINDOW), lambda i: (0, prev_blk(i))),
        pl.BlockSpec((nkv, tm), lambda i: (0, i)),
        pl.BlockSpec((nh, LANES), lambda i: (0, 0)),
        pl.BlockSpec((None, d, d), lambda i: (blayer, 0, 0), **resident),
    ]
    nbytes = (2 * d * d * 2 + 4 * tm * d * 4 + 3 * tm * d * 4 + 2 * tm * d * 2 + 2 * KEYS * GROUP * LANES * 2
              + KEYS * GROUP * LANES * 4 * 2 + 4 * (WINDOW + tm) * nkv * 2)
    return pl.pallas_call(
        _attn_prompt_body,
        grid=(t // tm,),
        in_specs=in_specs,
        out_specs=pl.BlockSpec((tm, d), lambda i: (i, 0)),
        out_shape=jax.ShapeDtypeStruct((t, d), F32),
        scratch_shapes=[pltpu.VMEM((d, tm), BF16), pltpu.VMEM((d, tm), BF16),
                        pltpu.VMEM((WINDOW + tm, nkv), BF16), pltpu.VMEM((nkv, WINDOW + tm), BF16)],
        compiler_params=pltpu.CompilerParams(
            dimension_semantics=("arbitrary",), vmem_limit_bytes=_vmem_limit(nbytes)),
        name="attn_mixer_prompt",
    )(x, g, w_qt, qg, cos_t, sin_t, kb, kb, vt, vt, sink_rows, w_o)


def _rope_tables(pos):
    half = ROT_DIM // 2
    inv = ROPE_THETA ** (-jnp.arange(half, dtype=F32) / half)
    ang = pos.astype(F32)[:, None] * inv[None, :]
    cos = jnp.tile(jnp.cos(ang), (1, LANES // half))
    sin = jnp.tile(jnp.sin(ang), (1, LANES // half))
    dim = jnp.arange(LANES) % HEAD_DIM
    c = jnp.where(dim < ROT_DIM, cos, 1.0)
    sneg = jnp.where(dim < half, -sin, 0.0)
    spos = jnp.where((dim >= half) & (dim < ROT_DIM), sin, 0.0)
    return c, sneg, spos


def _sink_column(sinks_l, rows_per_head):
    s = (sinks_l.astype(F32) * LOG2E).reshape(N_KV_HEADS, GROUP // 2, 2).transpose(0, 2, 1)
    return jnp.repeat(s.reshape(N_KV_HEADS, GROUP), rows_per_head, axis=1)[..., None]


def _dup_heads(t):
    t = t.transpose(0, 2, 1, 3)
    return jnp.concatenate([t, t], axis=-1).astype(BF16)


def _forward(x_prompt, x_sample, state_conv, cache_k, cache_v, mix_norm_g, mlp_norm_g, w_up, w_down,
             conv_w_in, conv_w, conv_w_out, kv_norm_g, w_kv, k_norm_g, w_q, q_norm_g, sinks, w_o,
             *, tm_mlp, tf, tf_cast, tm_conv, tn, tn_cast, tm_attn, tm_attn_s, tm_kv):
    _, s, d = x_prompt.shape
    b, l, _ = x_sample.shape
    n_a = conv_w_in.shape[0]
    depth = w_up.shape[0]
    xp = x_prompt.reshape(s, d)
    xs = x_sample.reshape(b * l, d)

    w_kv_b, w_q_b, w_o_b = w_kv.astype(BF16), w_q.astype(BF16), w_o.astype(BF16)
    w_qt_b = w_q_b.transpose(0, 2, 1)
    mix_g = mix_norm_g.reshape(depth, 1, d)
    mlp_g = mlp_norm_g.reshape(depth, 1, d)

    half = ROT_DIM // 2
    ang_t = (ROPE_THETA ** (-jnp.arange(half, dtype=F32) / half))[:, None] * jnp.arange(s).astype(F32)[None, :]
    cos_t, sin_t = jnp.cos(ang_t), jnp.sin(ang_t)
    rope_p = _rope_tables(jnp.arange(s))
    rope_s = _rope_tables(jnp.tile(PAST_LEN + jnp.arange(l), b))
    kg = jnp.tile(k_norm_g.astype(F32), LANES // HEAD_DIM).reshape(1, LANES)

    conv_p, conv_s = [], []
    for i in range(depth):
        if i < n_a:
            xs, cs, conv_wb = _conv_mixer(xs, mix_g, conv_w_in, conv_w, conv_w_out, state_conv, i, tm=None, tn=tn_cast, seg=l)
            xp, cp = _conv_mixer(xp, mix_g, None, conv_w, conv_wb, None, i, tm=tm_conv, tn=tn, seg=None)
            conv_p.append(cp)
            conv_s.append(cs)
        else:
            if i == n_a:
                kp, vp, kbp, vtp = _shared_kv(xp, kv_norm_g.reshape(1, d), w_kv_b, kg, rope_p, tm=tm_kv, dup=False)
                ks, vs, k2s, v2s = _shared_kv(xs, kv_norm_g.reshape(1, d), w_kv_b, kg, rope_s, tm=tm_kv, dup=True)
                cache2 = (_dup_heads(cache_k), _dup_heads(cache_v))
            j = i - n_a
            qg = jnp.tile(q_norm_g[j].astype(F32), LANES // HEAD_DIM).reshape(1, LANES)
            qg_t = jnp.broadcast_to((q_norm_g[j].astype(F32) * (SCALE * LOG2E))[:, None], (HEAD_DIM, LANES))
            sink_rows = jnp.broadcast_to((sinks[j].astype(F32) * LOG2E)[:, None], (sinks.shape[1], LANES))
            xp = _attn_mixer_prompt(xp, mix_g, w_qt_b, qg_t, cos_t, sin_t, kbp, vtp, sink_rows, w_o_b, i, j, tm=tm_attn)
            xs = _attn_mixer_sample(xs, mix_g, w_q_b, qg, rope_s, k2s, v2s, cache2, _sink_column(sinks[j], l),
                                    w_o_b, i, j, tm=tm_attn_s, seg=l)
        xs, w_up_b, w_down_b = _mlp_cast(xs, mlp_g, w_up, w_down, i, tf=tf_cast)
        xp = _mlp(xp, mlp_g, w_up_b, w_down_b, i, tm=tm_mlp, tf=tf)

    hd = (N_KV_HEADS, HEAD_DIM)
    ks_new = ks.reshape(b, l, *hd)
    vs_new = vs.reshape(b, l, *hd)
    return (xp.reshape(1, s, d), xs.reshape(b, l, d), jnp.stack(conv_p), jnp.stack(conv_s),
            kp[-WINDOW:].reshape(1, WINDOW, *hd), vp[-WINDOW:].reshape(1, WINDOW, *hd),
            jnp.concatenate([cache_k[:, l:], ks_new], axis=1), jnp.concatenate([cache_v[:, l:], vs_new], axis=1))


def kernel(x_prompt, x_sample, state_conv, cache_k, cache_v, mix_norm_g, mlp_norm_g, w_up, w_down, conv_w_in, conv_w, conv_w_out, kv_norm_g, w_kv, k_norm_g, w_q, q_norm_g, sinks, w_o):
    return _forward(x_prompt, x_sample, state_conv, cache_k, cache_v, mix_norm_g, mlp_norm_g, w_up, w_down,
                    conv_w_in, conv_w, conv_w_out, kv_norm_g, w_kv, k_norm_g, w_q, q_norm_g, sinks, w_o,
                    tm_mlp=512, tf=2048, tf_cast=512, tm_conv=512, tn=1024, tn_cast=256, tm_attn=512, tm_attn_s=256, tm_kv=1024)
```

```python
import functools

import jax
import jax.numpy as jnp
from jax import lax
from jax.experimental import pallas as pl
from jax.experimental.pallas import tpu as pltpu

EPS = 1e-6
CHUNK = 64
WINDOW = 128
HEAD_DIM = 64
N_KV_HEADS = 4
GROUP = 8
ROT_DIM = 16
ROPE_THETA = 500000.0
PAST_LEN = 2048
SCALE = HEAD_DIM ** -0.5
LOG2E = 1.4426950408889634

LANES = 128
KEYS = 2 * WINDOW
KV_ROW_GROUP = 256
CONV_SUBCHUNK = 256
VMEM_LIMIT_CAP = 56 * 2 ** 20

F32 = jnp.float32
BF16 = jnp.bfloat16


def _vmem_limit(nbytes):
    return int(min(VMEM_LIMIT_CAP, max(32 * 2 ** 20, nbytes * 5 // 4 + 4 * 2 ** 20)))


def _rms(x, g):
    return x * lax.rsqrt(jnp.mean(x * x, axis=-1, keepdims=True) + EPS) * g


def _half_mask(shape):
    return lax.broadcasted_iota(jnp.int32, shape, len(shape) - 1) < HEAD_DIM


def _head_norm_rope(t, gain, cos, sneg, spos):
    lo = _half_mask(t.shape)
    sq = t * t
    s_lo = jnp.sum(jnp.where(lo, sq, 0.0), axis=-1, keepdims=True)
    s_hi = jnp.sum(jnp.where(lo, 0.0, sq), axis=-1, keepdims=True)
    inv = jnp.where(lo, lax.rsqrt(s_lo / HEAD_DIM + EPS), lax.rsqrt(s_hi / HEAD_DIM + EPS))
    tn = t * inv * gain
    half = ROT_DIM // 2
    return tn * cos + pltpu.roll(tn, LANES - half, 1) * sneg + pltpu.roll(tn, half, 1) * spos


def _mlp_body(x_ref, g_ref, wu_ref, wd_ref, o_ref, xn_ref):
    def contribution():
        h = jnp.dot(xn_ref[...], wu_ref[...], preferred_element_type=F32)
        h = jnp.square(jnp.maximum(h, 0.0)).astype(BF16)
        return jnp.dot(h, wd_ref[...], preferred_element_type=F32)

    @pl.when(pl.program_id(1) == 0)
    def _():
        xn_ref[...] = _rms(x_ref[...], g_ref[...]).astype(BF16)
        o_ref[...] = x_ref[...] + contribution()

    @pl.when(pl.program_id(1) > 0)
    def _():
        o_ref[...] += contribution()


def _mlp_cast_body(x_ref, g_ref, wu_ref, wd_ref, o_ref, wub_ref, wdb_ref, xn_ref):
    def contribution():
        wu = wu_ref[...].astype(BF16)
        wd = wd_ref[...].astype(BF16)
        wub_ref[...] = wu
        wdb_ref[...] = wd
        h = jnp.dot(xn_ref[...], wu, preferred_element_type=F32)
        h = jnp.square(jnp.maximum(h, 0.0)).astype(BF16)
        return jnp.dot(h, wd, preferred_element_type=F32)

    @pl.when(pl.program_id(0) == 0)
    def _():
        xn_ref[...] = _rms(x_ref[...], g_ref[...]).astype(BF16)
        o_ref[...] = x_ref[...] + contribution()

    @pl.when(pl.program_id(0) > 0)
    def _():
        o_ref[...] += contribution()


def _mlp_cast(x, g, w_up, w_down, layer, *, tf):
    t, d = x.shape
    f = w_up.shape[2]
    tf = min(tf, f)
    nbytes = 2 * t * d * 4 + t * d * 2 + 2 * 2 * d * tf * (4 + 2) + t * tf * 6
    return pl.pallas_call(
        _mlp_cast_body,
        grid=(f // tf,),
        in_specs=[
            pl.BlockSpec((t, d), lambda j: (0, 0), pipeline_mode=pl.Buffered(1)),
            pl.BlockSpec((None, 1, d), lambda j: (layer, 0, 0)),
            pl.BlockSpec((None, d, tf), lambda j: (layer, 0, j)),
            pl.BlockSpec((None, tf, d), lambda j: (layer, j, 0)),
        ],
        out_specs=[
            pl.BlockSpec((t, d), lambda j: (0, 0), pipeline_mode=pl.Buffered(1)),
            pl.BlockSpec((d, tf), lambda j: (0, j)),
            pl.BlockSpec((tf, d), lambda j: (j, 0)),
        ],
        out_shape=[
            jax.ShapeDtypeStruct((t, d), F32),
            jax.ShapeDtypeStruct((d, f), BF16),
            jax.ShapeDtypeStruct((f, d), BF16),
        ],
        scratch_shapes=[pltpu.VMEM((t, d), BF16)],
        compiler_params=pltpu.CompilerParams(
            dimension_semantics=("arbitrary",), vmem_limit_bytes=_vmem_limit(nbytes)),
        name="mlp_cast",
    )(x, g, w_up, w_down)


def _mlp(x, g, w_up, w_down, layer, *, tm, tf):
    t, d = x.shape
    f = w_up.shape[1]
    tm, tf = min(tm, t), min(tf, f)
    nbytes = 2 * (2 * tm * d * 4 + 2 * d * tf * 2) + tm * d * 2 + tm * tf * 6
    return pl.pallas_call(
        _mlp_body,
        grid=(t // tm, f // tf),
        in_specs=[
            pl.BlockSpec((tm, d), lambda i, j: (i, 0)),
            pl.BlockSpec((None, 1, d), lambda i, j: (layer, 0, 0)),
            pl.BlockSpec((d, tf), lambda i, j: (0, j)),
            pl.BlockSpec((tf, d), lambda i, j: (j, 0)),
        ],
        out_specs=pl.BlockSpec((tm, d), lambda i, j: (i, 0)),
        out_shape=jax.ShapeDtypeStruct((t, d), F32),
        scratch_shapes=[pltpu.VMEM((tm, d), BF16)],
        compiler_params=pltpu.CompilerParams(
            dimension_semantics=("arbitrary", "arbitrary"), vmem_limit_bytes=_vmem_limit(nbytes)),
        name="mlp",
    )(x, g, w_up, w_down)


def _conv_body(*refs, seg, tn, cast):
    if seg is None:
        x_ref, g_ref, wb_ref, wc_ref, wu_ref, cw_ref, wout_ref, o_ref, zl_ref, *rest = refs
    else:
        x_ref, g_ref, wb_ref, wc_ref, wu_ref, cw_ref, wout_ref, st_ref, o_ref, zl_ref, *rest = refs
    xn_ref = rest[-1]
    i, j = pl.program_id(0), pl.program_id(1)

    if seg is None:
        @pl.when(i == 0)
        def _():
            zl_ref[j] = jnp.zeros((2, tn), F32)

    tm = x_ref.shape[0]
    tc = min(tn, CONV_SUBCHUNK)

    def weights():
        if not cast:
            return wb_ref, wc_ref, wu_ref, wout_ref
        ws = [r[...].astype(BF16) for r in (wb_ref, wc_ref, wu_ref, wout_ref)]
        for dst, w in zip(rest[:4], ws):
            dst[...] = w
        return ws

    def project(c, wb, wc, wu):
        cs = slice(c * tc, (c + 1) * tc)
        xn = xn_ref[...]
        z = jnp.dot(xn, wc[:, cs], preferred_element_type=F32) * jnp.dot(xn, wu[:, cs], preferred_element_type=F32)
        return jnp.dot(xn, wb[:, cs], preferred_element_type=F32), z

    def gated_conv(gate_b, z, c):
        cs = slice(c * tc, (c + 1) * tc)
        row = lax.broadcasted_iota(jnp.int32, z.shape, 0)
        r1 = pltpu.roll(z, 1, 0)
        r2 = pltpu.roll(z, 2, 0)
        if seg is None:
            prev = zl_ref[j, :, cs]
            p0, p1 = prev[0:1, :], prev[1:2, :]
            zl_ref[j, :, cs] = z[tm - 2:, :]
        else:
            nb = tm // seg
            st = st_ref[:, :, cs]
            p0 = jnp.broadcast_to(st[:, 0:1, :], (nb, seg, tc)).reshape(tm, tc)
            p1 = jnp.broadcast_to(st[:, 1:2, :], (nb, seg, tc)).reshape(tm, tc)
            row = row % seg
            zl_ref[:, :, cs] = z.reshape(nb, seg, tc)[:, seg - 2:, :]
        zp1 = jnp.where(row == 0, p1, r1)
        zp2 = jnp.where(row == 0, p0, jnp.where(row == 1, p1, r2))
        cw = cw_ref[:, cs]
        conv = zp2 * cw[0:1, :] + zp1 * cw[1:2, :] + z * cw[2:3, :]
        return (gate_b * conv).astype(BF16)

    def contribution():
        nc = tn // tc
        wb, wc, wu, wout = weights()
        ys = []
        pending = project(0, wb, wc, wu)
        for c in range(nc):
            nxt = project(c + 1, wb, wc, wu) if c + 1 < nc else None
            ys.append(gated_conv(*pending, c))
            pending = nxt
        return jnp.dot(jnp.concatenate(ys, axis=1), wout[...], preferred_element_type=F32)

    @pl.when(j == 0)
    def _():
        xn_ref[...] = _rms(x_ref[...], g_ref[...]).astype(BF16)
        o_ref[...] = x_ref[...] + contribution()

    @pl.when(j > 0)
    def _():
        o_ref[...] += contribution()


def _conv_mixer(x, g, w_in, cw, w_out, state, layer, *, tm, tn, seg):
    t, d = x.shape
    cast = w_in is not None
    tm = t if cast else min(tm, t)
    nj = d // tn
    once = dict(pipeline_mode=pl.Buffered(1)) if cast else {}
    in_specs = [
        pl.BlockSpec((tm, d), lambda i, j: (i, 0), **once),
        pl.BlockSpec((None, 1, d), lambda i, j: (layer, 0, 0)),
    ]
    if cast:
        in_specs += [
            pl.BlockSpec((None, d, tn), lambda i, j: (layer, 0, j)),
            pl.BlockSpec((None, d, tn), lambda i, j: (layer, 0, nj + j)),
            pl.BlockSpec((None, d, tn), lambda i, j: (layer, 0, 2 * nj + j)),
            pl.BlockSpec((None, 3, tn), lambda i, j: (layer, 0, j)),
            pl.BlockSpec((None, tn, d), lambda i, j: (layer, j, 0)),
        ]
        args = [x, g, w_in, w_in, w_in, cw, w_out]
    else:
        in_specs += [
            pl.BlockSpec((d, tn), lambda i, j: (0, j)),
            pl.BlockSpec((d, tn), lambda i, j: (0, j)),
            pl.BlockSpec((d, tn), lambda i, j: (0, j)),
            pl.BlockSpec((None, 3, tn), lambda i, j: (layer, 0, j)),
            pl.BlockSpec((tn, d), lambda i, j: (j, 0)),
        ]
        args = [x, g, *w_out[:3], cw, w_out[3]]
    if seg is None:
        zl_shape = (nj, 2, tn)
        zl_spec = pl.BlockSpec((nj, 2, tn), lambda i, j: (0, 0, 0))
    else:
        nb = tm // seg
        zl_shape = (t // seg, 2, d)
        zl_spec = pl.BlockSpec((nb, 2, tn), lambda i, j: (i, 0, j))
        in_specs.append(pl.BlockSpec((None, nb, 2, tn), lambda i, j: (layer, i, 0, j)))
        args.append(state)
    out_specs = [pl.BlockSpec((tm, d), lambda i, j: (i, 0), **once), zl_spec]
    out_shape = [jax.ShapeDtypeStruct((t, d), F32), jax.ShapeDtypeStruct(zl_shape, F32)]
    if cast:
        out_specs += [pl.BlockSpec((d, tn), lambda i, j: (0, j))] * 3 + [pl.BlockSpec((tn, d), lambda i, j: (j, 0))]
        out_shape += [jax.ShapeDtypeStruct((d, d), BF16)] * 4
        nbytes = 2 * tm * d * 4 + 2 * 4 * d * tn * (4 + 2) + tm * d * 2 + tm * tn * 4 * 8
    else:
        nbytes = 2 * (2 * tm * d * 4 + d * 3 * tn * 2 + tn * d * 2) + tm * d * 2 + tm * tn * 4 * 8
    y, zl, *w_b = pl.pallas_call(
        functools.partial(_conv_body, seg=seg, tn=tn, cast=cast),
        grid=(t // tm, nj),
        in_specs=in_specs,
        out_specs=out_specs,
        out_shape=out_shape,
        scratch_shapes=[pltpu.VMEM((tm, d), BF16)],
        compiler_params=pltpu.CompilerParams(
            dimension_semantics=("arbitrary", "arbitrary"), vmem_limit_bytes=_vmem_limit(nbytes)),
        name="conv_mixer_cast" if cast else "conv_mixer",
    )(*args)
    if seg is None:
        zl = zl.transpose(1, 0, 2).reshape(1, 2, d)
    return (y, zl, tuple(w_b)) if cast else (y, zl)


def _kv_body(x_ref, g_ref, wkv_ref, kg_ref, cos_ref, sneg_ref, spos_ref, k_ref, v_ref, ka_ref, va_ref, *, dup):
    tm = x_ref.shape[0]
    nkv = N_KV_HEADS * HEAD_DIM
    gr = min(tm, KV_ROW_GROUP)
    lo = _half_mask((gr, LANES))

    def normed(r):
        return _rms(x_ref[r * gr:(r + 1) * gr, :], g_ref[...]).astype(BF16)

    xn = normed(0)
    for r in range(tm // gr):
        rs = slice(r * gr, (r + 1) * gr)
        xn_next = normed(r + 1) if (r + 1) * gr < tm else None
        kv = jnp.dot(xn, wkv_ref[...], preferred_element_type=F32)
        for p in range(nkv // LANES):
            sl = slice(p * LANES, (p + 1) * LANES)
            kr = _head_norm_rope(kv[:, sl], kg_ref[...], cos_ref[rs, :], sneg_ref[rs, :], spos_ref[rs, :])
            vr = kv[:, nkv + p * LANES: nkv + (p + 1) * LANES]
            k_ref[rs, sl] = kr
            v_ref[rs, sl] = vr
            if dup:
                for src, dst in ((kr, ka_ref), (vr, va_ref)):
                    sw = pltpu.roll(src, HEAD_DIM, 1)
                    dst[2 * p, rs, :] = jnp.where(lo, src, sw).astype(BF16)
                    dst[2 * p + 1, rs, :] = jnp.where(lo, sw, src).astype(BF16)
            else:
                ka_ref[rs, sl] = kr.astype(BF16)
                va_ref[sl, rs] = vr.T.astype(BF16)
        xn = xn_next


def _shared_kv(x, g, w_kv, kg, rope, *, tm, dup):
    t, d = x.shape
    tm = min(tm, t)
    nkv = N_KV_HEADS * HEAD_DIM
    row = lambda i: (i, 0)
    const = lambda i: (0, 0)
    if dup:
        f32_spec, f32_shape = pl.BlockSpec((tm, nkv), row), jax.ShapeDtypeStruct((t, nkv), F32)
        aux_specs = [pl.BlockSpec((N_KV_HEADS, tm, LANES), lambda i: (0, i, 0))] * 2
        aux_shapes = [jax.ShapeDtypeStruct((N_KV_HEADS, t, LANES), BF16)] * 2
    else:
        f32_spec, f32_shape = pl.BlockSpec((tm, nkv), const), jax.ShapeDtypeStruct((tm, nkv), F32)
        aux_specs = [pl.BlockSpec((tm, nkv), row), pl.BlockSpec((nkv, tm), lambda i: (0, i))]
        aux_shapes = [jax.ShapeDtypeStruct((t, nkv), BF16), jax.ShapeDtypeStruct((nkv, t), BF16)]
    nbytes = 2 * (tm * d * 4 + d * 2 * nkv * 2 + 5 * tm * nkv * 4) + tm * d * 8
    return pl.pallas_call(
        functools.partial(_kv_body, dup=dup),
        grid=(t // tm,),
        in_specs=[
            pl.BlockSpec((tm, d), row),
            pl.BlockSpec((1, d), const),
            pl.BlockSpec((d, 2 * nkv), const),
            pl.BlockSpec((1, LANES), const),
            pl.BlockSpec((tm, LANES), row),
            pl.BlockSpec((tm, LANES), row),
            pl.BlockSpec((tm, LANES), row),
        ],
        out_specs=[f32_spec, f32_spec] + aux_specs,
        out_shape=[f32_shape, f32_shape] + aux_shapes,
        compiler_params=pltpu.CompilerParams(
            dimension_semantics=("arbitrary",), vmem_limit_bytes=_vmem_limit(nbytes)),
        name="shared_kv",
    )(x, g, w_kv, kg, *rope)


def _project_q(x_ref, g_ref, wq_ref, qg_ref, cos_ref, sneg_ref, spos_ref, qe_ref, qo_ref):
    xn = _rms(x_ref[...], g_ref[...]).astype(BF16)
    q = jnp.dot(xn, wq_ref[...], preferred_element_type=F32)
    lo = _half_mask((q.shape[0], LANES))
    for p in range(q.shape[1] // LANES):
        sl = slice(p * LANES, (p + 1) * LANES)
        qr = _head_norm_rope(q[:, sl], qg_ref[...], cos_ref[...], sneg_ref[...], spos_ref[...]) * (SCALE * LOG2E)
        qe_ref[:, sl] = jnp.where(lo, qr, 0.0).astype(BF16)
        qo_ref[:, sl] = jnp.where(lo, 0.0, qr).astype(BF16)


def _attn_prompt_body(x_ref, g_ref, wqt_ref, qg_ref, cos_ref, sin_ref,
                      kprev_ref, kcur_ref, vprev_ref, vcur_ref, sink_ref, wo_ref,
                      o_ref, qt_ref, att_ref, kw_ref, vw_ref):
    i = pl.program_id(0)
    tm, d = x_ref.shape
    half = ROT_DIM // 2
    cos, sin = cos_ref[...], sin_ref[...]
    gain = jnp.concatenate([qg_ref[...]] * (tm // LANES), axis=1)
    rows = GROUP * HEAD_DIM

    nblk = tm // LANES
    dk = d // nblk

    x = x_ref[...]
    inv_rms = lax.rsqrt(jnp.mean(x * x, axis=-1, keepdims=True) + EPS)
    xn = {}

    def normed(c):
        if c not in xn:
            xn[c] = (x_ref[:, c * dk:(c + 1) * dk] * inv_rms * g_ref[:, c * dk:(c + 1) * dk]).astype(BF16)
        return xn[c]

    def project_q(kh, c):
        return lax.dot_general(wqt_ref[kh * rows:(kh + 1) * rows, c * dk:(c + 1) * dk], normed(c),
                               (((1,), (1,)), ((), ())), preferred_element_type=F32)

    def norm_rope_q(qt, kh, c):
        for j in range(c * (GROUP // nblk), (c + 1) * (GROUP // nblk)):
            t = qt[j * HEAD_DIM:(j + 1) * HEAD_DIM, :]
            tn = t * lax.rsqrt(jnp.sum(t * t, axis=0, keepdims=True) / HEAD_DIM + EPS) * gain
            x1, x2 = tn[:half], tn[half:ROT_DIM]
            h = GROUP * kh + j
            qt_ref[h * HEAD_DIM:(h + 1) * HEAD_DIM, :] = jnp.concatenate(
                [x1 * cos - x2 * sin, x2 * cos + x1 * sin, tn[ROT_DIM:]], axis=0).astype(BF16)


    kw_ref[:WINDOW, :] = kprev_ref[...]
    kw_ref[WINDOW:, :] = kcur_ref[...]
    vw_ref[:, :WINDOW] = vprev_ref[...]
    vw_ref[:, WINDOW:] = vcur_ref[...]

    first_query_chunk = lax.broadcasted_iota(jnp.int32, (CHUNK, LANES), 1) < CHUNK
    has_past = jnp.broadcast_to(i > 0, (CHUNK, LANES))
    zeros = jnp.zeros((HEAD_DIM, LANES), BF16)

    def scores_t(p, kh):
        qs = slice(p * LANES, (p + 1) * LANES)
        kwin = kw_ref[p * LANES:p * LANES + KEYS, (kh // 2) * LANES:(kh // 2 + 1) * LANES]
        rhs = jnp.concatenate(
            [jnp.concatenate([qt_ref[h * HEAD_DIM:(h + 1) * HEAD_DIM, qs], zeros] if kh % 2 == 0 else
                             [zeros, qt_ref[h * HEAD_DIM:(h + 1) * HEAD_DIM, qs]], axis=0)
             for h in range(GROUP * kh, GROUP * (kh + 1))], axis=1)
        return jnp.dot(kwin, rhs, preferred_element_type=F32)

    def softmax_t(st, p, kh):
        masks = [first_query_chunk & has_past if p == 0 else first_query_chunk, has_past if p == 0 else None,
                 None, ~first_query_chunk]
        sink_terms, pts = [], []
        for j in range(GROUP):
            s = jnp.concatenate(
                [st[c * CHUNK:(c + 1) * CHUNK, j * LANES:(j + 1) * LANES] if mask is None else
                 jnp.where(mask, st[c * CHUNK:(c + 1) * CHUNK, j * LANES:(j + 1) * LANES], -jnp.inf)
                 for c, mask in enumerate(masks)], axis=0)
            sink = sink_ref[GROUP * kh + j:GROUP * kh + j + 1, :]
            m = jnp.maximum(jnp.max(s, axis=0, keepdims=True), sink)
            sink_terms.append(jnp.exp2(sink - m))
            pts.append(jnp.exp2(s - m).astype(BF16))
        return jnp.concatenate(pts, axis=1), jnp.concatenate(sink_terms, axis=1)

    ones_rows = jnp.ones((16, KEYS), BF16)

    def weighted_values_t(p, kh, pt, sink_term):
        qs = slice(p * LANES, (p + 1) * LANES)
        v_ones = jnp.concatenate([vw_ref[kh * HEAD_DIM:(kh + 1) * HEAD_DIM, p * LANES:p * LANES + KEYS], ones_rows], axis=0)
        ot = jnp.dot(v_ones, pt, preferred_element_type=F32)
        ot = ot[:HEAD_DIM] * (1.0 / (ot[HEAD_DIM:HEAD_DIM + 1] + sink_term))
        for j in range(GROUP):
            h = GROUP * kh + j
            att_ref[h * HEAD_DIM:(h + 1) * HEAD_DIM, qs] = ot[:, j * LANES:(j + 1) * LANES].astype(BF16)

    def full_q(kh):
        qt = project_q(kh, 0)
        for c in range(1, nblk):
            qt = qt + project_q(kh, c)
        return qt

    qts = {0: full_q(0)}
    for c in range(nblk):
        piece = project_q(1, c)
        qts[1] = piece if c == 0 else qts[1] + piece
        norm_rope_q(qts[0], 0, c)
    for kh in range(N_KV_HEADS):
        st = scores_t(0, kh)
        pending = None
        for n in range(nblk):
            st_next = scores_t(n + 1, kh) if n + 1 < nblk else None
            if kh + 2 < N_KV_HEADS:
                piece = project_q(kh + 2, n)
                qts[kh + 2] = piece if n == 0 else qts[kh + 2] + piece
            pt, sink_term = softmax_t(st, n, kh)
            if kh + 1 < N_KV_HEADS:
                norm_rope_q(qts[kh + 1], kh + 1, n)
            if pending is not None:
                weighted_values_t(*pending)
            pending = (n, kh, pt, sink_term)
            st = st_next
        weighted_values_t(*pending)
    o_ref[...] = x_ref[...] + lax.dot_general(att_ref[...], wo_ref[...], (((0,), (0,)), ((), ())),
                                              preferred_element_type=F32)


def _attn_sample_body(x_ref, g_ref, wq_ref, qg_ref, cos_ref, sneg_ref, spos_ref,
                      kc_ref, knew_ref, vc_ref, vnew_ref, sink_ref, wo_ref,
                      o_ref, qe_ref, qo_ref, att_ref, *, seg):
    tm = x_ref.shape[0]
    _project_q(x_ref, g_ref, wq_ref, qg_ref, cos_ref, sneg_ref, spos_ref, qe_ref, qo_ref)
    nkeys = WINDOW + seg
    valid = lax.broadcasted_iota(jnp.int32, (1, KEYS), 1) < nkeys
    pad = jnp.zeros((KEYS - nkeys, LANES), BF16)
    pairs = GROUP // 2
    lo = _half_mask((seg, LANES))

    def window(cache_ref, new_ref, b, kh):
        return jnp.concatenate([cache_ref[b, kh], new_ref[kh, b * seg:(b + 1) * seg, :], pad], axis=0)

    def head_cols(kh):
        return [slice((pairs * kh + j) * LANES, (pairs * kh + j + 1) * LANES) for j in range(pairs)]

    def scores(b, kh):
        rows = slice(b * seg, (b + 1) * seg)
        qcat = jnp.concatenate([qe_ref[rows, c] for c in head_cols(kh)] + [qo_ref[rows, c] for c in head_cols(kh)], axis=0)
        return lax.dot_general(qcat, window(kc_ref, knew_ref, b, kh), (((1,), (1,)), ((), ())),
                               preferred_element_type=F32)

    def softmax(s, kh):
        sink = sink_ref[kh]
        s = jnp.where(valid, s, -jnp.inf)
        m = jnp.maximum(jnp.max(s, axis=-1, keepdims=True), sink)
        e = jnp.exp2(s - m)
        return e.astype(BF16), jnp.sum(e, axis=-1, keepdims=True) + jnp.exp2(sink - m)

    def weighted_values(b, kh, p, den):
        o = jnp.dot(p, window(vc_ref, vnew_ref, b, kh), preferred_element_type=F32) / den
        for j, c in enumerate(head_cols(kh)):
            att_ref[b * seg:(b + 1) * seg, c] = jnp.where(
                lo, o[j * seg:(j + 1) * seg], o[(pairs + j) * seg:(pairs + j + 1) * seg]).astype(BF16)

    blocks = [(b, kh) for b in range(tm // seg) for kh in range(N_KV_HEADS)]
    s = scores(*blocks[0])
    pending = None
    for n, blk in enumerate(blocks):
        s_next = scores(*blocks[n + 1]) if n + 1 < len(blocks) else None
        p, den = softmax(s, blk[1])
        if pending is not None:
            weighted_values(*pending)
        pending = (*blk, p, den)
        s = s_next
    weighted_values(*pending)
    o_ref[...] = x_ref[...] + jnp.dot(att_ref[...], wo_ref[...], preferred_element_type=F32)


def _attn_mixer_sample(x, g, w_q, qg, rope, k2, v2, cache, sink_col, w_o, layer, blayer, *, tm, seg):
    t, d = x.shape
    tm = min(tm, t)
    nb = tm // seg
    row = lambda i: (i, 0)
    const = lambda i: (0, 0)
    resident = dict(pipeline_mode=pl.Buffered(1))
    cur = pl.BlockSpec((N_KV_HEADS, tm, LANES), lambda i: (0, i, 0))
    cspec = pl.BlockSpec((nb, N_KV_HEADS, WINDOW, LANES), lambda i: (i, 0, 0, 0))
    in_specs = [
        pl.BlockSpec((tm, d), row),
        pl.BlockSpec((None, 1, d), lambda i: (layer, 0, 0)),
        pl.BlockSpec((None, d, d), lambda i: (blayer, 0, 0), **resident),
        pl.BlockSpec((1, LANES), const),
        pl.BlockSpec((tm, LANES), row),
        pl.BlockSpec((tm, LANES), row),
        pl.BlockSpec((tm, LANES), row),
        cspec, cur, cspec, cur,
        pl.BlockSpec((N_KV_HEADS, GROUP * seg, 1), lambda i: (0, 0, 0)),
        pl.BlockSpec((None, d, d), lambda i: (blayer, 0, 0), **resident),
    ]
    nbytes = 2 * d * d * 2 + 4 * tm * d * 4 + tm * d * (4 + 3 * 2) + 8 * tm * LANES * 4 * 2 + 4 * 2 ** 20
    return pl.pallas_call(
        functools.partial(_attn_sample_body, seg=seg),
        grid=(t // tm,),
        in_specs=in_specs,
        out_specs=pl.BlockSpec((tm, d), row),
        out_shape=jax.ShapeDtypeStruct((t, d), F32),
        scratch_shapes=[pltpu.VMEM((tm, d), BF16), pltpu.VMEM((tm, d), BF16), pltpu.VMEM((tm, d), BF16)],
        compiler_params=pltpu.CompilerParams(
            dimension_semantics=("arbitrary",), vmem_limit_bytes=_vmem_limit(nbytes)),
        name="attn_mixer_sample",
    )(x, g, w_q, qg, *rope, cache[0], k2, cache[1], v2, sink_col, w_o)


def _attn_mixer_prompt(x, g, w_qt, qg, cos_t, sin_t, kb, vt, sink_rows, w_o, layer, blayer, *, tm):
    t, d = x.shape
    tm = min(tm, t)
    nkv = N_KV_HEADS * HEAD_DIM
    nh = d // HEAD_DIM
    half = ROT_DIM // 2
    prev_blk = lambda i: jnp.maximum(i * (tm // WINDOW) - 1, 0)
    resident = dict(pipeline_mode=pl.Buffered(1))
    in_specs = [
        pl.BlockSpec((tm, d), lambda i: (i, 0)),
        pl.BlockSpec((None, 1, d), lambda i: (layer, 0, 0)),
        pl.BlockSpec((None, d, d), lambda i: (blayer, 0, 0), **resident),
        pl.BlockSpec((HEAD_DIM, LANES), lambda i: (0, 0)),
        pl.BlockSpec((half, tm), lambda i: (0, i)),
        pl.BlockSpec((half, tm), lambda i: (0, i)),
        pl.BlockSpec((WINDOW, nkv), lambda i: (prev_blk(i), 0)),
        pl.BlockSpec((tm, nkv), lambda i: (i, 0)),
        pl.BlockSpec((nkv, WINDOW), lambda i: (0, prev_blk(i))),
        pl.BlockSpec((nkv, tm), lambda i: (0, i)),
        pl.BlockSpec((nh, LANES), lambda i: (0, 0)),
        pl.BlockSpec((None, d, d), lambda i: (blayer, 0, 0), **resident),
    ]
    nbytes = (2 * d * d * 2 + 4 * tm * d * 4 + 3 * tm * d * 4 + 2 * tm * d * 2 + 2 * KEYS * GROUP * LANES * 2
              + KEYS * GROUP * LANES * 4 * 2 + 4 * (WINDOW + tm) * nkv * 2)
    return pl.pallas_call(
        _attn_prompt_body,
        grid=(t // tm,),
        in_specs=in_specs,
        out_specs=pl.BlockSpec((tm, d), lambda i: (i, 0)),
        out_shape=jax.ShapeDtypeStruct((t, d), F32),
        scratch_shapes=[pltpu.VMEM((d, tm), BF16), pltpu.VMEM((d, tm), BF16),
                        pltpu.VMEM((WINDOW + tm, nkv), BF16), pltpu.VMEM((nkv, WINDOW + tm), BF16)],
        compiler_params=pltpu.CompilerParams(
            dimension_semantics=("arbitrary",), vmem_limit_bytes=_vmem_limit(nbytes)),
        name="attn_mixer_prompt",
    )(x, g, w_qt, qg, cos_t, sin_t, kb, kb, vt, vt, sink_rows, w_o)


def _rope_tables(pos):
    half = ROT_DIM // 2
    inv = ROPE_THETA ** (-jnp.arange(half, dtype=F32) / half)
    ang = pos.astype(F32)[:, None] * inv[None, :]
    cos = jnp.tile(jnp.cos(ang), (1, LANES // half))
    sin = jnp.tile(jnp.sin(ang), (1, LANES // half))
    dim = jnp.arange(LANES) % HEAD_DIM
    c = jnp.where(dim < ROT_DIM, cos, 1.0)
    sneg = jnp.where(dim < half, -sin, 0.0)
    spos = jnp.where((dim >= half) & (dim < ROT_DIM), sin, 0.0)
    return c, sneg, spos


def _sink_column(sinks_l, rows_per_head):
    s = (sinks_l.astype(F32) * LOG2E).reshape(N_KV_HEADS, GROUP // 2, 2).transpose(0, 2, 1)
    return jnp.repeat(s.reshape(N_KV_HEADS, GROUP), rows_per_head, axis=1)[..., None]


def _dup_heads(t):
    t = t.transpose(0, 2, 1, 3)
    return jnp.concatenate([t, t], axis=-1).astype(BF16)


def _forward(x_prompt, x_sample, state_conv, cache_k, cache_v, mix_norm_g, mlp_norm_g, w_up, w_down,
             conv_w_in, conv_w, conv_w_out, kv_norm_g, w_kv, k_norm_g, w_q, q_norm_g, sinks, w_o,
             *, tm_mlp, tf, tf_cast, tm_conv, tn, tn_cast, tm_attn, tm_attn_s, tm_kv):
    _, s, d = x_prompt.shape
    b, l, _ = x_sample.shape
    n_a = conv_w_in.shape[0]
    depth = w_up.shape[0]
    xp = x_prompt.reshape(s, d)
    xs = x_sample.reshape(b * l, d)

    w_kv_b, w_q_b, w_o_b = w_kv.astype(BF16), w_q.astype(BF16), w_o.astype(BF16)
    w_qt_b = w_q_b.transpose(0, 2, 1)
    mix_g = mix_norm_g.reshape(depth, 1, d)
    mlp_g = mlp_norm_g.reshape(depth, 1, d)

    half = ROT_DIM // 2
    ang_t = (ROPE_THETA ** (-jnp.arange(half, dtype=F32) / half))[:, None] * jnp.arange(s).astype(F32)[None, :]
    cos_t, sin_t = jnp.cos(ang_t), jnp.sin(ang_t)
    rope_p = _rope_tables(jnp.arange(s))
    rope_s = _rope_tables(jnp.tile(PAST_LEN + jnp.arange(l), b))
    kg = jnp.tile(k_norm_g.astype(F32), LANES // HEAD_DIM).reshape(1, LANES)

    conv_p, conv_s = [], []
    for i in range(depth):
        if i < n_a:
            xs, cs, conv_wb = _conv_mixer(xs, mix_g, conv_w_in, conv_w, conv_w_out, state_conv, i, tm=None, tn=tn_cast, seg=l)
            xp, cp = _conv_mixer(xp, mix_g, None, conv_w, conv_wb, None, i, tm=tm_conv, tn=tn, seg=None)
            conv_p.append(cp)
            conv_s.append(cs)
        else:
            if i == n_a:
                kp, vp, kbp, vtp = _shared_kv(xp, kv_norm_g.reshape(1, d), w_kv_b, kg, rope_p, tm=tm_kv, dup=False)
                ks, vs, k2s, v2s = _shared_kv(xs, kv_norm_g.reshape(1, d), w_kv_b, kg, rope_s, tm=tm_kv, dup=True)
                cache2 = (_dup_heads(cache_k), _dup_heads(cache_v))
            j = i - n_a
            qg = jnp.tile(q_norm_g[j].astype(F32), LANES // HEAD_DIM).reshape(1, LANES)
            qg_t = jnp.broadcast_to((q_norm_g[j].astype(F32) * (SCALE * LOG2E))[:, None], (HEAD_DIM, LANES))
            sink_rows = jnp.broadcast_to((sinks[j].astype(F32) * LOG2E)[:, None], (sinks.shape[1], LANES))
            xp = _attn_mixer_prompt(xp, mix_g, w_qt_b, qg_t, cos_t, sin_t, kbp, vtp, sink_rows, w_o_b, i, j, tm=tm_attn)
            xs = _attn_mixer_sample(xs, mix_g, w_q_b, qg, rope_s, k2s, v2s, cache2, _sink_column(sinks[j], l),
                                    w_o_b, i, j, tm=tm_attn_s, seg=l)
        xs, w_up_b, w_down_b = _mlp_cast(xs, mlp_g, w_up, w_down, i, tf=tf_cast)
        xp = _mlp(xp, mlp_g, w_up_b, w_down_b, i, tm=tm_mlp, tf=tf)

    hd = (N_KV_HEADS, HEAD_DIM)
    ks_new = ks.reshape(b, l, *hd)
    vs_new = vs.reshape(b, l, *hd)
    return (xp.reshape(1, s, d), xs.reshape(b, l, d), jnp.stack(conv_p), jnp.stack(conv_s),
            kp[-WINDOW:].reshape(1, WINDOW, *hd), vp[-WINDOW:].reshape(1, WINDOW, *hd),
            jnp.concatenate([cache_k[:, l:], ks_new], axis=1), jnp.concatenate([cache_v[:, l:], vs_new], axis=1))


def kernel(x_prompt, x_sample, state_conv, cache_k, cache_v, mix_norm_g, mlp_norm_g, w_up, w_down, conv_w_in, conv_w, conv_w_out, kv_norm_g, w_kv, k_norm_g, w_q, q_norm_g, sinks, w_o):
    return _forward(x_prompt, x_sample, state_conv, cache_k, cache_v, mix_norm_g, mlp_norm_g, w_up, w_down,
                    conv_w_in, conv_w, conv_w_out, kv_norm_g, w_kv, k_norm_g, w_q, q_norm_g, sinks, w_o,
                    tm_mlp=512, tf=2048, tf_cast=512, tm_conv=512, tn=1024, tn_cast=256, tm_attn=512, tm_attn_s=256, tm_kv=2048)
```

```python
import functools

import jax
import jax.numpy as jnp
from jax import lax
from jax.experimental import pallas as pl
from jax.experimental.pallas import tpu as pltpu

EPS = 1e-6
CHUNK = 64
WINDOW = 128
HEAD_DIM = 64
N_KV_HEADS = 4
GROUP = 8
ROT_DIM = 16
ROPE_THETA = 500000.0
PAST_LEN = 2048
SCALE = HEAD_DIM ** -0.5
LOG2E = 1.4426950408889634

LANES = 128
KEYS = 2 * WINDOW
KV_ROW_GROUP = 256
CONV_SUBCHUNK = 256
VMEM_LIMIT_CAP = 56 * 2 ** 20

F32 = jnp.float32
BF16 = jnp.bfloat16


def _vmem_limit(nbytes):
    return int(min(VMEM_LIMIT_CAP, max(32 * 2 ** 20, nbytes * 5 // 4 + 4 * 2 ** 20)))


def _rms(x, g):
    return x * lax.rsqrt(jnp.mean(x * x, axis=-1, keepdims=True) + EPS) * g


def _half_mask(shape):
    return lax.broadcasted_iota(jnp.int32, shape, len(shape) - 1) < HEAD_DIM


def _head_norm_rope(t, gain, cos, sneg, spos):
    lo = _half_mask(t.shape)
    sq = t * t
    s_lo = jnp.sum(jnp.where(lo, sq, 0.0), axis=-1, keepdims=True)
    s_hi = jnp.sum(jnp.where(lo, 0.0, sq), axis=-1, keepdims=True)
    inv = jnp.where(lo, lax.rsqrt(s_lo / HEAD_DIM + EPS), lax.rsqrt(s_hi / HEAD_DIM + EPS))
    tn = t * inv * gain
    half = ROT_DIM // 2
    return tn * cos + pltpu.roll(tn, LANES - half, 1) * sneg + pltpu.roll(tn, half, 1) * spos


def _mlp_body(x_ref, g_ref, wu_ref, wd_ref, o_ref, xn_ref):
    def contribution():
        h = jnp.dot(xn_ref[...], wu_ref[...], preferred_element_type=F32)
        h = jnp.square(jnp.maximum(h, 0.0)).astype(BF16)
        return jnp.dot(h, wd_ref[...], preferred_element_type=F32)

    @pl.when(pl.program_id(1) == 0)
    def _():
        xn_ref[...] = _rms(x_ref[...], g_ref[...]).astype(BF16)
        o_ref[...] = x_ref[...] + contribution()

    @pl.when(pl.program_id(1) > 0)
    def _():
        o_ref[...] += contribution()


def _mlp_cast_body(x_ref, g_ref, wu_ref, wd_ref, o_ref, wub_ref, wdb_ref, xn_ref):
    def contribution():
        wu = wu_ref[...].astype(BF16)
        wd = wd_ref[...].astype(BF16)
        wub_ref[...] = wu
        wdb_ref[...] = wd
        h = jnp.dot(xn_ref[...], wu, preferred_element_type=F32)
        h = jnp.square(jnp.maximum(h, 0.0)).astype(BF16)
        return jnp.dot(h, wd, preferred_element_type=F32)

    @pl.when(pl.program_id(0) == 0)
    def _():
        xn_ref[...] = _rms(x_ref[...], g_ref[...]).astype(BF16)
        o_ref[...] = x_ref[...] + contribution()

    @pl.when(pl.program_id(0) > 0)
    def _():
        o_ref[...] += contribution()


def _mlp_cast(x, g, w_up, w_down, layer, *, tf):
    t, d = x.shape
    f = w_up.shape[2]
    tf = min(tf, f)
    nbytes = 2 * t * d * 4 + t * d * 2 + 2 * 2 * d * tf * (4 + 2) + t * tf * 6
    return pl.pallas_call(
        _mlp_cast_body,
        grid=(f // tf,),
        in_specs=[
            pl.BlockSpec((t, d), lambda j: (0, 0), pipeline_mode=pl.Buffered(1)),
            pl.BlockSpec((None, 1, d), lambda j: (layer, 0, 0)),
            pl.BlockSpec((None, d, tf), lambda j: (layer, 0, j)),
            pl.BlockSpec((None, tf, d), lambda j: (layer, j, 0)),
        ],
        out_specs=[
            pl.BlockSpec((t, d), lambda j: (0, 0), pipeline_mode=pl.Buffered(1)),
            pl.BlockSpec((d, tf), lambda j: (0, j)),
            pl.BlockSpec((tf, d), lambda j: (j, 0)),
        ],
        out_shape=[
            jax.ShapeDtypeStruct((t, d), F32),
            jax.ShapeDtypeStruct((d, f), BF16),
            jax.ShapeDtypeStruct((f, d), BF16),
        ],
        scratch_shapes=[pltpu.VMEM((t, d), BF16)],
        compiler_params=pltpu.CompilerParams(
            dimension_semantics=("arbitrary",), vmem_limit_bytes=_vmem_limit(nbytes)),
        name="mlp_cast",
    )(x, g, w_up, w_down)


def _mlp(x, g, w_up, w_down, layer, *, tm, tf):
    t, d = x.shape
    f = w_up.shape[1]
    tm, tf = min(tm, t), min(tf, f)
    nbytes = 2 * (2 * tm * d * 4 + 2 * d * tf * 2) + tm * d * 2 + tm * tf * 6
    return pl.pallas_call(
        _mlp_body,
        grid=(t // tm, f // tf),
        in_specs=[
            pl.BlockSpec((tm, d), lambda i, j: (i, 0)),
            pl.BlockSpec((None, 1, d), lambda i, j: (layer, 0, 0)),
            pl.BlockSpec((d, tf), lambda i, j: (0, j)),
            pl.BlockSpec((tf, d), lambda i, j: (j, 0)),
        ],
        out_specs=pl.BlockSpec((tm, d), lambda i, j: (i, 0)),
        out_shape=jax.ShapeDtypeStruct((t, d), F32),
        scratch_shapes=[pltpu.VMEM((tm, d), BF16)],
        compiler_params=pltpu.CompilerParams(
            dimension_semantics=("arbitrary", "arbitrary"), vmem_limit_bytes=_vmem_limit(nbytes)),
        name="mlp",
    )(x, g, w_up, w_down)


def _conv_body(*refs, seg, tn, cast):
    if seg is None:
        x_ref, g_ref, wb_ref, wc_ref, wu_ref, cw_ref, wout_ref, o_ref, zl_ref, *rest = refs
    else:
        x_ref, g_ref, wb_ref, wc_ref, wu_ref, cw_ref, wout_ref, st_ref, o_ref, zl_ref, *rest = refs
    xn_ref = rest[-1]
    i, j = pl.program_id(0), pl.program_id(1)

    if seg is None:
        @pl.when(i == 0)
        def _():
            zl_ref[j] = jnp.zeros((2, tn), F32)

    tm = x_ref.shape[0]
    tc = min(tn, CONV_SUBCHUNK)

    def weights():
        if not cast:
            return wb_ref, wc_ref, wu_ref, wout_ref
        ws = [r[...].astype(BF16) for r in (wb_ref, wc_ref, wu_ref, wout_ref)]
        for dst, w in zip(rest[:4], ws):
            dst[...] = w
        return ws

    def project(c, wb, wc, wu):
        cs = slice(c * tc, (c + 1) * tc)
        xn = xn_ref[...]
        z = jnp.dot(xn, wc[:, cs], preferred_element_type=F32) * jnp.dot(xn, wu[:, cs], preferred_element_type=F32)
        return jnp.dot(xn, wb[:, cs], preferred_element_type=F32), z

    def gated_conv(gate_b, z, c):
        cs = slice(c * tc, (c + 1) * tc)
        row = lax.broadcasted_iota(jnp.int32, z.shape, 0)
        r1 = pltpu.roll(z, 1, 0)
        r2 = pltpu.roll(z, 2, 0)
        if seg is None:
            prev = zl_ref[j, :, cs]
            p0, p1 = prev[0:1, :], prev[1:2, :]
            zl_ref[j, :, cs] = z[tm - 2:, :]
        else:
            nb = tm // seg
            st = st_ref[:, :, cs]
            p0 = jnp.broadcast_to(st[:, 0:1, :], (nb, seg, tc)).reshape(tm, tc)
            p1 = jnp.broadcast_to(st[:, 1:2, :], (nb, seg, tc)).reshape(tm, tc)
            row = row % seg
            zl_ref[:, :, cs] = z.reshape(nb, seg, tc)[:, seg - 2:, :]
        zp1 = jnp.where(row == 0, p1, r1)
        zp2 = jnp.where(row == 0, p0, jnp.where(row == 1, p1, r2))
        cw = cw_ref[:, cs]
        conv = zp2 * cw[0:1, :] + zp1 * cw[1:2, :] + z * cw[2:3, :]
        return (gate_b * conv).astype(BF16)

    def contribution():
        nc = tn // tc
        wb, wc, wu, wout = weights()
        ys = []
        pending = project(0, wb, wc, wu)
        for c in range(nc):
            nxt = project(c + 1, wb, wc, wu) if c + 1 < nc else None
            ys.append(gated_conv(*pending, c))
            pending = nxt
        return jnp.dot(jnp.concatenate(ys, axis=1), wout[...], preferred_element_type=F32)

    @pl.when(j == 0)
    def _():
        xn_ref[...] = _rms(x_ref[...], g_ref[...]).astype(BF16)
        o_ref[...] = x_ref[...] + contribution()

    @pl.when(j > 0)
    def _():
        o_ref[...] += contribution()


def _conv_mixer(x, g, w_in, cw, w_out, state, layer, *, tm, tn, seg):
    t, d = x.shape
    cast = w_in is not None
    tm = t if cast else min(tm, t)
    nj = d // tn
    once = dict(pipeline_mode=pl.Buffered(1)) if cast else {}
    in_specs = [
        pl.BlockSpec((tm, d), lambda i, j: (i, 0), **once),
        pl.BlockSpec((None, 1, d), lambda i, j: (layer, 0, 0)),
    ]
    if cast:
        in_specs += [
            pl.BlockSpec((None, d, tn), lambda i, j: (layer, 0, j)),
            pl.BlockSpec((None, d, tn), lambda i, j: (layer, 0, nj + j)),
            pl.BlockSpec((None, d, tn), lambda i, j: (layer, 0, 2 * nj + j)),
            pl.BlockSpec((None, 3, tn), lambda i, j: (layer, 0, j)),
            pl.BlockSpec((None, tn, d), lambda i, j: (layer, j, 0)),
        ]
        args = [x, g, w_in, w_in, w_in, cw, w_out]
    else:
        in_specs += [
            pl.BlockSpec((d, tn), lambda i, j: (0, j)),
            pl.BlockSpec((d, tn), lambda i, j: (0, j)),
            pl.BlockSpec((d, tn), lambda i, j: (0, j)),
            pl.BlockSpec((None, 3, tn), lambda i, j: (layer, 0, j)),
            pl.BlockSpec((tn, d), lambda i, j: (j, 0)),
        ]
        args = [x, g, *w_out[:3], cw, w_out[3]]
    if seg is None:
        zl_shape = (nj, 2, tn)
        zl_spec = pl.BlockSpec((nj, 2, tn), lambda i, j: (0, 0, 0))
    else:
        nb = tm // seg
        zl_shape = (t // seg, 2, d)
        zl_spec = pl.BlockSpec((nb, 2, tn), lambda i, j: (i, 0, j))
        in_specs.append(pl.BlockSpec((None, nb, 2, tn), lambda i, j: (layer, i, 0, j)))
        args.append(state)
    out_specs = [pl.BlockSpec((tm, d), lambda i, j: (i, 0), **once), zl_spec]
    out_shape = [jax.ShapeDtypeStruct((t, d), F32), jax.ShapeDtypeStruct(zl_shape, F32)]
    if cast:
        out_specs += [pl.BlockSpec((d, tn), lambda i, j: (0, j))] * 3 + [pl.BlockSpec((tn, d), lambda i, j: (j, 0))]
        out_shape += [jax.ShapeDtypeStruct((d, d), BF16)] * 4
        nbytes = 2 * tm * d * 4 + 2 * 4 * d * tn * (4 + 2) + tm * d * 2 + tm * tn * 4 * 8
    else:
        nbytes = 2 * (2 * tm * d * 4 + d * 3 * tn * 2 + tn * d * 2) + tm * d * 2 + tm * tn * 4 * 8
    y, zl, *w_b = pl.pallas_call(
        functools.partial(_conv_body, seg=seg, tn=tn, cast=cast),
        grid=(t // tm, nj),
        in_specs=in_specs,
        out_specs=out_specs,
        out_shape=out_shape,
        scratch_shapes=[pltpu.VMEM((tm, d), BF16)],
        compiler_params=pltpu.CompilerParams(
            dimension_semantics=("arbitrary", "arbitrary"), vmem_limit_bytes=_vmem_limit(nbytes)),
        name="conv_mixer_cast" if cast else "conv_mixer",
    )(*args)
    if seg is None:
        zl = zl.transpose(1, 0, 2).reshape(1, 2, d)
    return (y, zl, tuple(w_b)) if cast else (y, zl)


def _kv_body(x_ref, g_ref, wkv_ref, kg_ref, cos_ref, sneg_ref, spos_ref, k_ref, v_ref, ka_ref, va_ref, *, dup):
    tm = x_ref.shape[0]
    nkv = N_KV_HEADS * HEAD_DIM
    gr = min(tm, KV_ROW_GROUP)
    lo = _half_mask((gr, LANES))

    def normed(r):
        return _rms(x_ref[r * gr:(r + 1) * gr, :], g_ref[...]).astype(BF16)

    xn = normed(0)
    for r in range(tm // gr):
        rs = slice(r * gr, (r + 1) * gr)
        xn_next = normed(r + 1) if (r + 1) * gr < tm else None
        kv = jnp.dot(xn, wkv_ref[...], preferred_element_type=F32)
        for p in range(nkv // LANES):
            sl = slice(p * LANES, (p + 1) * LANES)
            kr = _head_norm_rope(kv[:, sl], kg_ref[...], cos_ref[rs, :], sneg_ref[rs, :], spos_ref[rs, :])
            vr = kv[:, nkv + p * LANES: nkv + (p + 1) * LANES]
            k_ref[rs, sl] = kr
            v_ref[rs, sl] = vr
            if dup:
                for src, dst in ((kr, ka_ref), (vr, va_ref)):
                    sw = pltpu.roll(src, HEAD_DIM, 1)
                    dst[2 * p, rs, :] = jnp.where(lo, src, sw).astype(BF16)
                    dst[2 * p + 1, rs, :] = jnp.where(lo, sw, src).astype(BF16)
            else:
                ka_ref[rs, sl] = kr.astype(BF16)
                va_ref[sl, rs] = vr.T.astype(BF16)
        xn = xn_next


def _shared_kv(x, g, w_kv, kg, rope, *, tm, dup):
    t, d = x.shape
    tm = min(tm, t)
    nkv = N_KV_HEADS * HEAD_DIM
    row = lambda i: (i, 0)
    const = lambda i: (0, 0)
    if dup:
        f32_spec, f32_shape = pl.BlockSpec((tm, nkv), row), jax.ShapeDtypeStruct((t, nkv), F32)
        aux_specs = [pl.BlockSpec((N_KV_HEADS, tm, LANES), lambda i: (0, i, 0))] * 2
        aux_shapes = [jax.ShapeDtypeStruct((N_KV_HEADS, t, LANES), BF16)] * 2
    else:
        f32_spec, f32_shape = pl.BlockSpec((tm, nkv), const), jax.ShapeDtypeStruct((tm, nkv), F32)
        aux_specs = [pl.BlockSpec((tm, nkv), row), pl.BlockSpec((nkv, tm), lambda i: (0, i))]
        aux_shapes = [jax.ShapeDtypeStruct((t, nkv), BF16), jax.ShapeDtypeStruct((nkv, t), BF16)]
    nbytes = 2 * (tm * d * 4 + d * 2 * nkv * 2 + 5 * tm * nkv * 4) + tm * d * 8
    return pl.pallas_call(
        functools.partial(_kv_body, dup=dup),
        grid=(t // tm,),
        in_specs=[
            pl.BlockSpec((tm, d), row),
            pl.BlockSpec((1, d), const),
            pl.BlockSpec((d, 2 * nkv), const),
            pl.BlockSpec((1, LANES), const),
            pl.BlockSpec((tm, LANES), row),
            pl.BlockSpec((tm, LANES), row),
            pl.BlockSpec((tm, LANES), row),
        ],
        out_specs=[f32_spec, f32_spec] + aux_specs,
        out_shape=[f32_shape, f32_shape] + aux_shapes,
        compiler_params=pltpu.CompilerParams(
            dimension_semantics=("arbitrary",), vmem_limit_bytes=_vmem_limit(nbytes)),
        name="shared_kv",
    )(x, g, w_kv, kg, *rope)


def _project_q(x_ref, g_ref, wq_ref, qg_ref, cos_ref, sneg_ref, spos_ref, qe_ref, qo_ref):
    xn = _rms(x_ref[...], g_ref[...]).astype(BF16)
    q = jnp.dot(xn, wq_ref[...], preferred_element_type=F32)
    lo = _half_mask((q.shape[0], LANES))
    for p in range(q.shape[1] // LANES):
        sl = slice(p * LANES, (p + 1) * LANES)
        qr = _head_norm_rope(q[:, sl], qg_ref[...], cos_ref[...], sneg_ref[...], spos_ref[...]) * (SCALE * LOG2E)
        qe_ref[:, sl] = jnp.where(lo, qr, 0.0).astype(BF16)
        qo_ref[:, sl] = jnp.where(lo, 0.0, qr).astype(BF16)


def _attn_prompt_body(x_ref, g_ref, wqt_ref, qg_ref, cos_ref, sin_ref,
                      kprev_ref, kcur_ref, vprev_ref, vcur_ref, sink_ref, wo_ref,
                      o_ref, qt_ref, att_ref, kw_ref, vw_ref):
    i = pl.program_id(0)
    tm, d = x_ref.shape
    half = ROT_DIM // 2
    cos, sin = cos_ref[...], sin_ref[...]
    gain = jnp.concatenate([qg_ref[...]] * (tm // LANES), axis=1)
    rows = GROUP * HEAD_DIM

    nblk = tm // LANES
    dk = d // nblk

    x = x_ref[...]
    inv_rms = lax.rsqrt(jnp.mean(x * x, axis=-1, keepdims=True) + EPS)
    xn = {}

    def normed(c):
        if c not in xn:
            xn[c] = (x_ref[:, c * dk:(c + 1) * dk] * inv_rms * g_ref[:, c * dk:(c + 1) * dk]).astype(BF16)
        return xn[c]

    def project_q(kh, c):
        return lax.dot_general(wqt_ref[kh * rows:(kh + 1) * rows, c * dk:(c + 1) * dk], normed(c),
                               (((1,), (1,)), ((), ())), preferred_element_type=F32)

    def norm_rope_q(qt, kh, c):
        for j in range(c * (GROUP // nblk), (c + 1) * (GROUP // nblk)):
            t = qt[j * HEAD_DIM:(j + 1) * HEAD_DIM, :]
            tn = t * lax.rsqrt(jnp.sum(t * t, axis=0, keepdims=True) / HEAD_DIM + EPS) * gain
            x1, x2 = tn[:half], tn[half:ROT_DIM]
            h = GROUP * kh + j
            qt_ref[h * HEAD_DIM:(h + 1) * HEAD_DIM, :] = jnp.concatenate(
                [x1 * cos - x2 * sin, x2 * cos + x1 * sin, tn[ROT_DIM:]], axis=0).astype(BF16)


    kw_ref[:WINDOW, :] = kprev_ref[...]
    kw_ref[WINDOW:, :] = kcur_ref[...]
    vw_ref[:, :WINDOW] = vprev_ref[...]
    vw_ref[:, WINDOW:] = vcur_ref[...]

    first_query_chunk = lax.broadcasted_iota(jnp.int32, (CHUNK, LANES), 1) < CHUNK
    has_past = jnp.broadcast_to(i > 0, (CHUNK, LANES))
    zeros = jnp.zeros((HEAD_DIM, LANES), BF16)

    def scores_t(p, kh):
        qs = slice(p * LANES, (p + 1) * LANES)
        kwin = kw_ref[p * LANES:p * LANES + KEYS, (kh // 2) * LANES:(kh // 2 + 1) * LANES]
        rhs = jnp.concatenate(
            [jnp.concatenate([qt_ref[h * HEAD_DIM:(h + 1) * HEAD_DIM, qs], zeros] if kh % 2 == 0 else
                             [zeros, qt_ref[h * HEAD_DIM:(h + 1) * HEAD_DIM, qs]], axis=0)
             for h in range(GROUP * kh, GROUP * (kh + 1))], axis=1)
        return jnp.dot(kwin, rhs, preferred_element_type=F32)

    def softmax_t(st, p, kh):
        masks = [first_query_chunk & has_past if p == 0 else first_query_chunk, has_past if p == 0 else None,
                 None, ~first_query_chunk]
        sink_terms, pts = [], []
        for j in range(GROUP):
            s = jnp.concatenate(
                [st[c * CHUNK:(c + 1) * CHUNK, j * LANES:(j + 1) * LANES] if mask is None else
                 jnp.where(mask, st[c * CHUNK:(c + 1) * CHUNK, j * LANES:(j + 1) * LANES], -jnp.inf)
                 for c, mask in enumerate(masks)], axis=0)
            sink = sink_ref[GROUP * kh + j:GROUP * kh + j + 1, :]
            m = jnp.maximum(jnp.max(s, axis=0, keepdims=True), sink)
            sink_terms.append(jnp.exp2(sink - m))
            pts.append(jnp.exp2(s - m).astype(BF16))
        return jnp.concatenate(pts, axis=1), jnp.concatenate(sink_terms, axis=1)

    ones_rows = jnp.ones((16, KEYS), BF16)

    def weighted_values_t(p, kh, pt, sink_term):
        qs = slice(p * LANES, (p + 1) * LANES)
        v_ones = jnp.concatenate([vw_ref[kh * HEAD_DIM:(kh + 1) * HEAD_DIM, p * LANES:p * LANES + KEYS], ones_rows], axis=0)
        ot = jnp.dot(v_ones, pt, preferred_element_type=F32)
        ot = ot[:HEAD_DIM] * (1.0 / (ot[HEAD_DIM:HEAD_DIM + 1] + sink_term))
        for j in range(GROUP):
            h = GROUP * kh + j
            att_ref[h * HEAD_DIM:(h + 1) * HEAD_DIM, qs] = ot[:, j * LANES:(j + 1) * LANES].astype(BF16)

    early = N_KV_HEADS // 2 * rows

    def project_o(r0, r1, c):
        return lax.dot_general(att_ref[r0:r1, :], wo_ref[r0:r1, c * dk:(c + 1) * dk], (((0,), (0,)), ((), ())),
                               preferred_element_type=F32)

    def full_q(kh):
        qt = project_q(kh, 0)
        for c in range(1, nblk):
            qt = qt + project_q(kh, c)
        return qt

    qts = {0: full_q(0)}
    for c in range(nblk):
        piece = project_q(1, c)
        qts[1] = piece if c == 0 else qts[1] + piece
        norm_rope_q(qts[0], 0, c)
    acc = [None] * nblk
    for kh in range(N_KV_HEADS):
        st = scores_t(0, kh)
        pending = None
        for n in range(nblk):
            st_next = scores_t(n + 1, kh) if n + 1 < nblk else None
            if kh + 2 < N_KV_HEADS:
                piece = project_q(kh + 2, n)
                qts[kh + 2] = piece if n == 0 else qts[kh + 2] + piece
            elif kh == N_KV_HEADS // 2:
                acc[n] = x_ref[:, n * dk:(n + 1) * dk] + project_o(0, early, n)
            pt, sink_term = softmax_t(st, n, kh)
            if kh + 1 < N_KV_HEADS:
                norm_rope_q(qts[kh + 1], kh + 1, n)
            if pending is not None:
                weighted_values_t(*pending)
            pending = (n, kh, pt, sink_term)
            st = st_next
        weighted_values_t(*pending)
    for c in range(nblk):
        o_ref[:, c * dk:(c + 1) * dk] = acc[c] + project_o(early, d, c)


def _attn_sample_body(x_ref, g_ref, wq_ref, qg_ref, cos_ref, sneg_ref, spos_ref,
                      kc_ref, knew_ref, vc_ref, vnew_ref, sink_ref, wo_ref,
                      o_ref, qe_ref, qo_ref, att_ref, *, seg):
    tm = x_ref.shape[0]
    _project_q(x_ref, g_ref, wq_ref, qg_ref, cos_ref, sneg_ref, spos_ref, qe_ref, qo_ref)
    nkeys = WINDOW + seg
    valid = lax.broadcasted_iota(jnp.int32, (1, KEYS), 1) < nkeys
    pad = jnp.zeros((KEYS - nkeys, LANES), BF16)
    pairs = GROUP // 2
    lo = _half_mask((seg, LANES))

    def window(cache_ref, new_ref, b, kh):
        return jnp.concatenate([cache_ref[b, kh], new_ref[kh, b * seg:(b + 1) * seg, :], pad], axis=0)

    def head_cols(kh):
        return [slice((pairs * kh + j) * LANES, (pairs * kh + j + 1) * LANES) for j in range(pairs)]

    def scores(b, kh):
        rows = slice(b * seg, (b + 1) * seg)
        qcat = jnp.concatenate([qe_ref[rows, c] for c in head_cols(kh)] + [qo_ref[rows, c] for c in head_cols(kh)], axis=0)
        return lax.dot_general(qcat, window(kc_ref, knew_ref, b, kh), (((1,), (1,)), ((), ())),
                               preferred_element_type=F32)

    def softmax(s, kh):
        sink = sink_ref[kh]
        s = jnp.where(valid, s, -jnp.inf)
        m = jnp.maximum(jnp.max(s, axis=-1, keepdims=True), sink)
        e = jnp.exp2(s - m)
        return e.astype(BF16), jnp.sum(e, axis=-1, keepdims=True) + jnp.exp2(sink - m)

    def weighted_values(b, kh, p, den):
        o = jnp.dot(p, window(vc_ref, vnew_ref, b, kh), preferred_element_type=F32) / den
        for j, c in enumerate(head_cols(kh)):
            att_ref[b * seg:(b + 1) * seg, c] = jnp.where(
                lo, o[j * seg:(j + 1) * seg], o[(pairs + j) * seg:(pairs + j + 1) * seg]).astype(BF16)

    blocks = [(b, kh) for b in range(tm // seg) for kh in range(N_KV_HEADS)]
    s = scores(*blocks[0])
    pending = None
    for n, blk in enumerate(blocks):
        s_next = scores(*blocks[n + 1]) if n + 1 < len(blocks) else None
        p, den = softmax(s, blk[1])
        if pending is not None:
            weighted_values(*pending)
        pending = (*blk, p, den)
        s = s_next
    weighted_values(*pending)
    o_ref[...] = x_ref[...] + jnp.dot(att_ref[...], wo_ref[...], preferred_element_type=F32)


def _attn_mixer_sample(x, g, w_q, qg, rope, k2, v2, cache, sink_col, w_o, layer, blayer, *, tm, seg):
    t, d = x.shape
    tm = min(tm, t)
    nb = tm // seg
    row = lambda i: (i, 0)
    const = lambda i: (0, 0)
    resident = dict(pipeline_mode=pl.Buffered(1))
    cur = pl.BlockSpec((N_KV_HEADS, tm, LANES), lambda i: (0, i, 0))
    cspec = pl.BlockSpec((nb, N_KV_HEADS, WINDOW, LANES), lambda i: (i, 0, 0, 0))
    in_specs = [
        pl.BlockSpec((tm, d), row),
        pl.BlockSpec((None, 1, d), lambda i: (layer, 0, 0)),
        pl.BlockSpec((None, d, d), lambda i: (blayer, 0, 0), **resident),
        pl.BlockSpec((1, LANES), const),
        pl.BlockSpec((tm, LANES), row),
        pl.BlockSpec((tm, LANES), row),
        pl.BlockSpec((tm, LANES), row),
        cspec, cur, cspec, cur,
        pl.BlockSpec((N_KV_HEADS, GROUP * seg, 1), lambda i: (0, 0, 0)),
        pl.BlockSpec((None, d, d), lambda i: (blayer, 0, 0), **resident),
    ]
    nbytes = 2 * d * d * 2 + 4 * tm * d * 4 + tm * d * (4 + 3 * 2) + 8 * tm * LANES * 4 * 2 + 4 * 2 ** 20
    return pl.pallas_call(
        functools.partial(_attn_sample_body, seg=seg),
        grid=(t // tm,),
        in_specs=in_specs,
        out_specs=pl.BlockSpec((tm, d), row),
        out_shape=jax.ShapeDtypeStruct((t, d), F32),
        scratch_shapes=[pltpu.VMEM((tm, d), BF16), pltpu.VMEM((tm, d), BF16), pltpu.VMEM((tm, d), BF16)],
        compiler_params=pltpu.CompilerParams(
            dimension_semantics=("arbitrary",), vmem_limit_bytes=_vmem_limit(nbytes)),
        name="attn_mixer_sample",
    )(x, g, w_q, qg, *rope, cache[0], k2, cache[1], v2, sink_col, w_o)


def _attn_mixer_prompt(x, g, w_qt, qg, cos_t, sin_t, kb, vt, sink_rows, w_o, layer, blayer, *, tm):
    t, d = x.shape
    tm = min(tm, t)
    nkv = N_KV_HEADS * HEAD_DIM
    nh = d // HEAD_DIM
    half = ROT_DIM // 2
    prev_blk = lambda i: jnp.maximum(i * (tm // WINDOW) - 1, 0)
    resident = dict(pipeline_mode=pl.Buffered(1))
    in_specs = [
        pl.BlockSpec((tm, d), lambda i: (i, 0)),
        pl.BlockSpec((None, 1, d), lambda i: (layer, 0, 0)),
        pl.BlockSpec((None, d, d), lambda i: (blayer, 0, 0), **resident),
        pl.BlockSpec((HEAD_DIM, LANES), lambda i: (0, 0)),
        pl.BlockSpec((half, tm), lambda i: (0, i)),
        pl.BlockSpec((half, tm), lambda i: (0, i)),
        pl.BlockSpec((WINDOW, nkv), lambda i: (prev_blk(i), 0)),
        pl.BlockSpec((tm, nkv), lambda i: (i, 0)),
        pl.BlockSpec((nkv, WINDOW), lambda i: (0, prev_blk(i))),
        pl.BlockSpec((nkv, tm), lambda i: (0, i)),
        pl.BlockSpec((nh, LANES), lambda i: (0, 0)),
        pl.BlockSpec((None, d, d), lambda i: (blayer, 0, 0), **resident),
    ]
    nbytes = (2 * d * d * 2 + 4 * tm * d * 4 + 3 * tm * d * 4 + 2 * tm * d * 2 + 2 * KEYS * GROUP * LANES * 2
              + KEYS * GROUP * LANES * 4 * 2 + 4 * (WINDOW + tm) * nkv * 2)
    return pl.pallas_call(
        _attn_prompt_body,
        grid=(t // tm,),
        in_specs=in_specs,
        out_specs=pl.BlockSpec((tm, d), lambda i: (i, 0)),
        out_shape=jax.ShapeDtypeStruct((t, d), F32),
        scratch_shapes=[pltpu.VMEM((d, tm), BF16), pltpu.VMEM((d, tm), BF16),
                        pltpu.VMEM((WINDOW + tm, nkv), BF16), pltpu.VMEM((nkv, WINDOW + tm), BF16)],
        compiler_params=pltpu.CompilerParams(
            dimension_semantics=("arbitrary",), vmem_limit_bytes=_vmem_limit(nbytes)),
        name="attn_mixer_prompt",
    )(x, g, w_qt, qg, cos_t, sin_t, kb, kb, vt, vt, sink_rows, w_o)


def _rope_tables(pos):
    half = ROT_DIM // 2
    inv = ROPE_THETA ** (-jnp.arange(half, dtype=F32) / half)
    ang = pos.astype(F32)[:, None] * inv[None, :]
    cos = jnp.tile(jnp.cos(ang), (1, LANES // half))
    sin = jnp.tile(jnp.sin(ang), (1, LANES // half))
    dim = jnp.arange(LANES) % HEAD_DIM
    c = jnp.where(dim < ROT_DIM, cos, 1.0)
    sneg = jnp.where(dim < half, -sin, 0.0)
    spos = jnp.where((dim >= half) & (dim < ROT_DIM), sin, 0.0)
    return c, sneg, spos


def _sink_column(sinks_l, rows_per_head):
    s = (sinks_l.astype(F32) * LOG2E).reshape(N_KV_HEADS, GROUP // 2, 2).transpose(0, 2, 1)
    return jnp.repeat(s.reshape(N_KV_HEADS, GROUP), rows_per_head, axis=1)[..., None]


def _dup_heads(t):
    t = t.transpose(0, 2, 1, 3)
    return jnp.concatenate([t, t], axis=-1).astype(BF16)


def _forward(x_prompt, x_sample, state_conv, cache_k, cache_v, mix_norm_g, mlp_norm_g, w_up, w_down,
             conv_w_in, conv_w, conv_w_out, kv_norm_g, w_kv, k_norm_g, w_q, q_norm_g, sinks, w_o,
             *, tm_mlp, tf, tf_cast, tm_conv, tn, tn_cast, tm_attn, tm_attn_s, tm_kv):
    _, s, d = x_prompt.shape
    b, l, _ = x_sample.shape
    n_a = conv_w_in.shape[0]
    depth = w_up.shape[0]
    xp = x_prompt.reshape(s, d)
    xs = x_sample.reshape(b * l, d)

    w_kv_b, w_q_b, w_o_b = w_kv.astype(BF16), w_q.astype(BF16), w_o.astype(BF16)
    w_qt_b = w_q_b.transpose(0, 2, 1)
    mix_g = mix_norm_g.reshape(depth, 1, d)
    mlp_g = mlp_norm_g.reshape(depth, 1, d)

    half = ROT_DIM // 2
    ang_t = (ROPE_THETA ** (-jnp.arange(half, dtype=F32) / half))[:, None] * jnp.arange(s).astype(F32)[None, :]
    cos_t, sin_t = jnp.cos(ang_t), jnp.sin(ang_t)
    rope_p = _rope_tables(jnp.arange(s))
    rope_s = _rope_tables(jnp.tile(PAST_LEN + jnp.arange(l), b))
    kg = jnp.tile(k_norm_g.astype(F32), LANES // HEAD_DIM).reshape(1, LANES)

    conv_p, conv_s = [], []
    for i in range(depth):
        if i < n_a:
            xs, cs, conv_wb = _conv_mixer(xs, mix_g, conv_w_in, conv_w, conv_w_out, state_conv, i, tm=None, tn=tn_cast, seg=l)
            xp, cp = _conv_mixer(xp, mix_g, None, conv_w, conv_wb, None, i, tm=tm_conv, tn=tn, seg=None)
            conv_p.append(cp)
            conv_s.append(cs)
        else:
            if i == n_a:
                kp, vp, kbp, vtp = _shared_kv(xp, kv_norm_g.reshape(1, d), w_kv_b, kg, rope_p, tm=tm_kv, dup=False)
                ks, vs, k2s, v2s = _shared_kv(xs, kv_norm_g.reshape(1, d), w_kv_b, kg, rope_s, tm=tm_kv, dup=True)
                cache2 = (_dup_heads(cache_k), _dup_heads(cache_v))
            j = i - n_a
            qg = jnp.tile(q_norm_g[j].astype(F32), LANES // HEAD_DIM).reshape(1, LANES)
            qg_t = jnp.broadcast_to((q_norm_g[j].astype(F32) * (SCALE * LOG2E))[:, None], (HEAD_DIM, LANES))
            sink_rows = jnp.broadcast_to((sinks[j].astype(F32) * LOG2E)[:, None], (sinks.shape[1], LANES))
            xp = _attn_mixer_prompt(xp, mix_g, w_qt_b, qg_t, cos_t, sin_t, kbp, vtp, sink_rows, w_o_b, i, j, tm=tm_attn)
            xs = _attn_mixer_sample(xs, mix_g, w_q_b, qg, rope_s, k2s, v2s, cache2, _sink_column(sinks[j], l),
                                    w_o_b, i, j, tm=tm_attn_s, seg=l)
        xs, w_up_b, w_down_b = _mlp_cast(xs, mlp_g, w_up, w_down, i, tf=tf_cast)
        xp = _mlp(xp, mlp_g, w_up_b, w_down_b, i, tm=tm_mlp, tf=tf)

    hd = (N_KV_HEADS, HEAD_DIM)
    ks_new = ks.reshape(b, l, *hd)
    vs_new = vs.reshape(b, l, *hd)
    return (xp.reshape(1, s, d), xs.reshape(b, l, d), jnp.stack(conv_p), jnp.stack(conv_s),
            kp[-WINDOW:].reshape(1, WINDOW, *hd), vp[-WINDOW:].reshape(1, WINDOW, *hd),
            jnp.concatenate([cache_k[:, l:], ks_new], axis=1), jnp.concatenate([cache_v[:, l:], vs_new], axis=1))


def kernel(x_prompt, x_sample, state_conv, cache_k, cache_v, mix_norm_g, mlp_norm_g, w_up, w_down, conv_w_in, conv_w, conv_w_out, kv_norm_g, w_kv, k_norm_g, w_q, q_norm_g, sinks, w_o):
    return _forward(x_prompt, x_sample, state_conv, cache_k, cache_v, mix_norm_g, mlp_norm_g, w_up, w_down,
                    conv_w_in, conv_w, conv_w_out, kv_norm_g, w_kv, k_norm_g, w_q, q_norm_g, sinks, w_o,
                    tm_mlp=512, tf=2048, tf_cast=512, tm_conv=512, tn=1024, tn_cast=256, tm_attn=512, tm_attn_s=256, tm_kv=1024)
```

```python
import functools

import jax
import jax.numpy as jnp
from jax import lax
from jax.experimental import pallas as pl
from jax.experimental.pallas import tpu as pltpu

EPS = 1e-6
CHUNK = 64
WINDOW = 128
HEAD_DIM = 64
N_KV_HEADS = 4
GROUP = 8
ROT_DIM = 16
ROPE_THETA = 500000.0
PAST_LEN = 2048
SCALE = HEAD_DIM ** -0.5
LOG2E = 1.4426950408889634

LANES = 128
KEYS = 2 * WINDOW
KV_ROW_GROUP = 256
WEIGHT_RING_SLOTS = 3
CONV_SUBCHUNK = 256
VMEM_LIMIT_CAP = 56 * 2 ** 20

F32 = jnp.float32
BF16 = jnp.bfloat16


def _vmem_limit(nbytes):
    return int(min(VMEM_LIMIT_CAP, max(32 * 2 ** 20, nbytes * 5 // 4 + 4 * 2 ** 20)))


def _rms(x, g):
    return x * lax.rsqrt(jnp.mean(x * x, axis=-1, keepdims=True) + EPS) * g


def _half_mask(shape):
    return lax.broadcasted_iota(jnp.int32, shape, len(shape) - 1) < HEAD_DIM


def _head_norm_rope(t, gain, cos, sneg, spos):
    lo = _half_mask(t.shape)
    sq = t * t
    s_lo = jnp.sum(jnp.where(lo, sq, 0.0), axis=-1, keepdims=True)
    s_hi = jnp.sum(jnp.where(lo, 0.0, sq), axis=-1, keepdims=True)
    inv = jnp.where(lo, lax.rsqrt(s_lo / HEAD_DIM + EPS), lax.rsqrt(s_hi / HEAD_DIM + EPS))
    tn = t * inv * gain
    half = ROT_DIM // 2
    return tn * cos + pltpu.roll(tn, LANES - half, 1) * sneg + pltpu.roll(tn, half, 1) * spos


def _mlp_body(x_ref, g_ref, wu_ref, wd_ref, o_ref, xn_ref):
    def contribution():
        h = jnp.dot(xn_ref[...], wu_ref[...], preferred_element_type=F32)
        h = jnp.square(jnp.maximum(h, 0.0)).astype(BF16)
        return jnp.dot(h, wd_ref[...], preferred_element_type=F32)

    @pl.when(pl.program_id(1) == 0)
    def _():
        xn_ref[...] = _rms(x_ref[...], g_ref[...]).astype(BF16)
        o_ref[...] = x_ref[...] + contribution()

    @pl.when(pl.program_id(1) > 0)
    def _():
        o_ref[...] += contribution()


def _mlp_cast_body(x_ref, g_ref, wu_hbm, wd_hbm, o_ref, wub_ref, wdb_ref, xn_ref, wu_buf, wd_buf, sem, *, layer, tf, nf):
    j = pl.program_id(0)

    def chunk_copies(k):
        slot = lax.rem(k, WEIGHT_RING_SLOTS)
        off = pl.multiple_of(k * tf, tf)
        return (pltpu.make_async_copy(wu_hbm.at[layer, :, pl.ds(off, tf)], wu_buf.at[slot], sem.at[0, slot]),
                pltpu.make_async_copy(wd_hbm.at[layer, pl.ds(off, tf), :], wd_buf.at[slot], sem.at[1, slot]))

    @pl.when(j == 0)
    def _():
        for k in range(min(WEIGHT_RING_SLOTS - 1, nf)):
            for c in chunk_copies(k):
                c.start()

    @pl.when(j + WEIGHT_RING_SLOTS - 1 < nf)
    def _():
        for c in chunk_copies(j + WEIGHT_RING_SLOTS - 1):
            c.start()

    for c in chunk_copies(j):
        c.wait()
    slot = lax.rem(j, WEIGHT_RING_SLOTS)

    def contribution():
        wu = wu_buf[slot].astype(BF16)
        wd = wd_buf[slot].astype(BF16)
        wub_ref[...] = wu
        wdb_ref[...] = wd
        h = jnp.dot(xn_ref[...], wu, preferred_element_type=F32)
        h = jnp.square(jnp.maximum(h, 0.0)).astype(BF16)
        return jnp.dot(h, wd, preferred_element_type=F32)

    @pl.when(pl.program_id(0) == 0)
    def _():
        xn_ref[...] = _rms(x_ref[...], g_ref[...]).astype(BF16)
        o_ref[...] = x_ref[...] + contribution()

    @pl.when(pl.program_id(0) > 0)
    def _():
        o_ref[...] += contribution()


def _mlp_cast(x, g, w_up, w_down, layer, *, tf):
    t, d = x.shape
    f = w_up.shape[2]
    tf = min(tf, f)
    nbytes = 2 * t * d * 4 + t * d * 2 + 2 * d * tf * (WEIGHT_RING_SLOTS * 4 + 2 * 2) + t * tf * 6
    return pl.pallas_call(
        functools.partial(_mlp_cast_body, layer=layer, tf=tf, nf=f // tf),
        grid=(f // tf,),
        in_specs=[
            pl.BlockSpec((t, d), lambda j: (0, 0), pipeline_mode=pl.Buffered(1)),
            pl.BlockSpec((None, 1, d), lambda j: (layer, 0, 0)),
            pl.BlockSpec(memory_space=pl.ANY),
            pl.BlockSpec(memory_space=pl.ANY),
        ],
        out_specs=[
            pl.BlockSpec((t, d), lambda j: (0, 0), pipeline_mode=pl.Buffered(1)),
            pl.BlockSpec((d, tf), lambda j: (0, j)),
            pl.BlockSpec((tf, d), lambda j: (j, 0)),
        ],
        out_shape=[
            jax.ShapeDtypeStruct((t, d), F32),
            jax.ShapeDtypeStruct((d, f), BF16),
            jax.ShapeDtypeStruct((f, d), BF16),
        ],
        scratch_shapes=[pltpu.VMEM((t, d), BF16),
                        pltpu.VMEM((WEIGHT_RING_SLOTS, d, tf), F32), pltpu.VMEM((WEIGHT_RING_SLOTS, tf, d), F32),
                        pltpu.SemaphoreType.DMA((2, WEIGHT_RING_SLOTS))],
        compiler_params=pltpu.CompilerParams(
            dimension_semantics=("arbitrary",), vmem_limit_bytes=_vmem_limit(nbytes)),
        name="mlp_cast",
    )(x, g, w_up, w_down)


def _mlp(x, g, w_up, w_down, layer, *, tm, tf):
    t, d = x.shape
    f = w_up.shape[1]
    tm, tf = min(tm, t), min(tf, f)
    nbytes = 2 * (2 * tm * d * 4 + 2 * d * tf * 2) + tm * d * 2 + tm * tf * 6
    return pl.pallas_call(
        _mlp_body,
        grid=(t // tm, f // tf),
        in_specs=[
            pl.BlockSpec((tm, d), lambda i, j: (i, 0)),
            pl.BlockSpec((None, 1, d), lambda i, j: (layer, 0, 0)),
            pl.BlockSpec((d, tf), lambda i, j: (0, j)),
            pl.BlockSpec((tf, d), lambda i, j: (j, 0)),
        ],
        out_specs=pl.BlockSpec((tm, d), lambda i, j: (i, 0)),
        out_shape=jax.ShapeDtypeStruct((t, d), F32),
        scratch_shapes=[pltpu.VMEM((tm, d), BF16)],
        compiler_params=pltpu.CompilerParams(
            dimension_semantics=("arbitrary", "arbitrary"), vmem_limit_bytes=_vmem_limit(nbytes)),
        name="mlp",
    )(x, g, w_up, w_down)


def _conv_body(*refs, seg, tn, cast):
    if seg is None:
        x_ref, g_ref, wb_ref, wc_ref, wu_ref, cw_ref, wout_ref, o_ref, zl_ref, *rest = refs
    else:
        x_ref, g_ref, wb_ref, wc_ref, wu_ref, cw_ref, wout_ref, st_ref, o_ref, zl_ref, *rest = refs
    xn_ref = rest[-1]
    i, j = pl.program_id(0), pl.program_id(1)

    if seg is None:
        @pl.when(i == 0)
        def _():
            zl_ref[j] = jnp.zeros((2, tn), F32)

    tm = x_ref.shape[0]
    tc = min(tn, CONV_SUBCHUNK)

    def weights():
        if not cast:
            return wb_ref, wc_ref, wu_ref, wout_ref
        ws = [r[...].astype(BF16) for r in (wb_ref, wc_ref, wu_ref, wout_ref)]
        for dst, w in zip(rest[:4], ws):
            dst[...] = w
        return ws

    def project(c, wb, wc, wu):
        cs = slice(c * tc, (c + 1) * tc)
        xn = xn_ref[...]
        z = jnp.dot(xn, wc[:, cs], preferred_element_type=F32) * jnp.dot(xn, wu[:, cs], preferred_element_type=F32)
        return jnp.dot(xn, wb[:, cs], preferred_element_type=F32), z

    def gated_conv(gate_b, z, c):
        cs = slice(c * tc, (c + 1) * tc)
        row = lax.broadcasted_iota(jnp.int32, z.shape, 0)
        r1 = pltpu.roll(z, 1, 0)
        r2 = pltpu.roll(z, 2, 0)
        if seg is None:
            prev = zl_ref[j, :, cs]
            p0, p1 = prev[0:1, :], prev[1:2, :]
            zl_ref[j, :, cs] = z[tm - 2:, :]
        else:
            nb = tm // seg
            st = st_ref[:, :, cs]
            p0 = jnp.broadcast_to(st[:, 0:1, :], (nb, seg, tc)).reshape(tm, tc)
            p1 = jnp.broadcast_to(st[:, 1:2, :], (nb, seg, tc)).reshape(tm, tc)
            row = row % seg
            zl_ref[:, :, cs] = z.reshape(nb, seg, tc)[:, seg - 2:, :]
        zp1 = jnp.where(row == 0, p1, r1)
        zp2 = jnp.where(row == 0, p0, jnp.where(row == 1, p1, r2))
        cw = cw_ref[:, cs]
        conv = zp2 * cw[0:1, :] + zp1 * cw[1:2, :] + z * cw[2:3, :]
        return (gate_b * conv).astype(BF16)

    def contribution():
        nc = tn // tc
        wb, wc, wu, wout = weights()
        ys = []
        pending = project(0, wb, wc, wu)
        for c in range(nc):
            nxt = project(c + 1, wb, wc, wu) if c + 1 < nc else None
            ys.append(gated_conv(*pending, c))
            pending = nxt
        return jnp.dot(jnp.concatenate(ys, axis=1), wout[...], preferred_element_type=F32)

    @pl.when(j == 0)
    def _():
        xn_ref[...] = _rms(x_ref[...], g_ref[...]).astype(BF16)
        o_ref[...] = x_ref[...] + contribution()

    @pl.when(j > 0)
    def _():
        o_ref[...] += contribution()


def _conv_mixer(x, g, w_in, cw, w_out, state, layer, *, tm, tn, seg):
    t, d = x.shape
    cast = w_in is not None
    tm = t if cast else min(tm, t)
    nj = d // tn
    once = dict(pipeline_mode=pl.Buffered(1)) if cast else {}
    in_specs = [
        pl.BlockSpec((tm, d), lambda i, j: (i, 0), **once),
        pl.BlockSpec((None, 1, d), lambda i, j: (layer, 0, 0)),
    ]
    if cast:
        in_specs += [
            pl.BlockSpec((None, d, tn), lambda i, j: (layer, 0, j)),
            pl.BlockSpec((None, d, tn), lambda i, j: (layer, 0, nj + j)),
            pl.BlockSpec((None, d, tn), lambda i, j: (layer, 0, 2 * nj + j)),
            pl.BlockSpec((None, 3, tn), lambda i, j: (layer, 0, j)),
            pl.BlockSpec((None, tn, d), lambda i, j: (layer, j, 0)),
        ]
        args = [x, g, w_in, w_in, w_in, cw, w_out]
    else:
        in_specs += [
            pl.BlockSpec((d, tn), lambda i, j: (0, j)),
            pl.BlockSpec((d, tn), lambda i, j: (0, j)),
            pl.BlockSpec((d, tn), lambda i, j: (0, j)),
            pl.BlockSpec((None, 3, tn), lambda i, j: (layer, 0, j)),
            pl.BlockSpec((tn, d), lambda i, j: (j, 0)),
        ]
        args = [x, g, *w_out[:3], cw, w_out[3]]
    if seg is None:
        zl_shape = (nj, 2, tn)
        zl_spec = pl.BlockSpec((nj, 2, tn), lambda i, j: (0, 0, 0))
    else:
        nb = tm // seg
        zl_shape = (t // seg, 2, d)
        zl_spec = pl.BlockSpec((nb, 2, tn), lambda i, j: (i, 0, j))
        in_specs.append(pl.BlockSpec((None, nb, 2, tn), lambda i, j: (layer, i, 0, j)))
        args.append(state)
    out_specs = [pl.BlockSpec((tm, d), lambda i, j: (i, 0), **once), zl_spec]
    out_shape = [jax.ShapeDtypeStruct((t, d), F32), jax.ShapeDtypeStruct(zl_shape, F32)]
    if cast:
        out_specs += [pl.BlockSpec((d, tn), lambda i, j: (0, j))] * 3 + [pl.BlockSpec((tn, d), lambda i, j: (j, 0))]
        out_shape += [jax.ShapeDtypeStruct((d, d), BF16)] * 4
        nbytes = 2 * tm * d * 4 + 2 * 4 * d * tn * (4 + 2) + tm * d * 2 + tm * tn * 4 * 8
    else:
        nbytes = 2 * (2 * tm * d * 4 + d * 3 * tn * 2 + tn * d * 2) + tm * d * 2 + tm * tn * 4 * 8
    y, zl, *w_b = pl.pallas_call(
        functools.partial(_conv_body, seg=seg, tn=tn, cast=cast),
        grid=(t // tm, nj),
        in_specs=in_specs,
        out_specs=out_specs,
        out_shape=out_shape,
        scratch_shapes=[pltpu.VMEM((tm, d), BF16)],
        compiler_params=pltpu.CompilerParams(
            dimension_semantics=("arbitrary", "arbitrary"), vmem_limit_bytes=_vmem_limit(nbytes)),
        name="conv_mixer_cast" if cast else "conv_mixer",
    )(*args)
    if seg is None:
        zl = zl.transpose(1, 0, 2).reshape(1, 2, d)
    return (y, zl, tuple(w_b)) if cast else (y, zl)


def _kv_body(x_ref, g_ref, wkv_ref, kg_ref, cos_ref, sneg_ref, spos_ref, k_ref, v_ref, ka_ref, va_ref, *, dup):
    tm = x_ref.shape[0]
    nkv = N_KV_HEADS * HEAD_DIM
    gr = min(tm, KV_ROW_GROUP)
    lo = _half_mask((gr, LANES))

    def normed(r):
        return _rms(x_ref[r * gr:(r + 1) * gr, :], g_ref[...]).astype(BF16)

    xn = normed(0)
    for r in range(tm // gr):
        rs = slice(r * gr, (r + 1) * gr)
        xn_next = normed(r + 1) if (r + 1) * gr < tm else None
        kv = jnp.dot(xn, wkv_ref[...], preferred_element_type=F32)
        for p in range(nkv // LANES):
            sl = slice(p * LANES, (p + 1) * LANES)
            kr = _head_norm_rope(kv[:, sl], kg_ref[...], cos_ref[rs, :], sneg_ref[rs, :], spos_ref[rs, :])
            vr = kv[:, nkv + p * LANES: nkv + (p + 1) * LANES]
            k_ref[rs, sl] = kr
            v_ref[rs, sl] = vr
            if dup:
                for src, dst in ((kr, ka_ref), (vr, va_ref)):
                    sw = pltpu.roll(src, HEAD_DIM, 1)
                    dst[2 * p, rs, :] = jnp.where(lo, src, sw).astype(BF16)
                    dst[2 * p + 1, rs, :] = jnp.where(lo, sw, src).astype(BF16)
            else:
                ka_ref[rs, sl] = kr.astype(BF16)
                va_ref[sl, rs] = vr.T.astype(BF16)
        xn = xn_next


def _shared_kv(x, g, w_kv, kg, rope, *, tm, dup):
    t, d = x.shape
    tm = min(tm, t)
    nkv = N_KV_HEADS * HEAD_DIM
    row = lambda i: (i, 0)
    const = lambda i: (0, 0)
    if dup:
        f32_spec, f32_shape = pl.BlockSpec((tm, nkv), row), jax.ShapeDtypeStruct((t, nkv), F32)
        aux_specs = [pl.BlockSpec((N_KV_HEADS, tm, LANES), lambda i: (0, i, 0))] * 2
        aux_shapes = [jax.ShapeDtypeStruct((N_KV_HEADS, t, LANES), BF16)] * 2
    else:
        f32_spec, f32_shape = pl.BlockSpec((tm, nkv), const), jax.ShapeDtypeStruct((tm, nkv), F32)
        aux_specs = [pl.BlockSpec((tm, nkv), row), pl.BlockSpec((nkv, tm), lambda i: (0, i))]
        aux_shapes = [jax.ShapeDtypeStruct((t, nkv), BF16), jax.ShapeDtypeStruct((nkv, t), BF16)]
    nbytes = 2 * (tm * d * 4 + d * 2 * nkv * 2 + 5 * tm * nkv * 4) + tm * d * 8
    return pl.pallas_call(
        functools.partial(_kv_body, dup=dup),
        grid=(t // tm,),
        in_specs=[
            pl.BlockSpec((tm, d), row),
            pl.BlockSpec((1, d), const),
            pl.BlockSpec((d, 2 * nkv), const),
            pl.BlockSpec((1, LANES), const),
            pl.BlockSpec((tm, LANES), row),
            pl.BlockSpec((tm, LANES), row),
            pl.BlockSpec((tm, LANES), row),
        ],
        out_specs=[f32_spec, f32_spec] + aux_specs,
        out_shape=[f32_shape, f32_shape] + aux_shapes,
        compiler_params=pltpu.CompilerParams(
            dimension_semantics=("arbitrary",), vmem_limit_bytes=_vmem_limit(nbytes)),
        name="shared_kv",
    )(x, g, w_kv, kg, *rope)


def _project_q(x_ref, g_ref, wq_ref, qg_ref, cos_ref, sneg_ref, spos_ref, qe_ref, qo_ref):
    xn = _rms(x_ref[...], g_ref[...]).astype(BF16)
    q = jnp.dot(xn, wq_ref[...], preferred_element_type=F32)
    lo = _half_mask((q.shape[0], LANES))
    for p in range(q.shape[1] // LANES):
        sl = slice(p * LANES, (p + 1) * LANES)
        qr = _head_norm_rope(q[:, sl], qg_ref[...], cos_ref[...], sneg_ref[...], spos_ref[...]) * (SCALE * LOG2E)
        qe_ref[:, sl] = jnp.where(lo, qr, 0.0).astype(BF16)
        qo_ref[:, sl] = jnp.where(lo, 0.0, qr).astype(BF16)


def _attn_prompt_body(x_ref, g_ref, wqt_ref, qg_ref, cos_ref, sin_ref,
                      kprev_ref, kcur_ref, vprev_ref, vcur_ref, sink_ref, wo_ref,
                      o_ref, qt_ref, att_ref, kw_ref, vw_ref):
    i = pl.program_id(0)
    tm, d = x_ref.shape
    half = ROT_DIM // 2
    cos, sin = cos_ref[...], sin_ref[...]
    gain = jnp.concatenate([qg_ref[...]] * (tm // LANES), axis=1)
    rows = GROUP * HEAD_DIM

    nblk = tm // LANES
    dk = d // nblk

    x = x_ref[...]
    inv_rms = lax.rsqrt(jnp.mean(x * x, axis=-1, keepdims=True) + EPS)
    xn = {}

    def normed(c):
        if c not in xn:
            xn[c] = (x_ref[:, c * dk:(c + 1) * dk] * inv_rms * g_ref[:, c * dk:(c + 1) * dk]).astype(BF16)
        return xn[c]

    def project_q(kh, c):
        return lax.dot_general(wqt_ref[kh * rows:(kh + 1) * rows, c * dk:(c + 1) * dk], normed(c),
                               (((1,), (1,)), ((), ())), preferred_element_type=F32)

    def norm_rope_q(qt, kh, c):
        for j in range(c * (GROUP // nblk), (c + 1) * (GROUP // nblk)):
            t = qt[j * HEAD_DIM:(j + 1) * HEAD_DIM, :]
            tn = t * lax.rsqrt(jnp.sum(t * t, axis=0, keepdims=True) / HEAD_DIM + EPS) * gain
            x1, x2 = tn[:half], tn[half:ROT_DIM]
            h = GROUP * kh + j
            qt_ref[h * HEAD_DIM:(h + 1) * HEAD_DIM, :] = jnp.concatenate(
                [x1 * cos - x2 * sin, x2 * cos + x1 * sin, tn[ROT_DIM:]], axis=0).astype(BF16)


    kw_ref[:WINDOW, :] = kprev_ref[...]
    kw_ref[WINDOW:, :] = kcur_ref[...]
    vw_ref[:, :WINDOW] = vprev_ref[...]
    vw_ref[:, WINDOW:] = vcur_ref[...]

    first_query_chunk = lax.broadcasted_iota(jnp.int32, (CHUNK, LANES), 1) < CHUNK
    has_past = jnp.broadcast_to(i > 0, (CHUNK, LANES))
    zeros = jnp.zeros((HEAD_DIM, LANES), BF16)

    def scores_t(p, kh):
        qs = slice(p * LANES, (p + 1) * LANES)
        kwin = kw_ref[p * LANES:p * LANES + KEYS, (kh // 2) * LANES:(kh // 2 + 1) * LANES]
        rhs = jnp.concatenate(
            [jnp.concatenate([qt_ref[h * HEAD_DIM:(h + 1) * HEAD_DIM, qs], zeros] if kh % 2 == 0 else
                             [zeros, qt_ref[h * HEAD_DIM:(h + 1) * HEAD_DIM, qs]], axis=0)
             for h in range(GROUP * kh, GROUP * (kh + 1))], axis=1)
        return jnp.dot(kwin, rhs, preferred_element_type=F32)

    def softmax_t(st, p, kh):
        masks = [first_query_chunk & has_past if p == 0 else first_query_chunk, has_past if p == 0 else None,
                 None, ~first_query_chunk]
        sink_terms, pts = [], []
        for j in range(GROUP):
            s = jnp.concatenate(
                [st[c * CHUNK:(c + 1) * CHUNK, j * LANES:(j + 1) * LANES] if mask is None else
                 jnp.where(mask, st[c * CHUNK:(c + 1) * CHUNK, j * LANES:(j + 1) * LANES], -jnp.inf)
                 for c, mask in enumerate(masks)], axis=0)
            sink = sink_ref[GROUP * kh + j:GROUP * kh + j + 1, :]
            m = jnp.maximum(jnp.max(s, axis=0, keepdims=True), sink)
            sink_terms.append(jnp.exp2(sink - m))
            pts.append(jnp.exp2(s - m).astype(BF16))
        return jnp.concatenate(pts, axis=1), jnp.concatenate(sink_terms, axis=1)

    ones_rows = jnp.ones((16, KEYS), BF16)

    def weighted_values_t(p, kh, pt, sink_term):
        qs = slice(p * LANES, (p + 1) * LANES)
        v_ones = jnp.concatenate([vw_ref[kh * HEAD_DIM:(kh + 1) * HEAD_DIM, p * LANES:p * LANES + KEYS], ones_rows], axis=0)
        ot = jnp.dot(v_ones, pt, preferred_element_type=F32)
        ot = ot[:HEAD_DIM] * (1.0 / (ot[HEAD_DIM:HEAD_DIM + 1] + sink_term))
        for j in range(GROUP):
            h = GROUP * kh + j
            att_ref[h * HEAD_DIM:(h + 1) * HEAD_DIM, qs] = ot[:, j * LANES:(j + 1) * LANES].astype(BF16)

    def full_q(kh):
        qt = project_q(kh, 0)
        for c in range(1, nblk):
            qt = qt + project_q(kh, c)
        return qt

    qts = {0: full_q(0)}
    for c in range(nblk):
        piece = project_q(1, c)
        qts[1] = piece if c == 0 else qts[1] + piece
        norm_rope_q(qts[0], 0, c)
    for kh in range(N_KV_HEADS):
        st = scores_t(0, kh)
        pending = None
        for n in range(nblk):
            st_next = scores_t(n + 1, kh) if n + 1 < nblk else None
            if kh + 2 < N_KV_HEADS:
                piece = project_q(kh + 2, n)
                qts[kh + 2] = piece if n == 0 else qts[kh + 2] + piece
            pt, sink_term = softmax_t(st, n, kh)
            if kh + 1 < N_KV_HEADS:
                norm_rope_q(qts[kh + 1], kh + 1, n)
            if pending is not None:
                weighted_values_t(*pending)
            pending = (n, kh, pt, sink_term)
            st = st_next
        weighted_values_t(*pending)
    o_ref[...] = x_ref[...] + lax.dot_general(att_ref[...], wo_ref[...], (((0,), (0,)), ((), ())),
                                              preferred_element_type=F32)


def _attn_sample_body(x_ref, g_ref, wq_ref, qg_ref, cos_ref, sneg_ref, spos_ref,
                      kc_ref, knew_ref, vc_ref, vnew_ref, sink_ref, wo_ref,
                      o_ref, qe_ref, qo_ref, att_ref, *, seg):
    tm = x_ref.shape[0]
    _project_q(x_ref, g_ref, wq_ref, qg_ref, cos_ref, sneg_ref, spos_ref, qe_ref, qo_ref)
    nkeys = WINDOW + seg
    valid = lax.broadcasted_iota(jnp.int32, (1, KEYS), 1) < nkeys
    pad = jnp.zeros((KEYS - nkeys, LANES), BF16)
    pairs = GROUP // 2
    lo = _half_mask((seg, LANES))

    def window(cache_ref, new_ref, b, kh):
        return jnp.concatenate([cache_ref[b, kh], new_ref[kh, b * seg:(b + 1) * seg, :], pad], axis=0)

    def head_cols(kh):
        return [slice((pairs * kh + j) * LANES, (pairs * kh + j + 1) * LANES) for j in range(pairs)]

    def scores(b, kh):
        rows = slice(b * seg, (b + 1) * seg)
        qcat = jnp.concatenate([qe_ref[rows, c] for c in head_cols(kh)] + [qo_ref[rows, c] for c in head_cols(kh)], axis=0)
        return lax.dot_general(qcat, window(kc_ref, knew_ref, b, kh), (((1,), (1,)), ((), ())),
                               preferred_element_type=F32)

    def softmax(s, kh):
        sink = sink_ref[kh]
        s = jnp.where(valid, s, -jnp.inf)
        m = jnp.maximum(jnp.max(s, axis=-1, keepdims=True), sink)
        e = jnp.exp2(s - m)
        return e.astype(BF16), jnp.sum(e, axis=-1, keepdims=True) + jnp.exp2(sink - m)

    def weighted_values(b, kh, p, den):
        o = jnp.dot(p, window(vc_ref, vnew_ref, b, kh), preferred_element_type=F32) / den
        for j, c in enumerate(head_cols(kh)):
            att_ref[b * seg:(b + 1) * seg, c] = jnp.where(
                lo, o[j * seg:(j + 1) * seg], o[(pairs + j) * seg:(pairs + j + 1) * seg]).astype(BF16)

    blocks = [(b, kh) for b in range(tm // seg) for kh in range(N_KV_HEADS)]
    s = scores(*blocks[0])
    pending = None
    for n, blk in enumerate(blocks):
        s_next = scores(*blocks[n + 1]) if n + 1 < len(blocks) else None
        p, den = softmax(s, blk[1])
        if pending is not None:
            weighted_values(*pending)
        pending = (*blk, p, den)
        s = s_next
    weighted_values(*pending)
    o_ref[...] = x_ref[...] + jnp.dot(att_ref[...], wo_ref[...], preferred_element_type=F32)


def _attn_mixer_sample(x, g, w_q, qg, rope, k2, v2, cache, sink_col, w_o, layer, blayer, *, tm, seg):
    t, d = x.shape
    tm = min(tm, t)
    nb = tm // seg
    row = lambda i: (i, 0)
    const = lambda i: (0, 0)
    resident = dict(pipeline_mode=pl.Buffered(1))
    cur = pl.BlockSpec((N_KV_HEADS, tm, LANES), lambda i: (0, i, 0))
    cspec = pl.BlockSpec((nb, N_KV_HEADS, WINDOW, LANES), lambda i: (i, 0, 0, 0))
    in_specs = [
        pl.BlockSpec((tm, d), row),
        pl.BlockSpec((None, 1, d), lambda i: (layer, 0, 0)),
        pl.BlockSpec((None, d, d), lambda i: (blayer, 0, 0), **resident),
        pl.BlockSpec((1, LANES), const),
        pl.BlockSpec((tm, LANES), row),
        pl.BlockSpec((tm, LANES), row),
        pl.BlockSpec((tm, LANES), row),
        cspec, cur, cspec, cur,
        pl.BlockSpec((N_KV_HEADS, GROUP * seg, 1), lambda i: (0, 0, 0)),
        pl.BlockSpec((None, d, d), lambda i: (blayer, 0, 0), **resident),
    ]
    nbytes = 2 * d * d * 2 + 4 * tm * d * 4 + tm * d * (4 + 3 * 2) + 8 * tm * LANES * 4 * 2 + 4 * 2 ** 20
    return pl.pallas_call(
        functools.partial(_attn_sample_body, seg=seg),
        grid=(t // tm,),
        in_specs=in_specs,
        out_specs=pl.BlockSpec((tm, d), row),
        out_shape=jax.ShapeDtypeStruct((t, d), F32),
        scratch_shapes=[pltpu.VMEM((tm, d), BF16), pltpu.VMEM((tm, d), BF16), pltpu.VMEM((tm, d), BF16)],
        compiler_params=pltpu.CompilerParams(
            dimension_semantics=("arbitrary",), vmem_limit_bytes=_vmem_limit(nbytes)),
        name="attn_mixer_sample",
    )(x, g, w_q, qg, *rope, cache[0], k2, cache[1], v2, sink_col, w_o)


def _attn_mixer_prompt(x, g, w_qt, qg, cos_t, sin_t, kb, vt, sink_rows, w_o, layer, blayer, *, tm):
    t, d = x.shape
    tm = min(tm, t)
    nkv = N_KV_HEADS * HEAD_DIM
    nh = d // HEAD_DIM
    half = ROT_DIM // 2
    prev_blk = lambda i: jnp.maximum(i * (tm // WINDOW) - 1, 0)
    resident = dict(pipeline_mode=pl.Buffered(1))
    in_specs = [
        pl.BlockSpec((tm, d), lambda i: (i, 0)),
        pl.BlockSpec((None, 1, d), lambda i: (layer, 0, 0)),
        pl.BlockSpec((None, d, d), lambda i: (blayer, 0, 0), **resident),
        pl.BlockSpec((HEAD_DIM, LANES), lambda i: (0, 0)),
        pl.BlockSpec((half, tm), lambda i: (0, i)),
        pl.BlockSpec((half, tm), lambda i: (0, i)),
        pl.BlockSpec((WINDOW, nkv), lambda i: (prev_blk(i), 0)),
        pl.BlockSpec((tm, nkv), lambda i: (i, 0)),
        pl.BlockSpec((nkv, WINDOW), lambda i: (0, prev_blk(i))),
        pl.BlockSpec((nkv, tm), lambda i: (0, i)),
        pl.BlockSpec((nh, LANES), lambda i: (0, 0)),
        pl.BlockSpec((None, d, d), lambda i: (blayer, 0, 0), **resident),
    ]
    nbytes = (2 * d * d * 2 + 4 * tm * d * 4 + 3 * tm * d * 4 + 2 * tm * d * 2 + 2 * KEYS * GROUP * LANES * 2
              + KEYS * GROUP * LANES * 4 * 2 + 4 * (WINDOW + tm) * nkv * 2)
    return pl.pallas_call(
        _attn_prompt_body,
        grid=(t // tm,),
        in_specs=in_specs,
        out_specs=pl.BlockSpec((tm, d), lambda i: (i, 0)),
        out_shape=jax.ShapeDtypeStruct((t, d), F32),
        scratch_shapes=[pltpu.VMEM((d, tm), BF16), pltpu.VMEM((d, tm), BF16),
                        pltpu.VMEM((WINDOW + tm, nkv), BF16), pltpu.VMEM((nkv, WINDOW + tm), BF16)],
        compiler_params=pltpu.CompilerParams(
            dimension_semantics=("arbitrary",), vmem_limit_bytes=_vmem_limit(nbytes)),
        name="attn_mixer_prompt",
    )(x, g, w_qt, qg, cos_t, sin_t, kb, kb, vt, vt, sink_rows, w_o)


def _rope_tables(pos):
    half = ROT_DIM // 2
    inv = ROPE_THETA ** (-jnp.arange(half, dtype=F32) / half)
    ang = pos.astype(F32)[:, None] * inv[None, :]
    cos = jnp.tile(jnp.cos(ang), (1, LANES // half))
    sin = jnp.tile(jnp.sin(ang), (1, LANES // half))
    dim = jnp.arange(LANES) % HEAD_DIM
    c = jnp.where(dim < ROT_DIM, cos, 1.0)
    sneg = jnp.where(dim < half, -sin, 0.0)
    spos = jnp.where((dim >= half) & (dim < ROT_DIM), sin, 0.0)
    return c, sneg, spos


def _sink_column(sinks_l, rows_per_head):
    s = (sinks_l.astype(F32) * LOG2E).reshape(N_KV_HEADS, GROUP // 2, 2).transpose(0, 2, 1)
    return jnp.repeat(s.reshape(N_KV_HEADS, GROUP), rows_per_head, axis=1)[..., None]


def _dup_heads(t):
    t = t.transpose(0, 2, 1, 3)
    return jnp.concatenate([t, t], axis=-1).astype(BF16)


def _forward(x_prompt, x_sample, state_conv, cache_k, cache_v, mix_norm_g, mlp_norm_g, w_up, w_down,
             conv_w_in, conv_w, conv_w_out, kv_norm_g, w_kv, k_norm_g, w_q, q_norm_g, sinks, w_o,
             *, tm_mlp, tf, tf_cast, tm_conv, tn, tn_cast, tm_attn, tm_attn_s, tm_kv):
    _, s, d = x_prompt.shape
    b, l, _ = x_sample.shape
    n_a = conv_w_in.shape[0]
    depth = w_up.shape[0]
    xp = x_prompt.reshape(s, d)
    xs = x_sample.reshape(b * l, d)

    w_kv_b, w_q_b, w_o_b = w_kv.astype(BF16), w_q.astype(BF16), w_o.astype(BF16)
    w_qt_b = w_q_b.transpose(0, 2, 1)
    mix_g = mix_norm_g.reshape(depth, 1, d)
    mlp_g = mlp_norm_g.reshape(depth, 1, d)

    half = ROT_DIM // 2
    ang_t = (ROPE_THETA ** (-jnp.arange(half, dtype=F32) / half))[:, None] * jnp.arange(s).astype(F32)[None, :]
    cos_t, sin_t = jnp.cos(ang_t), jnp.sin(ang_t)
    rope_p = _rope_tables(jnp.arange(s))
    rope_s = _rope_tables(jnp.tile(PAST_LEN + jnp.arange(l), b))
    kg = jnp.tile(k_norm_g.astype(F32), LANES // HEAD_DIM).reshape(1, LANES)

    conv_p, conv_s = [], []
    for i in range(depth):
        if i < n_a:
            xs, cs, conv_wb = _conv_mixer(xs, mix_g, conv_w_in, conv_w, conv_w_out, state_conv, i, tm=None, tn=tn_cast, seg=l)
            xp, cp = _conv_mixer(xp, mix_g, None, conv_w, conv_wb, None, i, tm=tm_conv, tn=tn, seg=None)
            conv_p.append(cp)
            conv_s.append(cs)
        else:
            if i == n_a:
                kp, vp, kbp, vtp = _shared_kv(xp, kv_norm_g.reshape(1, d), w_kv_b, kg, rope_p, tm=tm_kv, dup=False)
                ks, vs, k2s, v2s = _shared_kv(xs, kv_norm_g.reshape(1, d), w_kv_b, kg, rope_s, tm=tm_kv, dup=True)
                cache2 = (_dup_heads(cache_k), _dup_heads(cache_v))
            j = i - n_a
            qg = jnp.tile(q_norm_g[j].astype(F32), LANES // HEAD_DIM).reshape(1, LANES)
            qg_t = jnp.broadcast_to((q_norm_g[j].astype(F32) * (SCALE * LOG2E))[:, None], (HEAD_DIM, LANES))
            sink_rows = jnp.broadcast_to((sinks[j].astype(F32) * LOG2E)[:, None], (sinks.shape[1], LANES))
            xp = _attn_mixer_prompt(xp, mix_g, w_qt_b, qg_t, cos_t, sin_t, kbp, vtp, sink_rows, w_o_b, i, j, tm=tm_attn)
            xs = _attn_mixer_sample(xs, mix_g, w_q_b, qg, rope_s, k2s, v2s, cache2, _sink_column(sinks[j], l),
                                    w_o_b, i, j, tm=tm_attn_s, seg=l)
        xs, w_up_b, w_down_b = _mlp_cast(xs, mlp_g, w_up, w_down, i, tf=tf_cast)
        xp = _mlp(xp, mlp_g, w_up_b, w_down_b, i, tm=tm_mlp, tf=tf)

    hd = (N_KV_HEADS, HEAD_DIM)
    ks_new = ks.reshape(b, l, *hd)
    vs_new = vs.reshape(b, l, *hd)
    return (xp.reshape(1, s, d), xs.reshape(b, l, d), jnp.stack(conv_p), jnp.stack(conv_s),
            kp[-WINDOW:].reshape(1, WINDOW, *hd), vp[-WINDOW:].reshape(1, WINDOW, *hd),
            jnp.concatenate([cache_k[:, l:], ks_new], axis=1), jnp.concatenate([cache_v[:, l:], vs_new], axis=1))


def kernel(x_prompt, x_sample, state_conv, cache_k, cache_v, mix_norm_g, mlp_norm_g, w_up, w_down, conv_w_in, conv_w, conv_w_out, kv_norm_g, w_kv, k_norm_g, w_q, q_norm_g, sinks, w_o):
    return _forward(x_prompt, x_sample, state_conv, cache_k, cache_v, mix_norm_g, mlp_norm_g, w_up, w_down,
                    conv_w_in, conv_w, conv_w_out, kv_norm_g, w_kv, k_norm_g, w_q, q_norm_g, sinks, w_o,
                    tm_mlp=512, tf=2048, tf_cast=512, tm_conv=512, tn=1024, tn_cast=256, tm_attn=512, tm_attn_s=256, tm_kv=1024)
```
